```python
import jax
import jax.numpy as jnp
from jax import lax
import numpy as np

D_MODEL = 1024
BATCH = 8
SEQ = 2048
DEPTH = 2

HEAD_DIM = 64
NORM_EPS = 1e-6
NEG = -1e30
BIG = 1e30
ATTN_BLOCK = 128

SSD_HEADS = 8
SSD_D_INNER = SSD_HEADS * HEAD_DIM
SSD_STATE = 128
SSD_GROUPS = 2
SSD_CONV = 4
SSD_CHUNK = 128
SSD_CONV_DIM = SSD_D_INNER + 2 * SSD_GROUPS * SSD_STATE

NSA_HEADS = 8
NSA_KV_HEADS = 2
NSA_CMP_BLOCK = 32
NSA_CMP_STRIDE = 16
NSA_CMP_HIDDEN = 64
NSA_SEL_BLOCK = 64
NSA_TOPK = 8
NSA_WINDOW = 512
NSA_SEL_QBLOCK = 64

RWKV_HEADS = 8
RWKV_DIM = RWKV_HEADS * HEAD_DIM
RWKV_W_LORA = 64
RWKV_A_LORA = 64
RWKV_G_LORA = 128
RWKV_IN = 3 * RWKV_DIM + RWKV_W_LORA + RWKV_A_LORA + RWKV_G_LORA
RWKV_GN_EPS = 64e-5

SWA_HEADS = 8
SWA_KV_HEADS = 2
SWA_WINDOW = 128
ROPE_THETA = 150000.0

N_BRANCHES = 4
D_FF = 2816
N_EXPERTS = 8
TOP_K = 2
MOE_BLOCK = 128
N_DENSE = (DEPTH + 1) // 2
N_MOE = DEPTH // 2
PLE_DIM = 256

IN_SPLITS = (SSD_D_INNER, SSD_CONV_DIM, SSD_HEADS,
             NSA_HEADS * HEAD_DIM, 6 * NSA_KV_HEADS * HEAD_DIM, 3 * NSA_HEADS,
             RWKV_IN,
             SWA_HEADS * HEAD_DIM, 2 * SWA_KV_HEADS * HEAD_DIM,
             N_BRANCHES * D_MODEL)
D_IN = sum(IN_SPLITS)

kernel_name = "hybrid_ssd_nsa_rwkv7_swa_moe"


def rmsnorm(x, g):
    xf = x.astype(jnp.float32)
    y = xf * lax.rsqrt(jnp.mean(xf * xf, -1, keepdims=True) + NORM_EPS)
    return (y * g.astype(jnp.float32)).astype(x.dtype)


def split_cols(u, widths):
    offsets = [int(o) for o in np.cumsum(widths)[:-1]]
    return jnp.split(u, offsets, axis=-1)


def swiglu(h, w_gate, w_up, w_down):
    return (jax.nn.silu(h @ w_gate) * (h @ w_up)) @ w_down


def causal_dwconv(x, w, b):
    K, C = w.shape
    y = lax.conv_general_dilated(x, w[:, None, :].astype(x.dtype), window_strides=(1,),
                                 padding=[(K - 1, 0)], dimension_numbers=('NWC', 'WIO', 'NWC'),
                                 feature_group_count=C)
    return y + b.astype(x.dtype)


def rope(x, positions):
    half = HEAD_DIM // 2
    inv_freq = ROPE_THETA ** (-jnp.arange(half, dtype=jnp.float32) / half)
    ang = positions.astype(jnp.float32)[..., None] * inv_freq
    ang = ang.reshape(ang.shape[:2] + (1,) * (x.ndim - 3) + (half,))
    cos, sin = jnp.cos(ang), jnp.sin(ang)
    xf = x.astype(jnp.float32)
    x1, x2 = xf[..., :half], xf[..., half:]
    return jnp.concatenate([x1 * cos - x2 * sin, x2 * cos + x1 * sin], -1).astype(x.dtype)


def banded_attention(q, k, v, window, sinks=None):
    Bsz, S, G, R, dh = q.shape
    nb = S // ATTN_BLOCK
    span = window + ATTN_BLOCK
    kp = jnp.pad(k, ((0, 0), (window, 0), (0, 0), (0, 0)))
    vp = jnp.pad(v, ((0, 0), (window, 0), (0, 0), (0, 0)))
    kidx = jnp.arange(nb)[:, None] * ATTN_BLOCK + jnp.arange(span)[None, :]
    kb = kp[:, kidx]
    vb = vp[:, kidx]
    qb = q.reshape(Bsz, nb, ATTN_BLOCK, G, R, dh)
    s = jnp.einsum('bnqgrd,bnkgd->bngrqk', qb, kb).astype(jnp.float32) * (dh ** -0.5)
    qpos = jnp.arange(S).reshape(nb, ATTN_BLOCK)
    kpos = kidx - window
    rel = qpos[:, :, None] - kpos[:, None, :]
    mask = (rel >= 0) & (rel < window) & (kpos[:, None, :] >= 0)
    mask = mask[None, :, None, None]
    s = jnp.where(mask, s, NEG)
    if sinks is None:
        pr = jax.nn.softmax(s, -1)
    else:
        sk = sinks.astype(jnp.float32).reshape(1, 1, G, R, 1, 1)
        m = jnp.maximum(s.max(-1, keepdims=True), sk)
        e = jnp.exp(s - m)
        pr = e / (e.sum(-1, keepdims=True) + jnp.exp(sk - m))
    o = jnp.einsum('bngrqk,bnkgd->bnqgrd', pr.astype(q.dtype), vb)
    return o.reshape(Bsz, S, G, R, dh)


def ssd_mixer(z, xbc, dt_raw, conv_w, conv_b, dt_bias, a_log, d_skip, norm_w):
    f32 = jnp.float32
    Bsz, S, _ = z.shape
    G, R, P, N, L = SSD_GROUPS, SSD_HEADS // SSD_GROUPS, HEAD_DIM, SSD_STATE, SSD_CHUNK
    nc = S // L
    xbc = jax.nn.silu(causal_dwconv(xbc, conv_w, conv_b))
    xs, Bm, Cm = split_cols(xbc, (SSD_D_INNER, G * N, G * N))
    X = xs.reshape(Bsz, nc, L, G, R, P).astype(f32)
    Bc = Bm.reshape(Bsz, nc, L, G, N).astype(f32)
    Cc = Cm.reshape(Bsz, nc, L, G, N).astype(f32)
    dt = jax.nn.softplus(dt_raw.astype(f32) + dt_bias.astype(f32)).reshape(Bsz, nc, L, G, R)
    A = -jnp.exp(a_log.astype(f32)).reshape(G, R)
    a_cum = jnp.cumsum(dt * A, axis=2)
    tril = jnp.tril(jnp.ones((L, L), bool))[None, None, :, :, None, None]
    seg = a_cum[:, :, :, None] - a_cum[:, :, None, :]
    decay = jnp.exp(jnp.where(tril, seg, -jnp.inf))
    cb = jnp.einsum('bclgn,bcsgn->bclsg', Cc, Bc)
    w_ls = cb[..., None] * decay * dt[:, :, None]
    y_diag = jnp.einsum('bclsgr,bcsgrp->bclgrp', w_ls, X)
    decay_to_end = jnp.exp(a_cum[:, :, -1:] - a_cum)
    states = jnp.einsum('bclgn,bclgr,bclgrp->bcgrpn', Bc, decay_to_end * dt, X)
    chunk_decay = jnp.exp(a_cum[:, :, -1])

    def step(hs, inp):
        st, dec = inp
        return hs * dec[..., None, None] + st, hs

    h0 = jnp.zeros((Bsz, G, R, P, N), f32)
    _, prev = lax.scan(step, h0, (jnp.moveaxis(states, 1, 0), jnp.moveaxis(chunk_decay, 1, 0)))
    prev = jnp.moveaxis(prev, 0, 1)
    y_off = jnp.einsum('bclgn,bcgrpn,bclgr->bclgrp', Cc, prev, jnp.exp(a_cum))
    y = y_diag + y_off + d_skip.astype(f32).reshape(G, R)[..., None] * X
    y = y.reshape(Bsz, S, SSD_D_INNER)
    yg = (y * jax.nn.silu(z.astype(f32))).reshape(Bsz, S, G, SSD_D_INNER // G)
    yg = yg * lax.rsqrt(jnp.mean(yg * yg, -1, keepdims=True) + NORM_EPS)
    return (yg.reshape(Bsz, S, SSD_D_INNER) * norm_w.astype(f32)).astype(z.dtype)


def compress_blocks(kv, pe, w1, w2):
    S = kv.shape[1]
    nc = (S - NSA_CMP_BLOCK) // NSA_CMP_STRIDE + 1
    idx = jnp.arange(nc)[:, None] * NSA_CMP_STRIDE + jnp.arange(NSA_CMP_BLOCK)[None, :]
    blk = kv[:, idx] + pe[None, None, :, None, :].astype(kv.dtype)
    hid = jax.nn.silu(jnp.einsum('bnlgd,ldf->bngf', blk, w1))
    return jnp.einsum('bngf,fd->bngd', hid, w2)


def nsa_mixer(q, kv, gates, cmp_pe, cmp_w1, cmp_w2):
    f32 = jnp.float32
    Bsz, S, _ = q.shape
    G, R, dh = NSA_KV_HEADS, NSA_HEADS // NSA_KV_HEADS, HEAD_DIM
    scale = dh ** -0.5
    q = q.reshape(Bsz, S, G, R, dh)
    k_c, v_c, k_s, v_s, k_w, v_w = [t.reshape(Bsz, S, G, dh) for t in jnp.split(kv, 6, axis=-1)]
    t_pos = jnp.arange(S)
    kc = compress_blocks(k_c, cmp_pe[0], cmp_w1[0], cmp_w2[0])
    vc = compress_blocks(v_c, cmp_pe[1], cmp_w1[1], cmp_w2[1])
    nc = kc.shape[1]
    cmp_end = jnp.arange(nc) * NSA_CMP_STRIDE + NSA_CMP_BLOCK - 1
    cmask = (cmp_end[None, :] <= t_pos[:, None])[None, :, None, None, :]
    s = jnp.einsum('bsgrd,bngd->bsgrn', q, kc).astype(f32) * scale
    s = jnp.where(cmask, s, NEG)
    p_cmp = jnp.where(cmask, jax.nn.softmax(s, -1), 0.0)
    o_cmp = jnp.einsum('bsgrn,bngd->bsgrd', p_cmp.astype(q.dtype), vc)
    n_sel = S // NSA_SEL_BLOCK
    c_start = jnp.arange(nc) * NSA_CMP_STRIDE
    s_start = jnp.arange(n_sel) * NSA_SEL_BLOCK
    overlap = ((c_start[:, None] < s_start[None, :] + NSA_SEL_BLOCK)
               & (c_start[:, None] + NSA_CMP_BLOCK > s_start[None, :])).astype(f32)
    imp = jnp.einsum('bsgrn,nj->bsgj', p_cmp, overlap)
    blk_of_t = t_pos // NSA_SEL_BLOCK
    j = jnp.arange(n_sel)
    valid = (j[None, :] <= blk_of_t[:, None])[None, :, None, :]
    forced = ((j[None, :] == 0) | (j[None, :] == blk_of_t[:, None]))[None, :, None, :]
    score = jnp.where(forced, BIG, jnp.where(valid, imp, NEG))
    k_eff = min(NSA_TOPK, n_sel)
    _, sel_idx = lax.top_k(score, k_eff)
    kb = k_s.reshape(Bsz, n_sel, NSA_SEL_BLOCK, G, dh).transpose(0, 3, 1, 2, 4)
    vb = v_s.reshape(Bsz, n_sel, NSA_SEL_BLOCK, G, dh).transpose(0, 3, 1, 2, 4)
    QB = NSA_SEL_QBLOCK
    nq = S // QB
    bi = jnp.arange(Bsz)[:, None, None, None]
    gi = jnp.arange(G)[None, None, :, None]

    def sel_block(args):
        qb, ib, tb = args
        kg = kb[bi, gi, ib]
        vg = vb[bi, gi, ib]
        sc = jnp.einsum('bqgrd,bqgkld->bqgrkl', qb, kg).astype(f32) * scale
        kpos = ib[..., None] * NSA_SEL_BLOCK + jnp.arange(NSA_SEL_BLOCK)
        m = (kpos <= tb[None, :, None, None, None])[:, :, :, None]
        sc = jnp.where(m, sc, NEG)
        shp = sc.shape
        pr = jax.nn.softmax(sc.reshape(shp[:4] + (-1,)), -1).reshape(shp)
        return jnp.einsum('bqgrkl,bqgkld->bqgrd', pr.astype(qb.dtype), vg)

    qs = q.reshape(Bsz, nq, QB, G, R, dh).swapaxes(0, 1)
    iss = sel_idx.reshape(Bsz, nq, QB, G, k_eff).swapaxes(0, 1)
    ts = t_pos.reshape(nq, QB)
    o_sel = lax.map(sel_block, (qs, iss, ts)).swapaxes(0, 1).reshape(Bsz, S, G, R, dh)
    o_win = banded_attention(q, k_w, v_w, NSA_WINDOW)
    g = jax.nn.sigmoid(gates).reshape(Bsz, S, 3, G, R, 1)
    o = g[:, :, 0] * o_cmp + g[:, :, 1] * o_sel + g[:, :, 2] * o_win
    return o.reshape(Bsz, S, NSA_HEADS * dh).astype(q.dtype)


def rwkv7_mixer(u, mu, w0, w_up, a0, a_up, g_up, k_k, k_a, r_k, ln_w, ln_b):
    f32 = jnp.float32
    Bsz, S, _ = u.shape
    H, N = RWKV_HEADS, HEAD_DIM
    prev = jnp.pad(u, ((0, 0), (1, 0), (0, 0)))[:, :-1]
    u = u + (prev - u) * mu.astype(u.dtype)
    r, k, v, wd, ad, gd = split_cols(u, (RWKV_DIM, RWKV_DIM, RWKV_DIM, RWKV_W_LORA, RWKV_A_LORA, RWKV_G_LORA))
    w = -jax.nn.softplus(-(w0 + jnp.tanh(wd) @ w_up).astype(f32)) - 0.5
    decay = jnp.exp(-jnp.exp(w))
    a = jax.nn.sigmoid((a0 + ad @ a_up).astype(f32))
    g = jax.nn.sigmoid(gd) @ g_up
    r = r.astype(f32)
    k = k.astype(f32)
    v = v.astype(f32)
    kk = (k * k_k.astype(f32)).reshape(Bsz, S, H, N)
    kk = kk / jnp.maximum(jnp.sqrt(jnp.sum(kk * kk, -1, keepdims=True)), 1e-12)
    k = k * (1.0 + (a - 1.0) * k_a.astype(f32))
    heads = lambda t: t.reshape(Bsz, S, H, N)
    r, k, v, decay, a = heads(r), heads(k), heads(v), heads(decay), heads(a)

    def step(state, inp):
        r_t, w_t, k_t, v_t, kk_t, a_t = inp
        sa = jnp.einsum('bhvk,bhk->bhv', state, -kk_t)
        state = (state * w_t[:, :, None, :] + sa[..., None] * (kk_t * a_t)[:, :, None, :]
                 + v_t[..., None] * k_t[:, :, None, :])
        return state, jnp.einsum('bhvk,bhk->bhv', state, r_t)

    xs = tuple(jnp.moveaxis(t, 1, 0) for t in (r, decay, k, v, kk, a))
    _, o = lax.scan(step, jnp.zeros((Bsz, H, N, N), f32), xs)
    o = jnp.moveaxis(o, 0, 1)
    mean = o.mean(-1, keepdims=True)
    var = jnp.mean((o - mean) ** 2, -1, keepdims=True)
    o = ((o - mean) * lax.rsqrt(var + RWKV_GN_EPS)).reshape(Bsz, S, RWKV_DIM)
    o = o * ln_w.astype(f32) + ln_b.astype(f32)
    bonus = jnp.sum(r * k * r_k.astype(f32).reshape(H, N), -1, keepdims=True) * v
    o = o + bonus.reshape(Bsz, S, RWKV_DIM)
    return (o * g.astype(f32)).astype(u.dtype)


def swa_mixer(q, kv, positions, sinks):
    Bsz, S, _ = q.shape
    G, R = SWA_KV_HEADS, SWA_HEADS // SWA_KV_HEADS
    q = rope(q.reshape(Bsz, S, G, R, HEAD_DIM), positions)
    k, v = jnp.split(kv, 2, axis=-1)
    k = rope(k.reshape(Bsz, S, G, HEAD_DIM), positions)
    v = v.reshape(Bsz, S, G, HEAD_DIM)
    o = banded_attention(q, k, v, SWA_WINDOW, sinks)
    return o.reshape(Bsz, S, SWA_HEADS * HEAD_DIM)


def moe_swiglu(h, router, w_gate, w_up, w_down):
    f32 = jnp.float32
    Bsz, S, D = h.shape
    T = Bsz * S
    TK = T * TOP_K
    xt = h.reshape(T, D)
    logits = (xt @ router).astype(f32)
    top_v, top_e = lax.top_k(logits, TOP_K)
    wts = jax.nn.softmax(top_v, -1)
    flat_e = top_e.reshape(-1)
    flat_t = jnp.repeat(jnp.arange(T, dtype=jnp.int32), TOP_K)
    flat_w = wts.reshape(-1)
    order = jnp.argsort(flat_e)
    se = flat_e[order]
    counts = jnp.bincount(flat_e, length=N_EXPERTS)
    starts = jnp.cumsum(counts) - counts
    pcounts = (counts + MOE_BLOCK - 1) // MOE_BLOCK * MOE_BLOCK
    pends = jnp.cumsum(pcounts)
    pstarts = pends - pcounts
    dest = pstarts[se] + jnp.arange(TK) - starts[se]
    nblk = -(-(TK + N_EXPERTS * (MOE_BLOCK - 1)) // MOE_BLOCK)
    P = nblk * MOE_BLOCK
    slot_tok = jnp.full((P,), T, jnp.int32).at[dest].set(flat_t[order])
    slot_w = jnp.zeros((P,), f32).at[dest].set(flat_w[order])
    blk_e = jnp.minimum(jnp.searchsorted(pends, jnp.arange(nblk) * MOE_BLOCK, side='right'), N_EXPERTS - 1)
    xpad = jnp.concatenate([xt, jnp.zeros((1, D), xt.dtype)], 0)
    xb = xpad[slot_tok].reshape(nblk, MOE_BLOCK, D)

    def expert_block(args):
        xs, e = args
        return swiglu(xs, w_gate[e], w_up[e], w_down[e])

    yb = lax.map(expert_block, (xb, blk_e)).reshape(P, D)
    yb = yb * slot_w[:, None].astype(yb.dtype)
    out = jnp.zeros((T + 1, D), yb.dtype).at[slot_tok].add(yb)[:T]
    return out.reshape(Bsz, S, D)


def setup_inputs(seed: int = 0) -> dict:
    key = jax.random.key(seed)
    keys = jax.random.split(key, 64)
    cnt = [0]

    def nk():
        cnt[0] += 1
        return keys[cnt[0] - 1]

    def nrm(shape, scale):
        return jax.random.normal(nk(), shape, jnp.float32) * scale

    def unif(shape, lo, hi):
        return jax.random.uniform(nk(), shape, jnp.float32, lo, hi)

    def gain(shape):
        return 1.0 + nrm(shape, 0.02)

    L = DEPTH
    dt = jnp.exp(unif((L, SSD_HEADS), float(np.log(1e-3)), float(np.log(1e-1))))
    return {
        "x": nrm((BATCH, SEQ, D_MODEL), 1.0),
        "p": nrm((DEPTH, BATCH, SEQ, PLE_DIM), 1.0),
        "positions": jax.random.randint(nk(), (BATCH, 1), 0, 4096, dtype=jnp.int32) + jnp.arange(SEQ, dtype=jnp.int32)[None, :],
        "norm_mix": gain((L, D_MODEL)),
        "w_in": nrm((L, D_MODEL, D_IN), D_MODEL ** -0.5),
        "ssd_conv_w": nrm((L, SSD_CONV, SSD_CONV_DIM), SSD_CONV ** -0.5),
        "ssd_conv_b": nrm((L, SSD_CONV_DIM), 0.02),
        "ssd_dt_bias": dt + jnp.log(-jnp.expm1(-dt)),
        "ssd_a_log": jnp.log(unif((L, SSD_HEADS), 1.0, 16.0)),
        "ssd_d": 1.0 + nrm((L, SSD_HEADS), 0.1),
        "ssd_norm": gain((L, SSD_D_INNER)),
        "nsa_cmp_pe": nrm((L, 2, NSA_CMP_BLOCK, HEAD_DIM), 0.1),
        "nsa_cmp_w1": nrm((L, 2, NSA_CMP_BLOCK, HEAD_DIM, NSA_CMP_HIDDEN), (NSA_CMP_BLOCK * HEAD_DIM) ** -0.5),
        "nsa_cmp_w2": nrm((L, 2, NSA_CMP_HIDDEN, HEAD_DIM), NSA_CMP_HIDDEN ** -0.5),
        "rwkv_mu": unif((L, RWKV_IN), 0.0, 1.0),
        "rwkv_w0": unif((L, RWKV_DIM), -6.0, -1.0),
        "rwkv_w_up": nrm((L, RWKV_W_LORA, RWKV_DIM), 0.1 * RWKV_W_LORA ** -0.5),
        "rwkv_a0": nrm((L, RWKV_DIM), 0.1),
        "rwkv_a_up": nrm((L, RWKV_A_LORA, RWKV_DIM), 0.1 * RWKV_A_LORA ** -0.5),
        "rwkv_g_up": nrm((L, RWKV_G_LORA, RWKV_DIM), RWKV_G_LORA ** -0.5),
        "rwkv_k_k": 0.85 + nrm((L, RWKV_DIM), 0.02),
        "rwkv_k_a": 1.0 + nrm((L, RWKV_DIM), 0.02),
        "rwkv_r_k": nrm((L, RWKV_DIM), 0.1),
        "rwkv_ln_w": gain((L, RWKV_DIM)),
        "rwkv_ln_b": nrm((L, RWKV_DIM), 0.02),
        "swa_sinks": nrm((L, SWA_HEADS), 0.5),
        "w_br_ssd": nrm((L, SSD_D_INNER, D_MODEL), SSD_D_INNER ** -0.5),
        "w_br_nsa": nrm((L, NSA_HEADS * HEAD_DIM, D_MODEL), (NSA_HEADS * HEAD_DIM) ** -0.5),
        "w_br_rwkv": nrm((L, RWKV_DIM, D_MODEL), RWKV_DIM ** -0.5),
        "w_br_swa": nrm((L, SWA_HEADS * HEAD_DIM, D_MODEL), (SWA_HEADS * HEAD_DIM) ** -0.5),
        "w_out": nrm((L, D_MODEL, D_MODEL), D_MODEL ** -0.5),
        "norm_ffn": gain((L, D_MODEL)),
        "ffn_w_gate": nrm((N_DENSE, D_MODEL, D_FF), D_MODEL ** -0.5),
        "ffn_w_up": nrm((N_DENSE, D_MODEL, D_FF), D_MODEL ** -0.5),
        "ffn_w_down": nrm((N_DENSE, D_FF, D_MODEL), D_FF ** -0.5),
        "moe_router": nrm((N_MOE, D_MODEL, N_EXPERTS), D_MODEL ** -0.5),
        "moe_w_gate": nrm((N_MOE, N_EXPERTS, D_MODEL, D_FF), D_MODEL ** -0.5),
        "moe_w_up": nrm((N_MOE, N_EXPERTS, D_MODEL, D_FF), D_MODEL ** -0.5),
        "moe_w_down": nrm((N_MOE, N_EXPERTS, D_FF, D_MODEL), D_FF ** -0.5),
        "ple_proj": nrm((L, PLE_DIM, D_MODEL), PLE_DIM ** -0.5),
        "ple_gate": nrm((L, D_MODEL, D_MODEL), D_MODEL ** -0.5),
        "norm_final": gain((D_MODEL,)),
    }


def reference(x, p, positions, norm_mix, w_in, ssd_conv_w, ssd_conv_b, ssd_dt_bias, ssd_a_log, ssd_d,
              ssd_norm, nsa_cmp_pe, nsa_cmp_w1, nsa_cmp_w2, rwkv_mu, rwkv_w0, rwkv_w_up, rwkv_a0, rwkv_a_up,
              rwkv_g_up, rwkv_k_k, rwkv_k_a, rwkv_r_k, rwkv_ln_w, rwkv_ln_b, swa_sinks, w_br_ssd, w_br_nsa,
              w_br_rwkv, w_br_swa, w_out, norm_ffn, ffn_w_gate, ffn_w_up, ffn_w_down, moe_router, moe_w_gate,
              moe_w_up, moe_w_down, ple_proj, ple_gate, norm_final):
    Bsz, S, _ = x.shape
    for i in range(DEPTH):
        h = rmsnorm(x, norm_mix[i])
        u = h @ w_in[i]
        (z, xbc, dt_raw, nsa_q, nsa_kv, nsa_g, rwkv_u, swa_q, swa_kv, gate_logits) = split_cols(u, IN_SPLITS)
        y_ssd = ssd_mixer(z, xbc, dt_raw, ssd_conv_w[i], ssd_conv_b[i], ssd_dt_bias[i], ssd_a_log[i],
                          ssd_d[i], ssd_norm[i])
        y_nsa = nsa_mixer(nsa_q, nsa_kv, nsa_g, nsa_cmp_pe[i], nsa_cmp_w1[i], nsa_cmp_w2[i])
        y_rwkv = rwkv7_mixer(rwkv_u, rwkv_mu[i], rwkv_w0[i], rwkv_w_up[i], rwkv_a0[i], rwkv_a_up[i],
                             rwkv_g_up[i], rwkv_k_k[i], rwkv_k_a[i], rwkv_r_k[i], rwkv_ln_w[i], rwkv_ln_b[i])
        y_swa = swa_mixer(swa_q, swa_kv, positions, swa_sinks[i])
        g = jax.nn.sigmoid(gate_logits).reshape(Bsz, S, N_BRANCHES, D_MODEL)
        merged = (g[:, :, 0] * (y_ssd @ w_br_ssd[i]) + g[:, :, 1] * (y_nsa @ w_br_nsa[i])
                  + g[:, :, 2] * (y_rwkv @ w_br_rwkv[i]) + g[:, :, 3] * (y_swa @ w_br_swa[i]))
        x = x + merged @ w_out[i]
        h = rmsnorm(x, norm_ffn[i])
        j = i // 2
        if i % 2 == 0:
            x = x + swiglu(h, ffn_w_gate[j], ffn_w_up[j], ffn_w_down[j])
        else:
            x = x + moe_swiglu(h, moe_router[j], moe_w_gate[j], moe_w_up[j], moe_w_down[j])
        x = x + jax.nn.sigmoid(x @ ple_gate[i]) * (p[i] @ ple_proj[i])
    return rmsnorm(x, norm_final)
```

```python
import functools

import numpy as np
import jax
import jax.numpy as jnp
from jax import lax
from jax.experimental import pallas as pl
from jax.experimental.pallas import tpu as pltpu

F32 = jnp.float32
BF16 = jnp.bfloat16

D_MODEL = 1024
HEAD_DIM = 64
NORM_EPS = 1e-6
NEG = -1e30
BIG = 1e30

SSD_HEADS = 8
SSD_D_INNER = SSD_HEADS * HEAD_DIM
SSD_STATE = 128
SSD_GROUPS = 2
SSD_CONV = 4
SSD_CHUNK = 128
SSD_CONV_DIM = SSD_D_INNER + 2 * SSD_GROUPS * SSD_STATE

NSA_HEADS = 8
NSA_KV_HEADS = 2
NSA_CMP_BLOCK = 32
NSA_CMP_STRIDE = 16
NSA_CMP_HIDDEN = 64
NSA_SEL_BLOCK = 64
NSA_TOPK = 8
NSA_WINDOW = 512

RWKV_HEADS = 8
RWKV_DIM = RWKV_HEADS * HEAD_DIM
RWKV_W_LORA = 64
RWKV_A_LORA = 64
RWKV_G_LORA = 128
RWKV_IN = 3 * RWKV_DIM + RWKV_W_LORA + RWKV_A_LORA + RWKV_G_LORA
RWKV_GN_EPS = 64e-5

SWA_HEADS = 8
SWA_KV_HEADS = 2
SWA_WINDOW = 128
ROPE_THETA = 150000.0

N_BRANCHES = 4
D_FF = 2816
N_EXPERTS = 8
TOP_K = 2
PLE_DIM = 256

LANES = 128
SUBLANES = 8
VMEM_LIMIT = 56 * 1024 * 1024

HIGHEST = lax.Precision.HIGHEST


def _cparams(sem):
    return pltpu.CompilerParams(dimension_semantics=sem, vmem_limit_bytes=VMEM_LIMIT)


def _sigmoid(x):
    return 1.0 / (1.0 + jnp.exp(-x))


def _silu(x):
    return x * _sigmoid(x)


def _softplus(x):
    return jnp.maximum(x, 0.0) + jnp.log1p(jnp.exp(-jnp.abs(x)))


def _dot(a, b):
    return jnp.dot(a, b, preferred_element_type=F32)


def _dot_nt(a, b):
    return lax.dot_general(a, b, (((1,), (1,)), ((), ())), preferred_element_type=F32)


def _pad_cols(w, n):
    return jnp.pad(w, ((0, 0), (0, n - w.shape[1])))


def _row_tile(t, pref):
    while t % pref:
        pref //= 2
    return pref


def _norm_matmul_kernel(x_ref, g_ref, w_ref, o_ref, h_ref):
    @pl.when(pl.program_id(1) == 0)
    def _():
        x = x_ref[...]
        ms = jnp.mean(x * x, axis=-1, keepdims=True)
        h_ref[...] = (x * lax.rsqrt(ms + NORM_EPS) * g_ref[...]).astype(BF16)

    o_ref[...] = _dot(h_ref[...], w_ref[...])


def _col_tile(n):
    if n <= 2048:
        return n
    for c in (2048, 1792, 1536, 1280, 1024, 768, 512, 384, 256, 128):
        if n % c == 0:
            return c
    raise ValueError(n)


def norm_matmul(x, g, w):
    t, d = x.shape
    n = w.shape[1]
    tm = _row_tile(t, 512)
    tn = _col_tile(n)
    return pl.pallas_call(
        _norm_matmul_kernel,
        name="norm_matmul",
        grid=(t // tm, n // tn),
        in_specs=[
            pl.BlockSpec((tm, d), lambda i, j: (i, 0)),
            pl.BlockSpec((1, d), lambda i, j: (0, 0)),
            pl.BlockSpec((d, tn), lambda i, j: (0, j)),
        ],
        out_specs=pl.BlockSpec((tm, tn), lambda i, j: (i, j)),
        out_shape=jax.ShapeDtypeStruct((t, n), F32),
        scratch_shapes=[pltpu.VMEM((tm, d), BF16)],
        compiler_params=_cparams(("parallel", "arbitrary")),
    )(x, g.reshape(1, d), w)


def _merge_kernel(x_ref, y0_ref, y1_ref, y2_ref, y3_ref, gl_ref, p_ref, wo_ref, o_ref):
    acc = None
    for m, y_ref in enumerate((y0_ref, y1_ref, y2_ref, y3_ref)):
        pm = _dot(y_ref[...].astype(BF16), p_ref[m])
        gm = _sigmoid(gl_ref[:, m * D_MODEL:(m + 1) * D_MODEL])
        acc = gm * pm if acc is None else acc + gm * pm
    o_ref[...] = x_ref[...] + _dot(acc.astype(BF16), wo_ref[...])


def merge_branches(x, ys, gate_logits, p_stack, w_out):
    t, d = x.shape
    tm = _row_tile(t, 256)
    dm = ys[0].shape[1]
    row = lambda w: pl.BlockSpec((tm, w), lambda i: (i, 0))
    return pl.pallas_call(
        _merge_kernel,
        name="merge",
        grid=(t // tm,),
        in_specs=[row(d), row(dm), row(dm), row(dm), row(dm), row(N_BRANCHES * d),
                  pl.BlockSpec((N_BRANCHES, dm, d), lambda i: (0, 0, 0)),
                  pl.BlockSpec((d, d), lambda i: (0, 0))],
        out_specs=row(d),
        out_shape=jax.ShapeDtypeStruct((t, d), F32),
        compiler_params=_cparams(("parallel",)),
    )(x, *ys, gate_logits, p_stack, w_out)


def _ffn_kernel(x_ref, g_ref, wg_ref, wu_ref, wd_ref, o_ref, h_ref, acc_ref):
    j = pl.program_id(1)

    @pl.when(j == 0)
    def _():
        x = x_ref[...]
        ms = jnp.mean(x * x, axis=-1, keepdims=True)
        h_ref[...] = (x * lax.rsqrt(ms + NORM_EPS) * g_ref[...]).astype(BF16)
        acc_ref[...] = jnp.zeros_like(acc_ref)

    h = h_ref[...]
    a = _silu(_dot(h, wg_ref[...])) * _dot(h, wu_ref[...])
    acc_ref[...] += _dot(a.astype(BF16), wd_ref[...])

    @pl.when(j == pl.num_programs(1) - 1)
    def _():
        o_ref[...] = x_ref[...] + acc_ref[...]


def dense_ffn(x, g, w_gate, w_up, w_down):
    t, d = x.shape
    f = w_gate.shape[1]
    tm = _row_tile(t, 512)
    tf = f // 2 if (f // 2) % LANES == 0 else f
    return pl.pallas_call(
        _ffn_kernel,
        name="dense_ffn",
        grid=(t // tm, f // tf),
        in_specs=[
            pl.BlockSpec((tm, d), lambda i, j: (i, 0)),
            pl.BlockSpec((1, d), lambda i, j: (0, 0)),
            pl.BlockSpec((d, tf), lambda i, j: (0, j)),
            pl.BlockSpec((d, tf), lambda i, j: (0, j)),
            pl.BlockSpec((tf, d), lambda i, j: (j, 0)),
        ],
        out_specs=pl.BlockSpec((tm, d), lambda i, j: (i, 0)),
        out_shape=jax.ShapeDtypeStruct((t, d), F32),
        scratch_shapes=[pltpu.VMEM((tm, d), BF16), pltpu.VMEM((tm, d), F32)],
        compiler_params=_cparams(("parallel", "arbitrary")),
    )(x, g.reshape(1, d), w_gate, w_up, w_down)


def _ple_kernel(x_ref, p_ref, wg_ref, wp_ref, o_ref):
    x = x_ref[...]
    gate = _sigmoid(_dot(x.astype(BF16), wg_ref[...]))
    o_ref[...] = x + gate * _dot(p_ref[...].astype(BF16), wp_ref[...])


def _ple_final_kernel(x_ref, p_ref, wg_ref, wp_ref, nf_ref, o_ref):
    x = x_ref[...]
    gate = _sigmoid(_dot(x.astype(BF16), wg_ref[...]))
    y = x + gate * _dot(p_ref[...].astype(BF16), wp_ref[...])
    ms = jnp.mean(y * y, axis=-1, keepdims=True)
    o_ref[...] = y * lax.rsqrt(ms + NORM_EPS) * nf_ref[...]


def ple(x, p, w_gate, w_proj, norm_final=None):
    t, d = x.shape
    pd = p.shape[1]
    tm = _row_tile(t, 512)
    in_specs = [pl.BlockSpec((tm, d), lambda i: (i, 0)),
                pl.BlockSpec((tm, pd), lambda i: (i, 0)),
                pl.BlockSpec((d, d), lambda i: (0, 0)),
                pl.BlockSpec((pd, d), lambda i: (0, 0))]
    args = [x, p, w_gate, w_proj]
    body = _ple_kernel
    if norm_final is not None:
        in_specs.append(pl.BlockSpec((1, d), lambda i: (0, 0)))
        args.append(norm_final.reshape(1, d))
        body = _ple_final_kernel
    return pl.pallas_call(
        body,
        name="ple",
        grid=(t // tm,),
        in_specs=in_specs,
        out_specs=pl.BlockSpec((tm, d), lambda i: (i, 0)),
        out_shape=jax.ShapeDtypeStruct((t, d), F32),
        compiler_params=_cparams(("parallel",)),
    )(*args)


SSD_U_COLS = SSD_D_INNER + SSD_CONV_DIM + LANES
_SSD_GN = SSD_GROUPS * SSD_STATE


def _ssd_kernel(u_ref, cw_ref, cb_ref, dtb_ref, alog_ref, dsk_ref, nw_ref, tril_ref,
                o_ref, xpad_ref, state_ref, y_ref):
    L = SSD_CHUNK
    P = HEAD_DIM
    R = SSD_HEADS // SSD_GROUPS

    @pl.when(pl.program_id(1) == 0)
    def _():
        xpad_ref[0:SUBLANES, :] = jnp.zeros((SUBLANES, SSD_CONV_DIM), F32)
        state_ref[...] = jnp.zeros_like(state_ref)

    z = u_ref[0, :, 0:SSD_D_INNER]
    xbc = u_ref[0, :, SSD_D_INNER:SSD_D_INNER + SSD_CONV_DIM]
    dt_raw = u_ref[0, :, SSD_D_INNER + SSD_CONV_DIM:SSD_U_COLS]

    xpad_ref[SUBLANES:SUBLANES + L, :] = xbc
    conv = cb_ref[...]
    for j in range(SSD_CONV):
        conv = conv + cw_ref[j:j + 1, :] * xpad_ref[pl.ds(SUBLANES - (SSD_CONV - 1) + j, L), :]
    xpad_ref[0:SUBLANES, :] = xbc[L - SUBLANES:L, :]
    act = _silu(conv)
    xs = act[:, 0:SSD_D_INNER]
    bm = act[:, SSD_D_INNER:SSD_D_INNER + _SSD_GN]
    cm = act[:, SSD_D_INNER + _SSD_GN:SSD_CONV_DIM]

    dt = _softplus(dt_raw + dtb_ref[...])
    a_neg = -jnp.exp(alog_ref[...])
    tril = tril_ref[...]
    a_cum = jnp.dot(tril, dt * a_neg, precision=HIGHEST, preferred_element_type=F32)
    a_cum_t = a_cum.T
    dt_t = dt.T
    lower = tril > 0.5

    for g in range(SSD_GROUPS):
        bg = bm[:, g * SSD_STATE:(g + 1) * SSD_STATE]
        cg = cm[:, g * SSD_STATE:(g + 1) * SSD_STATE]
        bg16 = bg.astype(BF16)
        cg16 = cg.astype(BF16)
        cb = _dot_nt(cg16, bg16)
        bgt16 = bg.T.astype(BF16)
        for r in range(R):
            h = g * R + r
            a_col = a_cum[:, h:h + 1]
            a_row = a_cum_t[h:h + 1, :]
            a_last = a_cum[L - 1:L, h:h + 1]
            xh = xs[:, h * P:(h + 1) * P]
            decay = jnp.where(lower, jnp.exp(jnp.where(lower, a_col - a_row, 0.0)), 0.0)
            w_ls = cb * decay * dt_t[h:h + 1, :]
            y_diag = _dot(w_ls.astype(BF16), xh.astype(BF16))
            xw = xh * (jnp.exp(a_last - a_col) * dt[:, h:h + 1])
            st = _dot(bgt16, xw.astype(BF16))
            prev = state_ref[h]
            y_off = _dot(cg16, prev.astype(BF16)) * jnp.exp(a_col)
            state_ref[h] = prev * jnp.exp(a_last) + st
            y_ref[:, h * P:(h + 1) * P] = y_diag + y_off + dsk_ref[:, h:h + 1] * xh

    yg = y_ref[...] * _silu(z)
    gw = SSD_D_INNER // SSD_GROUPS
    for g in range(SSD_GROUPS):
        part = yg[:, g * gw:(g + 1) * gw]
        ms = jnp.mean(part * part, axis=-1, keepdims=True)
        o_ref[0, :, g * gw:(g + 1) * gw] = part * lax.rsqrt(ms + NORM_EPS) * nw_ref[:, g * gw:(g + 1) * gw]


def _lane_row(v):
    return jnp.pad(v.astype(F32), (0, LANES - v.shape[0])).reshape(1, LANES)


def ssd_mixer(u, conv_w, conv_b, dt_bias, a_log, d_skip, norm_w):
    b, s, _ = u.shape
    L = SSD_CHUNK
    tril = jnp.asarray(np.tril(np.ones((L, L), np.float32)))
    full = lambda shape: pl.BlockSpec(shape, lambda i, c: (0,) * len(shape))
    return pl.pallas_call(
        _ssd_kernel,
        name="ssd",
        grid=(b, s // L),
        in_specs=[
            pl.BlockSpec((1, L, SSD_U_COLS), lambda i, c: (i, c, 0)),
            full((SSD_CONV, SSD_CONV_DIM)), full((1, SSD_CONV_DIM)),
            full((1, LANES)), full((1, LANES)), full((1, LANES)),
            full((1, SSD_D_INNER)), full((L, L)),
        ],
        out_specs=pl.BlockSpec((1, L, SSD_D_INNER), lambda i, c: (i, c, 0)),
        out_shape=jax.ShapeDtypeStruct((b, s, SSD_D_INNER), F32),
        scratch_shapes=[pltpu.VMEM((SUBLANES + L, SSD_CONV_DIM), F32),
                        pltpu.VMEM((SSD_HEADS, SSD_STATE, HEAD_DIM), F32),
                        pltpu.VMEM((L, SSD_D_INNER), F32)],
        compiler_params=_cparams(("parallel", "arbitrary")),
    )(u, conv_w, conv_b.reshape(1, -1), _lane_row(dt_bias), _lane_row(a_log), _lane_row(d_skip),
      norm_w.reshape(1, -1), tril)


ATTN_TQ = 128
_SCALE = HEAD_DIM ** -0.5


def _stack_heads(q, g, heads_per_group):
    parts = [q[:, (g * heads_per_group + r) * HEAD_DIM:(g * heads_per_group + r + 1) * HEAD_DIM]
             for r in range(heads_per_group)]
    return jnp.concatenate(parts, axis=0)


def _row_pos(t0, tq, reps):
    row = lax.broadcasted_iota(jnp.int32, (reps * tq, 1), 0)
    return t0 + (row & (tq - 1))


def _rope(x, cosf, sinf):
    n = x.shape[1]
    half = HEAD_DIM // 2
    lane = lax.broadcasted_iota(jnp.int32, x.shape, 1)
    first = (lane & (HEAD_DIM - 1)) < half
    rot = jnp.where(first, pltpu.roll(x, n - half, 1), pltpu.roll(x, half, 1))
    return x * cosf + rot * sinf


def _swa_kernel(q_ref, kv_ref, pos_ref, sink_ref, o_ref):
    tq = q_ref.shape[1]
    W = SWA_WINDOW
    span = W + tq
    R = SWA_HEADS // SWA_KV_HEADS
    kw = SWA_KV_HEADS * HEAD_DIM
    half = HEAD_DIM // 2
    t0 = pl.program_id(1) * tq
    start = pl.multiple_of(jnp.maximum(t0 - W, 0), SUBLANES)

    lane = lax.broadcasted_iota(jnp.int32, (1, LANES), 1)
    expo = -(lane & (half - 1)).astype(F32) / half
    inv_freq = jnp.power(jnp.full((1, LANES), ROPE_THETA, F32), expo)
    sign = jnp.where((lane & (HEAD_DIM - 1)) < half, -1.0, 1.0)

    def cos_sin(pos):
        ang = pos * inv_freq
        return jnp.cos(ang), jnp.sin(ang) * sign

    cq, sq = cos_sin(pos_ref[0, pl.ds(pl.multiple_of(t0, SUBLANES), tq), :])
    ck, sk = cos_sin(pos_ref[0, pl.ds(start, span), :])
    reps = q_ref.shape[2] // LANES
    qr = _rope(q_ref[0], jnp.concatenate([cq] * reps, 1), jnp.concatenate([sq] * reps, 1))
    kvs = kv_ref[0, pl.ds(start, span), :]
    kr = _rope(kvs[:, 0:kw], ck, sk)
    v = kvs[:, kw:2 * kw]

    qidx = _row_pos(t0, tq, R)
    kidx = start + lax.broadcasted_iota(jnp.int32, (1, span), 1)
    rel = qidx - kidx
    valid = (rel >= 0) & (rel < W)
    for g in range(SWA_KV_HEADS):
        qg = _stack_heads(qr, g, R).astype(BF16)
        s = _dot_nt(qg, kr[:, g * HEAD_DIM:(g + 1) * HEAD_DIM].astype(BF16)) * _SCALE
        s = jnp.where(valid, s, NEG)
        snk = jnp.concatenate([jnp.full((tq, 1), sink_ref[g * R + r], F32) for r in range(R)], 0)
        m = jnp.maximum(jnp.max(s, axis=-1, keepdims=True), snk)
        e = jnp.exp(s - m)
        pr = e / (jnp.sum(e, axis=-1, keepdims=True) + jnp.exp(snk - m))
        o = _dot(pr.astype(BF16), v[:, g * HEAD_DIM:(g + 1) * HEAD_DIM].astype(BF16))
        for r in range(R):
            h = g * R + r
            o_ref[0, :, h * HEAD_DIM:(h + 1) * HEAD_DIM] = o[r * tq:(r + 1) * tq]


def swa_mixer(u, positions, sinks):
    b, s, _ = u.shape
    tq = ATTN_TQ
    qw = SWA_HEADS * HEAD_DIM
    kvw = 2 * SWA_KV_HEADS * HEAD_DIM
    pos = positions.astype(F32).reshape(b, s, 1)
    return pl.pallas_call(
        _swa_kernel,
        name="swa",
        grid=(b, s // tq),
        in_specs=[
            pl.BlockSpec((1, tq, qw), lambda i, j: (i, j, 0)),
            pl.BlockSpec((1, s, kvw), lambda i, j: (i, 0, qw // kvw)),
            pl.BlockSpec((1, s, 1), lambda i, j: (i, 0, 0)),
            pl.BlockSpec(memory_space=pltpu.SMEM),
        ],
        out_specs=pl.BlockSpec((1, tq, qw), lambda i, j: (i, j, 0)),
        out_shape=jax.ShapeDtypeStruct((b, s, qw), F32),
        compiler_params=_cparams(("parallel", "arbitrary")),
    )(u, u, pos, sinks.astype(F32))


NSA_U_COLS = NSA_HEADS * HEAD_DIM + 6 * NSA_KV_HEADS * HEAD_DIM + LANES
_NSA_KVW = NSA_KV_HEADS * HEAD_DIM
_NSA_QBLK = NSA_HEADS * HEAD_DIM // _NSA_KVW
_NSA_R = NSA_HEADS // NSA_KV_HEADS
_CMP_HALF = NSA_CMP_BLOCK // 2


def _nsa_compress_kernel(k_ref, v_ref, pe_ref, w1_ref, w2_ref, kc_ref, vc_ref):
    nc = kc_ref.shape[1]
    for idx, (x_ref, o_ref) in enumerate(((k_ref, kc_ref), (v_ref, vc_ref))):
        ha = jnp.zeros((nc, _NSA_KVW), F32)
        hb = jnp.zeros((nc, _NSA_KVW), F32)
        for l in range(_CMP_HALF):
            y = x_ref[0, pl.ds(l, nc, stride=NSA_CMP_STRIDE), :]
            ha = ha + _dot((y + pe_ref[idx, l:l + 1, :]).astype(BF16), w1_ref[idx, l])
            hb = hb + _dot((y + pe_ref[idx, _CMP_HALF + l:_CMP_HALF + l + 1, :]).astype(BF16),
                           w1_ref[idx, _CMP_HALF + l])
        hid = _silu(ha + pltpu.roll(hb, nc - 1, 0))
        o_ref[0] = _dot(hid.astype(BF16), w2_ref[idx])


def _nsa_select_kernel(q_ref, kc_ref, vc_ref, ov_ref, ocmp_ref, sel_ref):
    tq = q_ref.shape[1]
    nc = kc_ref.shape[1]
    nsel = ov_ref.shape[0]
    k_eff = min(NSA_TOPK, nsel)
    t0 = pl.program_id(1) * tq
    t_row = _row_pos(t0, tq, _NSA_R)
    cmp_end = lax.broadcasted_iota(jnp.int32, (1, nc), 1) * NSA_CMP_STRIDE + (NSA_CMP_BLOCK - 1)
    cvalid = cmp_end <= t_row
    jidx = lax.broadcasted_iota(jnp.int32, (nsel, 1), 0)
    blk_t = (t0 + lax.broadcasted_iota(jnp.int32, (1, tq), 1)) // NSA_SEL_BLOCK
    jvalid = jidx <= blk_t
    forced = (jidx == 0) | (jidx == blk_t)
    q = q_ref[0]
    sel_rows = []
    for g in range(NSA_KV_HEADS):
        qg = _stack_heads(q, g, _NSA_R).astype(BF16)
        kc = kc_ref[0, :, g * HEAD_DIM:(g + 1) * HEAD_DIM].astype(BF16)
        vc = vc_ref[0, :, g * HEAD_DIM:(g + 1) * HEAD_DIM].astype(BF16)
        s = _dot_nt(qg, kc) * _SCALE
        m = jnp.max(jnp.where(cvalid, s, NEG), axis=-1, keepdims=True)
        e = jnp.exp(jnp.where(cvalid, s - m, NEG))
        den = jnp.sum(e, axis=-1, keepdims=True)
        p = e / jnp.where(den > 0.0, den, 1.0)
        o = _dot(p.astype(BF16), vc)
        psum = p[0:tq]
        for r in range(_NSA_R):
            h = g * _NSA_R + r
            ocmp_ref[0, :, h * HEAD_DIM:(h + 1) * HEAD_DIM] = o[r * tq:(r + 1) * tq]
            if r:
                psum = psum + p[r * tq:(r + 1) * tq]
        imp_t = lax.dot_general(ov_ref[...], psum, (((1,), (1,)), ((), ())),
                                precision=HIGHEST, preferred_element_type=F32)
        score = jnp.where(forced, BIG, jnp.where(jvalid, imp_t, NEG))
        cnt = jnp.zeros((nsel, tq), F32)
        for i in range(nsel):
            si = score[i:i + 1, :]
            beats = (si > score) | ((si == score) & (jidx > i))
            cnt = cnt + jnp.where(beats, 1.0, 0.0)
        sel_rows.append(jnp.where(cnt < k_eff, 1.0, 0.0))
    pad = LANES - NSA_KV_HEADS * nsel
    sel_t = jnp.concatenate(sel_rows + [jnp.zeros((pad, tq), F32)], axis=0)
    sel_ref[0] = sel_t.T


def _softmax_rows(s):
    m = jnp.max(s, axis=-1, keepdims=True)
    e = jnp.exp(s - m)
    return e / jnp.sum(e, axis=-1, keepdims=True)


def _nsa_attend_kernel(q_ref, sel_ref, gate_ref, ocmp_ref, ks_ref, vs_ref, kw_ref, vw_ref, ex_ref, o_ref):
    tq = q_ref.shape[1]
    s_len = ks_ref.shape[1]
    W = NSA_WINDOW
    span = min(W + tq, s_len)
    t0 = pl.program_id(1) * tq
    start = pl.multiple_of(jnp.maximum(t0 - W, 0), SUBLANES)
    t_row = _row_pos(t0, tq, _NSA_R)
    kidx = lax.broadcasted_iota(jnp.int32, (1, s_len), 1)
    causal = kidx <= t_row
    rel = t_row - (start + lax.broadcasted_iota(jnp.int32, (1, span), 1))
    in_win = (rel >= 0) & (rel < W)
    q = q_ref[0]
    sig = _sigmoid(gate_ref[0])
    sel16 = sel_ref[0].astype(BF16)
    for g in range(NSA_KV_HEADS):
        cols = slice(g * HEAD_DIM, (g + 1) * HEAD_DIM)
        qg = _stack_heads(q, g, _NSA_R).astype(BF16)
        chosen = _dot(sel16, ex_ref[g])
        chosen = jnp.concatenate([chosen] * _NSA_R, axis=0) > 0.5
        s = _dot_nt(qg, ks_ref[0, :, cols].astype(BF16)) * _SCALE
        p = _softmax_rows(jnp.where(chosen & causal, s, NEG))
        o_sel = _dot(p.astype(BF16), vs_ref[0, :, cols].astype(BF16))
        s2 = _dot_nt(qg, kw_ref[0, pl.ds(start, span), cols].astype(BF16)) * _SCALE
        p2 = _softmax_rows(jnp.where(in_win, s2, NEG))
        o_win = _dot(p2.astype(BF16), vw_ref[0, pl.ds(start, span), cols].astype(BF16))
        for r in range(_NSA_R):
            h = g * _NSA_R + r
            hc = slice(h * HEAD_DIM, (h + 1) * HEAD_DIM)
            rows = slice(r * tq, (r + 1) * tq)
            o_ref[0, :, hc] = (sig[:, h:h + 1] * ocmp_ref[0, :, hc]
                               + sig[:, NSA_HEADS + h:NSA_HEADS + h + 1] * o_sel[rows]
                               + sig[:, 2 * NSA_HEADS + h:2 * NSA_HEADS + h + 1] * o_win[rows])


def _block_diag2(w):
    z = jnp.zeros_like(w)
    return jnp.concatenate([jnp.concatenate([w, z], -1), jnp.concatenate([z, w], -1)], -2)


def nsa_mixer(u, cmp_pe, cmp_w1, cmp_w2):
    b, s, _ = u.shape
    tq = ATTN_TQ
    nc = s // NSA_CMP_STRIDE
    nsel = s // NSA_SEL_BLOCK
    qw = NSA_HEADS * HEAD_DIM
    pe2 = jnp.concatenate([cmp_pe, cmp_pe], -1)
    w1bd = _block_diag2(cmp_w1).astype(BF16)
    w2bd = _block_diag2(cmp_w2).astype(BF16)
    const = lambda shape: pl.BlockSpec(shape, lambda *_: (0,) * len(shape))
    col = lambda c: pl.BlockSpec((1, s, _NSA_KVW), lambda i, *_: (i, 0, _NSA_QBLK + c))

    kc, vc = pl.pallas_call(
        _nsa_compress_kernel,
        name="nsa_compress",
        grid=(b,),
        in_specs=[col(0), col(1), const(pe2.shape), const(w1bd.shape), const(w2bd.shape)],
        out_specs=[pl.BlockSpec((1, nc, _NSA_KVW), lambda i: (i, 0, 0))] * 2,
        out_shape=[jax.ShapeDtypeStruct((b, nc, _NSA_KVW), F32)] * 2,
        compiler_params=_cparams(("parallel",)),
    )(u, u, pe2, w1bd, w2bd)

    c_start = np.arange(nc) * NSA_CMP_STRIDE
    s_start = np.arange(nsel) * NSA_SEL_BLOCK
    ov_t = ((c_start[None, :] < s_start[:, None] + NSA_SEL_BLOCK)
            & (c_start[None, :] + NSA_CMP_BLOCK > s_start[:, None])
            & (np.arange(nc)[None, :] < nc - 1)).astype(np.float32)
    qspec = pl.BlockSpec((1, tq, qw), lambda i, j: (i, j, 0))
    o_cmp, sel = pl.pallas_call(
        _nsa_select_kernel,
        name="nsa_select",
        grid=(b, s // tq),
        in_specs=[qspec,
                  pl.BlockSpec((1, nc, _NSA_KVW), lambda i, j: (i, 0, 0)),
                  pl.BlockSpec((1, nc, _NSA_KVW), lambda i, j: (i, 0, 0)),
                  const(ov_t.shape)],
        out_specs=[qspec, pl.BlockSpec((1, tq, LANES), lambda i, j: (i, j, 0))],
        out_shape=[jax.ShapeDtypeStruct((b, s, qw), F32), jax.ShapeDtypeStruct((b, s, LANES), F32)],
        compiler_params=_cparams(("parallel", "arbitrary")),
    )(u, kc, vc, jnp.asarray(ov_t))

    expand = np.zeros((NSA_KV_HEADS, LANES, s), np.float32)
    for g in range(NSA_KV_HEADS):
        expand[g, g * nsel + np.arange(s) // NSA_SEL_BLOCK, np.arange(s)] = 1.0
    gate_blk = NSA_U_COLS // LANES - 1
    return pl.pallas_call(
        _nsa_attend_kernel,
        name="nsa_attend",
        grid=(b, s // tq),
        in_specs=[qspec,
                  pl.BlockSpec((1, tq, LANES), lambda i, j: (i, j, 0)),
                  pl.BlockSpec((1, tq, LANES), lambda i, j: (i, j, gate_blk)),
                  qspec,
                  col(2), col(3), col(4), col(5),
                  const(expand.shape)],
        out_specs=qspec,
        out_shape=jax.ShapeDtypeStruct((b, s, qw), F32),
        compiler_params=_cparams(("parallel", "arbitrary")),
    )(u, sel, u, o_cmp, u, u, u, u, jnp.asarray(expand, BF16))


def _seg_sum(x, ones16):
    hi = x.astype(BF16)
    lo = (x - hi.astype(F32)).astype(BF16)
    return _dot(hi, ones16) + _dot(lo, ones16)


def _head_ones(width):
    idx = np.arange(width) // HEAD_DIM
    return jnp.asarray((idx[:, None] == idx[None, :]).astype(np.float32), BF16)


_RW_R, _RW_K, _RW_V = 0, RWKV_DIM, 2 * RWKV_DIM
_RW_WD = 3 * RWKV_DIM
_RW_AD = _RW_WD + RWKV_W_LORA
_RW_GD = _RW_AD + RWKV_A_LORA


def _rwkv_prep_kernel(u_ref, mu_ref, w0_ref, wup_ref, a0_ref, aup_ref, gup_ref, kk_ref, ka_ref, rk_ref,
                      ones_ref, r_o, w_o, k_o, v_o, kk_o, q_o, g_o, bonus_o, up_ref):
    tm = u_ref.shape[1]

    @pl.when(pl.program_id(1) == 0)
    def _():
        up_ref[0:SUBLANES, :] = jnp.zeros((SUBLANES, RWKV_IN), F32)

    u = u_ref[0]
    up_ref[SUBLANES:SUBLANES + tm, :] = u
    prev = up_ref[pl.ds(SUBLANES - 1, tm), :]
    up_ref[0:SUBLANES, :] = u[tm - SUBLANES:tm, :]
    x = u + (prev - u) * mu_ref[...]
    r = x[:, _RW_R:_RW_R + RWKV_DIM]
    k = x[:, _RW_K:_RW_K + RWKV_DIM]
    v = x[:, _RW_V:_RW_V + RWKV_DIM]
    wd = x[:, _RW_WD:_RW_AD]
    ad = x[:, _RW_AD:_RW_GD]
    gd = x[:, _RW_GD:RWKV_IN]
    w = -_softplus(-(w0_ref[...] + _dot(jnp.tanh(wd).astype(BF16), wup_ref[...]))) - 0.5
    a = _sigmoid(a0_ref[...] + _dot(ad.astype(BF16), aup_ref[...]))
    ones16 = ones_ref[...]
    kk = k * kk_ref[...]
    kk = kk / jnp.maximum(jnp.sqrt(_seg_sum(kk * kk, ones16)), 1e-12)
    k2 = k * (1.0 + (a - 1.0) * ka_ref[...])
    r_o[0] = r
    w_o[0] = jnp.exp(-jnp.exp(w))
    k_o[0] = k2
    v_o[0] = v
    kk_o[0] = kk
    q_o[0] = kk * a
    g_o[0] = _dot(_sigmoid(gd).astype(BF16), gup_ref[...])
    bonus_o[0] = _seg_sum(r * k2 * rk_ref[...], ones16) * v


RWKV_TB = 128
_RW_SUB = HEAD_DIM
_RW_PAIRS = RWKV_HEADS // 2


def _rwkv_scan_kernel(r_ref, w_ref, k_ref, v_ref, kk_ref, q_ref, ones_ref, o_ref, s_ref, vt_ref, ot_ref):
    bb = r_ref.shape[0]
    half = HEAD_DIM

    @pl.when(pl.program_id(1) == 0)
    def _():
        s_ref[...] = jnp.zeros_like(s_ref)

    ones16 = ones_ref[...]
    lane = lax.broadcasted_iota(jnp.int32, (1, LANES), 1)
    chains = [(bi, p) for bi in range(bb) for p in range(_RW_PAIRS)]

    for sb in range(RWKV_TB // _RW_SUB):
        t_base = sb * _RW_SUB
        for c, (bi, p) in enumerate(chains):
            lanes = slice(p * LANES, (p + 1) * LANES)
            vt = v_ref[bi, :, lanes].T
            vt_ref[c] = jnp.concatenate([vt[0:half, t_base:t_base + _RW_SUB],
                                         vt[half:LANES, t_base:t_base + _RW_SUB]], axis=1)
            ot_ref[sb, c] = jnp.zeros((HEAD_DIM, LANES), F32)

        def step_group(jg, carry):
            t8 = pl.multiple_of(t_base + jg * SUBLANES, SUBLANES)
            for c, (bi, p) in enumerate(chains):
                lanes = slice(p * LANES, (p + 1) * LANES)
                rows = [ref[bi, pl.ds(t8, SUBLANES), lanes] for ref in (kk_ref, w_ref, q_ref, k_ref, r_ref)]
                st = s_ref[c]
                ot = ot_ref[sb, c]
                vt = vt_ref[c]
                for i in range(SUBLANES):
                    j = jg * SUBLANES + i
                    pick = (lane == j) | (lane == half + j)
                    kk_t, w_t, q_t, k_t, r_t = (x[i:i + 1, :] for x in rows)
                    sa = _seg_sum(st * kk_t, ones16)
                    vb = _seg_sum(jnp.where(pick, vt, 0.0), ones16)
                    st = st * w_t - sa * q_t + vb * k_t
                    o = _seg_sum(st * r_t, ones16)
                    ot = jnp.where(pick, o, ot)
                s_ref[c] = st
                ot_ref[sb, c] = ot
            return carry

        lax.fori_loop(0, _RW_SUB // SUBLANES, step_group, 0)

    for c, (bi, p) in enumerate(chains):
        subs = [ot_ref[sb, c] for sb in range(RWKV_TB // _RW_SUB)]
        top = jnp.concatenate([x[:, 0:half] for x in subs], axis=1)
        bot = jnp.concatenate([x[:, half:LANES] for x in subs], axis=1)
        o_ref[bi, :, p * LANES:(p + 1) * LANES] = jnp.concatenate([top, bot], axis=0).T


def _rwkv_post_kernel(o_ref, bonus_ref, g_ref, lnw_ref, lnb_ref, ones_ref, y_ref):
    ones16 = ones_ref[...]
    o = o_ref[...]
    mean = _seg_sum(o, ones16) * (1.0 / HEAD_DIM)
    cen = o - mean
    var = _seg_sum(cen * cen, ones16) * (1.0 / HEAD_DIM)
    o = cen * lax.rsqrt(var + RWKV_GN_EPS) * lnw_ref[...] + lnb_ref[...]
    y_ref[...] = (o + bonus_ref[...]) * g_ref[...]


def rwkv7_mixer(u, mu, w0, w_up, a0, a_up, g_up, k_k, k_a, r_k, ln_w, ln_b):
    b, s, _ = u.shape
    dm = RWKV_DIM
    tm = _row_tile(s, 256)
    row = lambda v: v.astype(F32).reshape(1, -1)
    const = lambda shape: pl.BlockSpec(shape, lambda *_: (0,) * len(shape))
    ones_d = _head_ones(dm)
    tile = pl.BlockSpec((1, tm, dm), lambda i, j: (i, j, 0))
    outs = pl.pallas_call(
        _rwkv_prep_kernel,
        name="rwkv_prep",
        grid=(b, s // tm),
        in_specs=[pl.BlockSpec((1, tm, RWKV_IN), lambda i, j: (i, j, 0)),
                  const((1, RWKV_IN)), const((1, dm)), const((RWKV_W_LORA, dm)), const((1, dm)),
                  const((RWKV_A_LORA, dm)), const((RWKV_G_LORA, dm)), const((1, dm)), const((1, dm)),
                  const((1, dm)), const((dm, dm))],
        out_specs=[tile] * 8,
        out_shape=[jax.ShapeDtypeStruct((b, s, dm), F32)] * 8,
        scratch_shapes=[pltpu.VMEM((SUBLANES + tm, RWKV_IN), F32)],
        compiler_params=_cparams(("parallel", "arbitrary")),
    )(u, row(mu), row(w0), w_up.astype(BF16), row(a0), a_up.astype(BF16), g_up.astype(BF16),
      row(k_k), row(k_a), row(r_k), ones_d)
    r, wdec, k2, v, kk, q, g, bonus = outs

    bb = 2 if b % 2 == 0 else 1
    nch = bb * _RW_PAIRS
    tb = RWKV_TB
    blk = pl.BlockSpec((bb, tb, dm), lambda i, j: (i, j, 0))
    o = pl.pallas_call(
        _rwkv_scan_kernel,
        name="rwkv_scan",
        grid=(b // bb, s // tb),
        in_specs=[blk] * 6 + [const((LANES, LANES))],
        out_specs=blk,
        out_shape=jax.ShapeDtypeStruct((b, s, dm), F32),
        scratch_shapes=[pltpu.VMEM((nch, HEAD_DIM, LANES), F32)] * 2
        + [pltpu.VMEM((tb // _RW_SUB, nch, HEAD_DIM, LANES), F32)],
        compiler_params=_cparams(("parallel", "arbitrary")),
    )(r, wdec, k2, v, kk, q, _head_ones(LANES))

    t = b * s
    tp = _row_tile(t, 512)
    flat = pl.BlockSpec((tp, dm), lambda i: (i, 0))
    y = pl.pallas_call(
        _rwkv_post_kernel,
        name="rwkv_post",
        grid=(t // tp,),
        in_specs=[flat, flat, flat, const((1, dm)), const((1, dm)), const((dm, dm))],
        out_specs=flat,
        out_shape=jax.ShapeDtypeStruct((t, dm), F32),
        compiler_params=_cparams(("parallel",)),
    )(o.reshape(t, dm), bonus.reshape(t, dm), g.reshape(t, dm), row(ln_w), row(ln_b), ones_d)
    return y.reshape(b, s, dm)


MOE_SLOT_BLOCK = 512
MOE_FF_CHUNK = 256
MOE_ROW_TILE = 256
_META_E0, _META_E1, _META_R0, _META_R1, _META_W0, _META_W1 = range(6)


def _moe_route_kernel(x_ref, g_ref, wr_ref, tril_ref, h_o, meta_o, cnt_o, carry_ref):
    @pl.when(pl.program_id(0) == 0)
    def _():
        carry_ref[...] = jnp.zeros_like(carry_ref)

    x = x_ref[...]
    ms = jnp.mean(x * x, axis=-1, keepdims=True)
    h = x * lax.rsqrt(ms + NORM_EPS) * g_ref[...]
    h_o[...] = h
    logits = jnp.dot(h, wr_ref[...], precision=HIGHEST, preferred_element_type=F32)
    lane = lax.broadcasted_iota(jnp.int32, logits.shape, 1)
    l1 = jnp.where(lane < N_EXPERTS, logits, NEG)
    m1 = jnp.max(l1, axis=-1, keepdims=True)
    i1 = jnp.min(jnp.where(l1 == m1, lane, LANES), axis=-1, keepdims=True)
    l2 = jnp.where(lane == i1, NEG, l1)
    m2 = jnp.max(l2, axis=-1, keepdims=True)
    i2 = jnp.min(jnp.where(l2 == m2, lane, LANES), axis=-1, keepdims=True)
    e21 = jnp.exp(m2 - m1)
    w1 = 1.0 / (1.0 + e21)
    w2 = e21 * w1
    cnt = jnp.where((lane == i1) | (lane == i2), 1.0, 0.0)
    before = _dot(tril_ref[...], cnt.astype(BF16)) + carry_ref[...]
    r1 = jnp.sum(jnp.where(lane == i1, before, 0.0), axis=-1, keepdims=True)
    r2 = jnp.sum(jnp.where(lane == i2, before, 0.0), axis=-1, keepdims=True)
    carry_ref[...] += jnp.sum(cnt, axis=0, keepdims=True)
    cnt_o[...] = carry_ref[...]
    meta = jnp.zeros(logits.shape, F32)
    for idx, val in ((_META_E0, i1.astype(F32)), (_META_E1, i2.astype(F32)), (_META_R0, r1),
                     (_META_R1, r2), (_META_W0, w1), (_META_W1, w2)):
        meta = jnp.where(lane == idx, val, meta)
    meta_o[...] = meta


def _row_copy(src_ref, src_row, dst_ref, dst_row, sem):
    return pltpu.make_async_copy(src_ref.at[pl.ds(src_row, 1), :], dst_ref.at[pl.ds(dst_row, 1), :], sem)


def _moe_scatter_kernel(d0_ref, d1_ref, h_ref, xs_in_ref, xs_ref, sem):
    del xs_in_ref
    tm = h_ref.shape[0]

    def issue(r, c):
        _row_copy(h_ref, r, xs_ref, d0_ref[r], sem).start()
        _row_copy(h_ref, r, xs_ref, d1_ref[r], sem).start()
        return c

    def drain(r, c):
        _row_copy(h_ref, r, xs_ref, d0_ref[r], sem).wait()
        _row_copy(h_ref, r, xs_ref, d1_ref[r], sem).wait()
        return c

    lax.fori_loop(0, tm, issue, 0)
    lax.fori_loop(0, tm, drain, 0)


def _moe_expert_kernel(be_ref, nused_ref, xs_ref, wg_ref, wu_ref, wd_ref, ys_ref):
    del be_ref
    i = pl.program_id(0)

    @pl.when(i < nused_ref[0])
    def _():
        x16 = xs_ref[...].astype(BF16)
        acc = jnp.zeros(ys_ref.shape, F32)
        for c in range(D_FF // MOE_FF_CHUNK):
            cols = slice(c * MOE_FF_CHUNK, (c + 1) * MOE_FF_CHUNK)
            a = _silu(_dot(x16, wg_ref[0, :, cols])) * _dot(x16, wu_ref[0, :, cols])
            acc = acc + _dot(a.astype(BF16), wd_ref[0, cols, :])
        ys_ref[...] = acc

    @pl.when(i >= nused_ref[0])
    def _():
        ys_ref[...] = jnp.zeros_like(ys_ref)


def _moe_combine_kernel(d0_ref, d1_ref, x_ref, meta_ref, ys_ref, o_ref, b0_ref, b1_ref, sem):
    tm = x_ref.shape[0]

    def issue(r, c):
        _row_copy(ys_ref, d0_ref[r], b0_ref, r, sem).start()
        _row_copy(ys_ref, d1_ref[r], b1_ref, r, sem).start()
        return c

    def drain(r, c):
        _row_copy(ys_ref, d0_ref[r], b0_ref, r, sem).wait()
        _row_copy(ys_ref, d1_ref[r], b1_ref, r, sem).wait()
        return c

    lax.fori_loop(0, tm, issue, 0)
    lax.fori_loop(0, tm, drain, 0)
    w0 = meta_ref[:, _META_W0:_META_W0 + 1]
    w1 = meta_ref[:, _META_W1:_META_W1 + 1]
    o_ref[...] = x_ref[...] + (w0 * b0_ref[...] + w1 * b1_ref[...])


def moe_ffn(x, g, router, w_gate, w_up, w_down):
    t, d = x.shape
    tm = _row_tile(t, MOE_ROW_TILE)
    blk = MOE_SLOT_BLOCK
    nblk = -(-(t * TOP_K + N_EXPERTS * (blk - 1)) // blk)
    slots = nblk * blk
    tril = jnp.asarray(np.tril(np.ones((tm, tm), np.float32), -1), BF16)
    const = lambda shape: pl.BlockSpec(shape, lambda *_: (0,) * len(shape))
    row = lambda w: pl.BlockSpec((tm, w), lambda i: (i, 0))
    h, meta, counts = pl.pallas_call(
        _moe_route_kernel,
        name="moe_route",
        grid=(t // tm,),
        in_specs=[row(d), const((1, d)), const((d, LANES)), const((tm, tm))],
        out_specs=[row(d), row(LANES), const((1, LANES))],
        out_shape=[jax.ShapeDtypeStruct((t, d), F32), jax.ShapeDtypeStruct((t, LANES), F32),
                   jax.ShapeDtypeStruct((1, LANES), F32)],
        scratch_shapes=[pltpu.VMEM((1, LANES), F32)],
        compiler_params=_cparams(("arbitrary",)),
    )(x, g.reshape(1, d), _pad_cols(router.astype(F32), LANES), tril)

    cnt = counts[0, :N_EXPERTS].astype(jnp.int32)
    pcnt = (cnt + blk - 1) // blk * blk
    pend = jnp.cumsum(pcnt)
    pstart = pend - pcnt
    e0 = meta[:, _META_E0].astype(jnp.int32)
    e1 = meta[:, _META_E1].astype(jnp.int32)
    d0 = pstart[e0] + meta[:, _META_R0].astype(jnp.int32)
    d1 = pstart[e1] + meta[:, _META_R1].astype(jnp.int32)
    blk_e = jnp.minimum(jnp.searchsorted(pend, jnp.arange(nblk, dtype=jnp.int32) * blk, side='right'),
                        N_EXPERTS - 1).astype(jnp.int32)
    nused = (pend[-1:] // blk).astype(jnp.int32)

    smem_rows = pl.BlockSpec((tm,), lambda i: (i,), memory_space=pltpu.SMEM)
    xs = pl.pallas_call(
        _moe_scatter_kernel,
        name="moe_scatter",
        grid=(t // tm,),
        in_specs=[smem_rows, smem_rows, row(d), pl.BlockSpec(memory_space=pl.ANY)],
        out_specs=pl.BlockSpec(memory_space=pl.ANY),
        out_shape=jax.ShapeDtypeStruct((slots, d), F32),
        scratch_shapes=[pltpu.SemaphoreType.DMA(())],
        input_output_aliases={3: 0},
        compiler_params=_cparams(("arbitrary",)),
    )(d0, d1, h, jnp.zeros((slots, d), F32))

    f = w_gate.shape[2]
    ys = pl.pallas_call(
        _moe_expert_kernel,
        name="moe_expert",
        grid_spec=pltpu.PrefetchScalarGridSpec(
            num_scalar_prefetch=2,
            grid=(nblk,),
            in_specs=[pl.BlockSpec((blk, d), lambda i, be, nu: (i, 0)),
                      pl.BlockSpec((1, d, f), lambda i, be, nu: (be[i], 0, 0)),
                      pl.BlockSpec((1, d, f), lambda i, be, nu: (be[i], 0, 0)),
                      pl.BlockSpec((1, f, d), lambda i, be, nu: (be[i], 0, 0))],
            out_specs=pl.BlockSpec((blk, d), lambda i, be, nu: (i, 0)),
        ),
        out_shape=jax.ShapeDtypeStruct((slots, d), F32),
        compiler_params=_cparams(("arbitrary",)),
    )(blk_e, nused, xs, w_gate, w_up, w_down)

    return pl.pallas_call(
        _moe_combine_kernel,
        name="moe_combine",
        grid=(t // tm,),
        in_specs=[smem_rows, smem_rows, row(d), row(LANES), pl.BlockSpec(memory_space=pl.ANY)],
        out_specs=row(d),
        out_shape=jax.ShapeDtypeStruct((t, d), F32),
        scratch_shapes=[pltpu.VMEM((tm, d), F32), pltpu.VMEM((tm, d), F32), pltpu.SemaphoreType.DMA(())],
        compiler_params=_cparams(("arbitrary",)),
    )(d0, d1, x, meta, ys)


_IN_SPLITS = (SSD_D_INNER, SSD_CONV_DIM, SSD_HEADS,
              NSA_HEADS * HEAD_DIM, 6 * NSA_KV_HEADS * HEAD_DIM, 3 * NSA_HEADS,
              RWKV_IN,
              SWA_HEADS * HEAD_DIM, 2 * SWA_KV_HEADS * HEAD_DIM,
              N_BRANCHES * D_MODEL)
_IN_OFF = tuple(int(o) for o in np.cumsum((0,) + _IN_SPLITS))


def _split_in_proj(w):
    o = _IN_OFF
    seg = lambda a, b: w[:, o[a]:o[b]]
    w_ssd = jnp.concatenate([seg(0, 2), _pad_cols(seg(2, 3), LANES)], axis=1)
    w_nsa = jnp.concatenate([seg(3, 5), _pad_cols(seg(5, 6), LANES)], axis=1)
    return tuple(m.astype(BF16) for m in (w_ssd, w_nsa, seg(6, 7), seg(7, 9), seg(9, 10)))


def kernel(x, p, positions, norm_mix, w_in, ssd_conv_w, ssd_conv_b, ssd_dt_bias, ssd_a_log, ssd_d, ssd_norm, nsa_cmp_pe, nsa_cmp_w1, nsa_cmp_w2, rwkv_mu, rwkv_w0, rwkv_w_up, rwkv_a0, rwkv_a_up, rwkv_g_up, rwkv_k_k, rwkv_k_a, rwkv_r_k, rwkv_ln_w, rwkv_ln_b, swa_sinks, w_br_ssd, w_br_nsa, w_br_rwkv, w_br_swa, w_out, norm_ffn, ffn_w_gate, ffn_w_up, ffn_w_down, moe_router, moe_w_gate, moe_w_up, moe_w_down, ple_proj, ple_gate, norm_final):
    b, s, d = x.shape
    t = b * s
    depth = w_in.shape[0]
    xf = x.reshape(t, d)
    for i in range(depth):
        w_ssd, w_nsa, w_rwkv, w_swa, w_gates = _split_in_proj(w_in[i])
        g_mix = norm_mix[i]
        u_ssd = norm_matmul(xf, g_mix, w_ssd).reshape(b, s, -1)
        u_nsa = norm_matmul(xf, g_mix, w_nsa).reshape(b, s, -1)
        u_rwkv = norm_matmul(xf, g_mix, w_rwkv).reshape(b, s, -1)
        u_swa = norm_matmul(xf, g_mix, w_swa).reshape(b, s, -1)
        gate_logits = norm_matmul(xf, g_mix, w_gates)
        y_ssd = ssd_mixer(u_ssd, ssd_conv_w[i], ssd_conv_b[i], ssd_dt_bias[i], ssd_a_log[i], ssd_d[i], ssd_norm[i])
        y_nsa = nsa_mixer(u_nsa, nsa_cmp_pe[i], nsa_cmp_w1[i], nsa_cmp_w2[i])
        y_rwkv = rwkv7_mixer(u_rwkv, rwkv_mu[i], rwkv_w0[i], rwkv_w_up[i], rwkv_a0[i], rwkv_a_up[i],
                             rwkv_g_up[i], rwkv_k_k[i], rwkv_k_a[i], rwkv_r_k[i], rwkv_ln_w[i], rwkv_ln_b[i])
        y_swa = swa_mixer(u_swa, positions, swa_sinks[i])
        p_stack = jnp.stack([w_br_ssd[i], w_br_nsa[i], w_br_rwkv[i], w_br_swa[i]]).astype(BF16)
        ys = [y.reshape(t, -1) for y in (y_ssd, y_nsa, y_rwkv, y_swa)]
        xf = merge_branches(xf, ys, gate_logits, p_stack, w_out[i].astype(BF16))
        j = i // 2
        if i % 2 == 0:
            xf = dense_ffn(xf, norm_ffn[i], ffn_w_gate[j].astype(BF16), ffn_w_up[j].astype(BF16),
                           ffn_w_down[j].astype(BF16))
        else:
            xf = moe_ffn(xf, norm_ffn[i], moe_router[j], moe_w_gate[j].astype(BF16),
                         moe_w_up[j].astype(BF16), moe_w_down[j].astype(BF16))
        xf = ple(xf, p[i].reshape(t, -1), ple_gate[i].astype(BF16), ple_proj[i].astype(BF16),
                 norm_final if i == depth - 1 else None)
    return xf.reshape(b, s, d)
```

```python
import functools

import numpy as np
import jax
import jax.numpy as jnp
from jax import lax
from jax.experimental import pallas as pl
from jax.experimental.pallas import tpu as pltpu

F32 = jnp.float32
BF16 = jnp.bfloat16

D_MODEL = 1024
HEAD_DIM = 64
NORM_EPS = 1e-6
NEG = -1e30
BIG = 1e30

SSD_HEADS = 8
SSD_D_INNER = SSD_HEADS * HEAD_DIM
SSD_STATE = 128
SSD_GROUPS = 2
SSD_CONV = 4
SSD_CHUNK = 128
SSD_CONV_DIM = SSD_D_INNER + 2 * SSD_GROUPS * SSD_STATE

NSA_HEADS = 8
NSA_KV_HEADS = 2
NSA_CMP_BLOCK = 32
NSA_CMP_STRIDE = 16
NSA_CMP_HIDDEN = 64
NSA_SEL_BLOCK = 64
NSA_TOPK = 8
NSA_WINDOW = 512

RWKV_HEADS = 8
RWKV_DIM = RWKV_HEADS * HEAD_DIM
RWKV_W_LORA = 64
RWKV_A_LORA = 64
RWKV_G_LORA = 128
RWKV_IN = 3 * RWKV_DIM + RWKV_W_LORA + RWKV_A_LORA + RWKV_G_LORA
RWKV_GN_EPS = 64e-5

SWA_HEADS = 8
SWA_KV_HEADS = 2
SWA_WINDOW = 128
ROPE_THETA = 150000.0

N_BRANCHES = 4
D_FF = 2816
N_EXPERTS = 8
TOP_K = 2
PLE_DIM = 256

LANES = 128
SUBLANES = 8
VMEM_LIMIT = 56 * 1024 * 1024

HIGHEST = lax.Precision.HIGHEST


def _cparams(sem):
    return pltpu.CompilerParams(dimension_semantics=sem, vmem_limit_bytes=VMEM_LIMIT)


def _sigmoid(x):
    return 1.0 / (1.0 + jnp.exp(-x))


def _silu(x):
    return x * _sigmoid(x)


def _softplus(x):
    return jnp.maximum(x, 0.0) + jnp.log1p(jnp.exp(-jnp.abs(x)))


def _dot(a, b):
    return jnp.dot(a, b, preferred_element_type=F32)


def _dot_nt(a, b):
    return lax.dot_general(a, b, (((1,), (1,)), ((), ())), preferred_element_type=F32)


def _pad_cols(w, n):
    return jnp.pad(w, ((0, 0), (0, n - w.shape[1])))


def _row_tile(t, pref):
    while t % pref:
        pref //= 2
    return pref


def _norm_matmul_kernel(x_ref, g_ref, w_ref, o_ref, h_ref):
    @pl.when(pl.program_id(1) == 0)
    def _():
        x = x_ref[...]
        ms = jnp.mean(x * x, axis=-1, keepdims=True)
        h_ref[...] = (x * lax.rsqrt(ms + NORM_EPS) * g_ref[...]).astype(BF16)

    o_ref[...] = _dot(h_ref[...], w_ref[...])


def _col_tile(n):
    if n <= 2048:
        return n
    for c in (2048, 1792, 1536, 1280, 1024, 768, 512, 384, 256, 128):
        if n % c == 0:
            return c
    raise ValueError(n)


def norm_matmul(x, g, w):
    t, d = x.shape
    n = w.shape[1]
    tm = _row_tile(t, 512)
    tn = _col_tile(n)
    return pl.pallas_call(
        _norm_matmul_kernel,
        name="norm_matmul",
        grid=(t // tm, n // tn),
        in_specs=[
            pl.BlockSpec((tm, d), lambda i, j: (i, 0)),
            pl.BlockSpec((1, d), lambda i, j: (0, 0)),
            pl.BlockSpec((d, tn), lambda i, j: (0, j)),
        ],
        out_specs=pl.BlockSpec((tm, tn), lambda i, j: (i, j)),
        out_shape=jax.ShapeDtypeStruct((t, n), F32),
        scratch_shapes=[pltpu.VMEM((tm, d), BF16)],
        compiler_params=_cparams(("parallel", "arbitrary")),
    )(x, g.reshape(1, d), w)


def _merge_kernel(x_ref, y0_ref, y1_ref, y2_ref, y3_ref, gl_ref, p_ref, wo_ref, o_ref):
    acc = None
    for m, y_ref in enumerate((y0_ref, y1_ref, y2_ref, y3_ref)):
        pm = _dot(y_ref[...].astype(BF16), p_ref[m])
        gm = _sigmoid(gl_ref[:, m * D_MODEL:(m + 1) * D_MODEL])
        acc = gm * pm if acc is None else acc + gm * pm
    o_ref[...] = x_ref[...] + _dot(acc.astype(BF16), wo_ref[...])


def merge_branches(x, ys, gate_logits, p_stack, w_out):
    t, d = x.shape
    tm = _row_tile(t, 256)
    dm = ys[0].shape[1]
    row = lambda w: pl.BlockSpec((tm, w), lambda i: (i, 0))
    return pl.pallas_call(
        _merge_kernel,
        name="merge",
        grid=(t // tm,),
        in_specs=[row(d), row(dm), row(dm), row(dm), row(dm), row(N_BRANCHES * d),
                  pl.BlockSpec((N_BRANCHES, dm, d), lambda i: (0, 0, 0)),
                  pl.BlockSpec((d, d), lambda i: (0, 0))],
        out_specs=row(d),
        out_shape=jax.ShapeDtypeStruct((t, d), F32),
        compiler_params=_cparams(("parallel",)),
    )(x, *ys, gate_logits, p_stack, w_out)


def _ffn_kernel(x_ref, g_ref, wg_ref, wu_ref, wd_ref, o_ref, h_ref, acc_ref):
    j = pl.program_id(1)

    @pl.when(j == 0)
    def _():
        x = x_ref[...]
        ms = jnp.mean(x * x, axis=-1, keepdims=True)
        h_ref[...] = (x * lax.rsqrt(ms + NORM_EPS) * g_ref[...]).astype(BF16)
        acc_ref[...] = jnp.zeros_like(acc_ref)

    h = h_ref[...]
    a = _silu(_dot(h, wg_ref[...])) * _dot(h, wu_ref[...])
    acc_ref[...] += _dot(a.astype(BF16), wd_ref[...])

    @pl.when(j == pl.num_programs(1) - 1)
    def _():
        o_ref[...] = x_ref[...] + acc_ref[...]


def dense_ffn(x, g, w_gate, w_up, w_down):
    t, d = x.shape
    f = w_gate.shape[1]
    tm = _row_tile(t, 512)
    tf = f // 2 if (f // 2) % LANES == 0 else f
    return pl.pallas_call(
        _ffn_kernel,
        name="dense_ffn",
        grid=(t // tm, f // tf),
        in_specs=[
            pl.BlockSpec((tm, d), lambda i, j: (i, 0)),
            pl.BlockSpec((1, d), lambda i, j: (0, 0)),
            pl.BlockSpec((d, tf), lambda i, j: (0, j)),
            pl.BlockSpec((d, tf), lambda i, j: (0, j)),
            pl.BlockSpec((tf, d), lambda i, j: (j, 0)),
        ],
        out_specs=pl.BlockSpec((tm, d), lambda i, j: (i, 0)),
        out_shape=jax.ShapeDtypeStruct((t, d), F32),
        scratch_shapes=[pltpu.VMEM((tm, d), BF16), pltpu.VMEM((tm, d), F32)],
        compiler_params=_cparams(("parallel", "arbitrary")),
    )(x, g.reshape(1, d), w_gate, w_up, w_down)


def _ple_kernel(x_ref, p_ref, wg_ref, wp_ref, o_ref):
    x = x_ref[...]
    gate = _sigmoid(_dot(x.astype(BF16), wg_ref[...]))
    o_ref[...] = x + gate * _dot(p_ref[...].astype(BF16), wp_ref[...])


def _ple_final_kernel(x_ref, p_ref, wg_ref, wp_ref, nf_ref, o_ref):
    x = x_ref[...]
    gate = _sigmoid(_dot(x.astype(BF16), wg_ref[...]))
    y = x + gate * _dot(p_ref[...].astype(BF16), wp_ref[...])
    ms = jnp.mean(y * y, axis=-1, keepdims=True)
    o_ref[...] = y * lax.rsqrt(ms + NORM_EPS) * nf_ref[...]


def ple(x, p, w_gate, w_proj, norm_final=None):
    t, d = x.shape
    pd = p.shape[1]
    tm = _row_tile(t, 512)
    in_specs = [pl.BlockSpec((tm, d), lambda i: (i, 0)),
                pl.BlockSpec((tm, pd), lambda i: (i, 0)),
                pl.BlockSpec((d, d), lambda i: (0, 0)),
                pl.BlockSpec((pd, d), lambda i: (0, 0))]
    args = [x, p, w_gate, w_proj]
    body = _ple_kernel
    if norm_final is not None:
        in_specs.append(pl.BlockSpec((1, d), lambda i: (0, 0)))
        args.append(norm_final.reshape(1, d))
        body = _ple_final_kernel
    return pl.pallas_call(
        body,
        name="ple",
        grid=(t // tm,),
        in_specs=in_specs,
        out_specs=pl.BlockSpec((tm, d), lambda i: (i, 0)),
        out_shape=jax.ShapeDtypeStruct((t, d), F32),
        compiler_params=_cparams(("parallel",)),
    )(*args)


SSD_U_COLS = SSD_D_INNER + SSD_CONV_DIM + LANES
_SSD_GN = SSD_GROUPS * SSD_STATE


def _ssd_kernel(u_ref, cw_ref, cb_ref, dtb_ref, alog_ref, dsk_ref, nw_ref, tril_ref,
                o_ref, xpad_ref, state_ref, y_ref):
    L = SSD_CHUNK
    P = HEAD_DIM
    R = SSD_HEADS // SSD_GROUPS

    @pl.when(pl.program_id(1) == 0)
    def _():
        xpad_ref[0:SUBLANES, :] = jnp.zeros((SUBLANES, SSD_CONV_DIM), F32)
        state_ref[...] = jnp.zeros_like(state_ref)

    z = u_ref[0, :, 0:SSD_D_INNER]
    xbc = u_ref[0, :, SSD_D_INNER:SSD_D_INNER + SSD_CONV_DIM]
    dt_raw = u_ref[0, :, SSD_D_INNER + SSD_CONV_DIM:SSD_U_COLS]

    xpad_ref[SUBLANES:SUBLANES + L, :] = xbc
    conv = cb_ref[...]
    for j in range(SSD_CONV):
        conv = conv + cw_ref[j:j + 1, :] * xpad_ref[pl.ds(SUBLANES - (SSD_CONV - 1) + j, L), :]
    xpad_ref[0:SUBLANES, :] = xbc[L - SUBLANES:L, :]
    act = _silu(conv)
    xs = act[:, 0:SSD_D_INNER]
    bm = act[:, SSD_D_INNER:SSD_D_INNER + _SSD_GN]
    cm = act[:, SSD_D_INNER + _SSD_GN:SSD_CONV_DIM]

    dt = _softplus(dt_raw + dtb_ref[...])
    a_neg = -jnp.exp(alog_ref[...])
    tril = tril_ref[...]
    a_cum = jnp.dot(tril, dt * a_neg, precision=HIGHEST, preferred_element_type=F32)
    a_cum_t = a_cum.T
    dt_t = dt.T
    lower = tril > 0.5

    for g in range(SSD_GROUPS):
        bg = bm[:, g * SSD_STATE:(g + 1) * SSD_STATE]
        cg = cm[:, g * SSD_STATE:(g + 1) * SSD_STATE]
        bg16 = bg.astype(BF16)
        cg16 = cg.astype(BF16)
        cb = _dot_nt(cg16, bg16)
        bgt16 = bg.T.astype(BF16)
        for r in range(R):
            h = g * R + r
            a_col = a_cum[:, h:h + 1]
            a_row = a_cum_t[h:h + 1, :]
            a_last = a_cum[L - 1:L, h:h + 1]
            xh = xs[:, h * P:(h + 1) * P]
            decay = jnp.where(lower, jnp.exp(jnp.where(lower, a_col - a_row, 0.0)), 0.0)
            w_ls = cb * decay * dt_t[h:h + 1, :]
            y_diag = _dot(w_ls.astype(BF16), xh.astype(BF16))
            xw = xh * (jnp.exp(a_last - a_col) * dt[:, h:h + 1])
            st = _dot(bgt16, xw.astype(BF16))
            prev = state_ref[h]
            y_off = _dot(cg16, prev.astype(BF16)) * jnp.exp(a_col)
            state_ref[h] = prev * jnp.exp(a_last) + st
            y_ref[:, h * P:(h + 1) * P] = y_diag + y_off + dsk_ref[:, h:h + 1] * xh

    yg = y_ref[...] * _silu(z)
    gw = SSD_D_INNER // SSD_GROUPS
    for g in range(SSD_GROUPS):
        part = yg[:, g * gw:(g + 1) * gw]
        ms = jnp.mean(part * part, axis=-1, keepdims=True)
        o_ref[0, :, g * gw:(g + 1) * gw] = part * lax.rsqrt(ms + NORM_EPS) * nw_ref[:, g * gw:(g + 1) * gw]


def _lane_row(v):
    return jnp.pad(v.astype(F32), (0, LANES - v.shape[0])).reshape(1, LANES)


def ssd_mixer(u, conv_w, conv_b, dt_bias, a_log, d_skip, norm_w):
    b, s, _ = u.shape
    L = SSD_CHUNK
    tril = jnp.asarray(np.tril(np.ones((L, L), np.float32)))
    full = lambda shape: pl.BlockSpec(shape, lambda i, c: (0,) * len(shape))
    return pl.pallas_call(
        _ssd_kernel,
        name="ssd",
        grid=(b, s // L),
        in_specs=[
            pl.BlockSpec((1, L, SSD_U_COLS), lambda i, c: (i, c, 0)),
            full((SSD_CONV, SSD_CONV_DIM)), full((1, SSD_CONV_DIM)),
            full((1, LANES)), full((1, LANES)), full((1, LANES)),
            full((1, SSD_D_INNER)), full((L, L)),
        ],
        out_specs=pl.BlockSpec((1, L, SSD_D_INNER), lambda i, c: (i, c, 0)),
        out_shape=jax.ShapeDtypeStruct((b, s, SSD_D_INNER), F32),
        scratch_shapes=[pltpu.VMEM((SUBLANES + L, SSD_CONV_DIM), F32),
                        pltpu.VMEM((SSD_HEADS, SSD_STATE, HEAD_DIM), F32),
                        pltpu.VMEM((L, SSD_D_INNER), F32)],
        compiler_params=_cparams(("parallel", "arbitrary")),
    )(u, conv_w, conv_b.reshape(1, -1), _lane_row(dt_bias), _lane_row(a_log), _lane_row(d_skip),
      norm_w.reshape(1, -1), tril)


ATTN_TQ = 128
_SCALE = HEAD_DIM ** -0.5


def _stack_heads(q, g, heads_per_group):
    parts = [q[:, (g * heads_per_group + r) * HEAD_DIM:(g * heads_per_group + r + 1) * HEAD_DIM]
             for r in range(heads_per_group)]
    return jnp.concatenate(parts, axis=0)


def _row_pos(t0, tq, reps):
    row = lax.broadcasted_iota(jnp.int32, (reps * tq, 1), 0)
    return t0 + (row & (tq - 1))


def _rope(x, cosf, sinf):
    n = x.shape[1]
    half = HEAD_DIM // 2
    lane = lax.broadcasted_iota(jnp.int32, x.shape, 1)
    first = (lane & (HEAD_DIM - 1)) < half
    rot = jnp.where(first, pltpu.roll(x, n - half, 1), pltpu.roll(x, half, 1))
    return x * cosf + rot * sinf


def _swa_kernel(q_ref, kv_ref, pos_ref, sink_ref, o_ref):
    tq = q_ref.shape[1]
    W = SWA_WINDOW
    span = W + tq
    R = SWA_HEADS // SWA_KV_HEADS
    kw = SWA_KV_HEADS * HEAD_DIM
    half = HEAD_DIM // 2
    t0 = pl.program_id(1) * tq
    start = pl.multiple_of(jnp.maximum(t0 - W, 0), SUBLANES)

    lane = lax.broadcasted_iota(jnp.int32, (1, LANES), 1)
    expo = -(lane & (half - 1)).astype(F32) / half
    inv_freq = jnp.power(jnp.full((1, LANES), ROPE_THETA, F32), expo)
    sign = jnp.where((lane & (HEAD_DIM - 1)) < half, -1.0, 1.0)

    def cos_sin(pos):
        ang = pos * inv_freq
        return jnp.cos(ang), jnp.sin(ang) * sign

    cq, sq = cos_sin(pos_ref[0, pl.ds(pl.multiple_of(t0, SUBLANES), tq), :])
    ck, sk = cos_sin(pos_ref[0, pl.ds(start, span), :])
    reps = q_ref.shape[2] // LANES
    qr = _rope(q_ref[0], jnp.concatenate([cq] * reps, 1), jnp.concatenate([sq] * reps, 1))
    kvs = kv_ref[0, pl.ds(start, span), :]
    kr = _rope(kvs[:, 0:kw], ck, sk)
    v = kvs[:, kw:2 * kw]

    qidx = _row_pos(t0, tq, R)
    kidx = start + lax.broadcasted_iota(jnp.int32, (1, span), 1)
    rel = qidx - kidx
    valid = (rel >= 0) & (rel < W)
    for g in range(SWA_KV_HEADS):
        qg = _stack_heads(qr, g, R).astype(BF16)
        s = _dot_nt(qg, kr[:, g * HEAD_DIM:(g + 1) * HEAD_DIM].astype(BF16)) * _SCALE
        s = jnp.where(valid, s, NEG)
        snk = jnp.concatenate([jnp.full((tq, 1), sink_ref[g * R + r], F32) for r in range(R)], 0)
        m = jnp.maximum(jnp.max(s, axis=-1, keepdims=True), snk)
        e = jnp.exp(s - m)
        pr = e / (jnp.sum(e, axis=-1, keepdims=True) + jnp.exp(snk - m))
        o = _dot(pr.astype(BF16), v[:, g * HEAD_DIM:(g + 1) * HEAD_DIM].astype(BF16))
        for r in range(R):
            h = g * R + r
            o_ref[0, :, h * HEAD_DIM:(h + 1) * HEAD_DIM] = o[r * tq:(r + 1) * tq]


def swa_mixer(u, positions, sinks):
    b, s, _ = u.shape
    tq = ATTN_TQ
    qw = SWA_HEADS * HEAD_DIM
    kvw = 2 * SWA_KV_HEADS * HEAD_DIM
    pos = positions.astype(F32).reshape(b, s, 1)
    return pl.pallas_call(
        _swa_kernel,
        name="swa",
        grid=(b, s // tq),
        in_specs=[
            pl.BlockSpec((1, tq, qw), lambda i, j: (i, j, 0)),
            pl.BlockSpec((1, s, kvw), lambda i, j: (i, 0, qw // kvw)),
            pl.BlockSpec((1, s, 1), lambda i, j: (i, 0, 0)),
            pl.BlockSpec(memory_space=pltpu.SMEM),
        ],
        out_specs=pl.BlockSpec((1, tq, qw), lambda i, j: (i, j, 0)),
        out_shape=jax.ShapeDtypeStruct((b, s, qw), F32),
        compiler_params=_cparams(("parallel", "arbitrary")),
    )(u, u, pos, sinks.astype(F32))


NSA_U_COLS = NSA_HEADS * HEAD_DIM + 6 * NSA_KV_HEADS * HEAD_DIM + LANES
_NSA_KVW = NSA_KV_HEADS * HEAD_DIM
_NSA_QBLK = NSA_HEADS * HEAD_DIM // _NSA_KVW
_NSA_R = NSA_HEADS // NSA_KV_HEADS
_CMP_HALF = NSA_CMP_BLOCK // 2


def _nsa_compress_kernel(k_ref, v_ref, pe_ref, w1_ref, w2_ref, kc_ref, vc_ref):
    nc = kc_ref.shape[1]
    for idx, (x_ref, o_ref) in enumerate(((k_ref, kc_ref), (v_ref, vc_ref))):
        ha = jnp.zeros((nc, _NSA_KVW), F32)
        hb = jnp.zeros((nc, _NSA_KVW), F32)
        for l in range(_CMP_HALF):
            y = x_ref[0, pl.ds(l, nc, stride=NSA_CMP_STRIDE), :]
            ha = ha + _dot((y + pe_ref[idx, l:l + 1, :]).astype(BF16), w1_ref[idx, l])
            hb = hb + _dot((y + pe_ref[idx, _CMP_HALF + l:_CMP_HALF + l + 1, :]).astype(BF16),
                           w1_ref[idx, _CMP_HALF + l])
        hid = _silu(ha + pltpu.roll(hb, nc - 1, 0))
        o_ref[0] = _dot(hid.astype(BF16), w2_ref[idx])


def _nsa_select_kernel(q_ref, kc_ref, vc_ref, ov_ref, ocmp_ref, sel_ref):
    tq = q_ref.shape[1]
    nc = kc_ref.shape[1]
    nsel = ov_ref.shape[0]
    k_eff = min(NSA_TOPK, nsel)
    t0 = pl.program_id(1) * tq
    t_row = _row_pos(t0, tq, _NSA_R)
    cmp_end = lax.broadcasted_iota(jnp.int32, (1, nc), 1) * NSA_CMP_STRIDE + (NSA_CMP_BLOCK - 1)
    cvalid = cmp_end <= t_row
    jidx = lax.broadcasted_iota(jnp.int32, (nsel, 1), 0)
    blk_t = (t0 + lax.broadcasted_iota(jnp.int32, (1, tq), 1)) // NSA_SEL_BLOCK
    jvalid = jidx <= blk_t
    forced = (jidx == 0) | (jidx == blk_t)
    q = q_ref[0]
    sel_rows = []
    for g in range(NSA_KV_HEADS):
        qg = _stack_heads(q, g, _NSA_R).astype(BF16)
        kc = kc_ref[0, :, g * HEAD_DIM:(g + 1) * HEAD_DIM].astype(BF16)
        vc = vc_ref[0, :, g * HEAD_DIM:(g + 1) * HEAD_DIM].astype(BF16)
        s = _dot_nt(qg, kc) * _SCALE
        m = jnp.max(jnp.where(cvalid, s, NEG), axis=-1, keepdims=True)
        e = jnp.exp(jnp.where(cvalid, s - m, NEG))
        den = jnp.sum(e, axis=-1, keepdims=True)
        p = e / jnp.where(den > 0.0, den, 1.0)
        o = _dot(p.astype(BF16), vc)
        psum = p[0:tq]
        for r in range(_NSA_R):
            h = g * _NSA_R + r
            ocmp_ref[0, :, h * HEAD_DIM:(h + 1) * HEAD_DIM] = o[r * tq:(r + 1) * tq]
            if r:
                psum = psum + p[r * tq:(r + 1) * tq]
        imp_t = lax.dot_general(ov_ref[...], psum, (((1,), (1,)), ((), ())),
                                precision=HIGHEST, preferred_element_type=F32)
        score = jnp.where(forced, BIG, jnp.where(jvalid, imp_t, NEG))
        cnt = jnp.zeros((nsel, tq), F32)
        for i in range(nsel):
            si = score[i:i + 1, :]
            beats = (si > score) | ((si == score) & (jidx > i))
            cnt = cnt + jnp.where(beats, 1.0, 0.0)
        sel_rows.append(jnp.where(cnt < k_eff, 1.0, 0.0))
    pad = LANES - NSA_KV_HEADS * nsel
    sel_t = jnp.concatenate(sel_rows + [jnp.zeros((pad, tq), F32)], axis=0)
    sel_ref[0] = sel_t.T


def _softmax_rows(s):
    m = jnp.max(s, axis=-1, keepdims=True)
    e = jnp.exp(s - m)
    return e / jnp.sum(e, axis=-1, keepdims=True)


def _nsa_attend_kernel(q_ref, sel_ref, gate_ref, ocmp_ref, ks_ref, vs_ref, kw_ref, vw_ref, ex_ref, o_ref):
    tq = q_ref.shape[1]
    s_len = ks_ref.shape[1]
    W = NSA_WINDOW
    span = min(W + tq, s_len)
    t0 = pl.program_id(1) * tq
    start = pl.multiple_of(jnp.maximum(t0 - W, 0), SUBLANES)
    t_row = _row_pos(t0, tq, _NSA_R)
    kidx = lax.broadcasted_iota(jnp.int32, (1, s_len), 1)
    causal = kidx <= t_row
    rel = t_row - (start + lax.broadcasted_iota(jnp.int32, (1, span), 1))
    in_win = (rel >= 0) & (rel < W)
    q = q_ref[0]
    sig = _sigmoid(gate_ref[0])
    sel16 = sel_ref[0].astype(BF16)
    for g in range(NSA_KV_HEADS):
        cols = slice(g * HEAD_DIM, (g + 1) * HEAD_DIM)
        qg = _stack_heads(q, g, _NSA_R).astype(BF16)
        chosen = _dot(sel16, ex_ref[g])
        chosen = jnp.concatenate([chosen] * _NSA_R, axis=0) > 0.5
        s = _dot_nt(qg, ks_ref[0, :, cols].astype(BF16)) * _SCALE
        p = _softmax_rows(jnp.where(chosen & causal, s, NEG))
        o_sel = _dot(p.astype(BF16), vs_ref[0, :, cols].astype(BF16))
        s2 = _dot_nt(qg, kw_ref[0, pl.ds(start, span), cols].astype(BF16)) * _SCALE
        p2 = _softmax_rows(jnp.where(in_win, s2, NEG))
        o_win = _dot(p2.astype(BF16), vw_ref[0, pl.ds(start, span), cols].astype(BF16))
        for r in range(_NSA_R):
            h = g * _NSA_R + r
            hc = slice(h * HEAD_DIM, (h + 1) * HEAD_DIM)
            rows = slice(r * tq, (r + 1) * tq)
            o_ref[0, :, hc] = (sig[:, h:h + 1] * ocmp_ref[0, :, hc]
                               + sig[:, NSA_HEADS + h:NSA_HEADS + h + 1] * o_sel[rows]
                               + sig[:, 2 * NSA_HEADS + h:2 * NSA_HEADS + h + 1] * o_win[rows])


def _block_diag2(w):
    z = jnp.zeros_like(w)
    return jnp.concatenate([jnp.concatenate([w, z], -1), jnp.concatenate([z, w], -1)], -2)


def nsa_mixer(u, cmp_pe, cmp_w1, cmp_w2):
    b, s, _ = u.shape
    tq = ATTN_TQ
    nc = s // NSA_CMP_STRIDE
    nsel = s // NSA_SEL_BLOCK
    qw = NSA_HEADS * HEAD_DIM
    pe2 = jnp.concatenate([cmp_pe, cmp_pe], -1)
    w1bd = _block_diag2(cmp_w1).astype(BF16)
    w2bd = _block_diag2(cmp_w2).astype(BF16)
    const = lambda shape: pl.BlockSpec(shape, lambda *_: (0,) * len(shape))
    col = lambda c: pl.BlockSpec((1, s, _NSA_KVW), lambda i, *_: (i, 0, _NSA_QBLK + c))

    kc, vc = pl.pallas_call(
        _nsa_compress_kernel,
        name="nsa_compress",
        grid=(b,),
        in_specs=[col(0), col(1), const(pe2.shape), const(w1bd.shape), const(w2bd.shape)],
        out_specs=[pl.BlockSpec((1, nc, _NSA_KVW), lambda i: (i, 0, 0))] * 2,
        out_shape=[jax.ShapeDtypeStruct((b, nc, _NSA_KVW), F32)] * 2,
        compiler_params=_cparams(("parallel",)),
    )(u, u, pe2, w1bd, w2bd)

    c_start = np.arange(nc) * NSA_CMP_STRIDE
    s_start = np.arange(nsel) * NSA_SEL_BLOCK
    ov_t = ((c_start[None, :] < s_start[:, None] + NSA_SEL_BLOCK)
            & (c_start[None, :] + NSA_CMP_BLOCK > s_start[:, None])
            & (np.arange(nc)[None, :] < nc - 1)).astype(np.float32)
    qspec = pl.BlockSpec((1, tq, qw), lambda i, j: (i, j, 0))
    o_cmp, sel = pl.pallas_call(
        _nsa_select_kernel,
        name="nsa_select",
        grid=(b, s // tq),
        in_specs=[qspec,
                  pl.BlockSpec((1, nc, _NSA_KVW), lambda i, j: (i, 0, 0)),
                  pl.BlockSpec((1, nc, _NSA_KVW), lambda i, j: (i, 0, 0)),
                  const(ov_t.shape)],
        out_specs=[qspec, pl.BlockSpec((1, tq, LANES), lambda i, j: (i, j, 0))],
        out_shape=[jax.ShapeDtypeStruct((b, s, qw), F32), jax.ShapeDtypeStruct((b, s, LANES), F32)],
        compiler_params=_cparams(("parallel", "arbitrary")),
    )(u, kc, vc, jnp.asarray(ov_t))

    expand = np.zeros((NSA_KV_HEADS, LANES, s), np.float32)
    for g in range(NSA_KV_HEADS):
        expand[g, g * nsel + np.arange(s) // NSA_SEL_BLOCK, np.arange(s)] = 1.0
    gate_blk = NSA_U_COLS // LANES - 1
    return pl.pallas_call(
        _nsa_attend_kernel,
        name="nsa_attend",
        grid=(b, s // tq),
        in_specs=[qspec,
                  pl.BlockSpec((1, tq, LANES), lambda i, j: (i, j, 0)),
                  pl.BlockSpec((1, tq, LANES), lambda i, j: (i, j, gate_blk)),
                  qspec,
                  col(2), col(3), col(4), col(5),
                  const(expand.shape)],
        out_specs=qspec,
        out_shape=jax.ShapeDtypeStruct((b, s, qw), F32),
        compiler_params=_cparams(("parallel", "arbitrary")),
    )(u, sel, u, o_cmp, u, u, u, u, jnp.asarray(expand, BF16))


def _seg_sum(x, ones16):
    hi = x.astype(BF16)
    lo = (x - hi.astype(F32)).astype(BF16)
    return _dot(hi, ones16) + _dot(lo, ones16)


def _head_ones(width):
    idx = np.arange(width) // HEAD_DIM
    return jnp.asarray((idx[:, None] == idx[None, :]).astype(np.float32), BF16)


_RW_R, _RW_K, _RW_V = 0, RWKV_DIM, 2 * RWKV_DIM
_RW_WD = 3 * RWKV_DIM
_RW_AD = _RW_WD + RWKV_W_LORA
_RW_GD = _RW_AD + RWKV_A_LORA


def _rwkv_prep_kernel(u_ref, mu_ref, w0_ref, wup_ref, a0_ref, aup_ref, gup_ref, kk_ref, ka_ref, rk_ref,
                      ones_ref, r_o, w_o, k_o, v_o, kk_o, q_o, g_o, bonus_o, up_ref):
    tm = u_ref.shape[1]

    @pl.when(pl.program_id(1) == 0)
    def _():
        up_ref[0:SUBLANES, :] = jnp.zeros((SUBLANES, RWKV_IN), F32)

    u = u_ref[0]
    up_ref[SUBLANES:SUBLANES + tm, :] = u
    prev = up_ref[pl.ds(SUBLANES - 1, tm), :]
    up_ref[0:SUBLANES, :] = u[tm - SUBLANES:tm, :]
    x = u + (prev - u) * mu_ref[...]
    r = x[:, _RW_R:_RW_R + RWKV_DIM]
    k = x[:, _RW_K:_RW_K + RWKV_DIM]
    v = x[:, _RW_V:_RW_V + RWKV_DIM]
    wd = x[:, _RW_WD:_RW_AD]
    ad = x[:, _RW_AD:_RW_GD]
    gd = x[:, _RW_GD:RWKV_IN]
    w = -_softplus(-(w0_ref[...] + _dot(jnp.tanh(wd).astype(BF16), wup_ref[...]))) - 0.5
    a = _sigmoid(a0_ref[...] + _dot(ad.astype(BF16), aup_ref[...]))
    ones16 = ones_ref[...]
    kk = k * kk_ref[...]
    kk = kk / jnp.maximum(jnp.sqrt(_seg_sum(kk * kk, ones16)), 1e-12)
    k2 = k * (1.0 + (a - 1.0) * ka_ref[...])
    r_o[0] = r
    w_o[0] = jnp.exp(-jnp.exp(w))
    k_o[0] = k2
    v_o[0] = v
    kk_o[0] = kk
    q_o[0] = kk * a
    g_o[0] = _dot(_sigmoid(gd).astype(BF16), gup_ref[...])
    bonus_o[0] = _seg_sum(r * k2 * rk_ref[...], ones16) * v


RWKV_TB = 128
_RW_SUB = HEAD_DIM
_RW_PAIRS = RWKV_HEADS // 2
_RW_BATCH = 4
_RW_MM_CHAINS = 4


def _split16(x):
    hi = x.astype(BF16)
    lo = (x - hi.astype(F32)).astype(BF16)
    return jnp.concatenate([hi, lo], axis=1)


def _rwkv_scan_kernel(r_ref, w_ref, k_ref, v_ref, kk_ref, q_ref, ones_ref, o_ref, s_ref, vt_ref, ot_ref):
    bb = r_ref.shape[0]
    half = HEAD_DIM

    @pl.when(pl.program_id(1) == 0)
    def _():
        s_ref[...] = jnp.zeros_like(s_ref)

    ones2 = ones_ref[...]
    lane = lax.broadcasted_iota(jnp.int32, (1, 2 * LANES), 1)
    chains = [(bi, p) for bi in range(bb) for p in range(_RW_PAIRS)]
    groups = [list(range(g0, min(g0 + _RW_MM_CHAINS, len(chains))))
              for g0 in range(0, len(chains), _RW_MM_CHAINS)]
    zero16 = jnp.zeros((HEAD_DIM, 2 * LANES), BF16)

    for sb in range(RWKV_TB // _RW_SUB):
        t_base = sb * _RW_SUB
        for c, (bi, p) in enumerate(chains):
            lanes = slice(p * LANES, (p + 1) * LANES)
            vt = v_ref[bi, :, lanes].T
            vt_ref[c] = _split16(jnp.concatenate([vt[0:half, t_base:t_base + _RW_SUB],
                                                  vt[half:LANES, t_base:t_base + _RW_SUB]], axis=1))
            ot_ref[sb, c] = jnp.zeros((HEAD_DIM, LANES), F32)

        def step_group(jg, carry):
            t8 = pl.multiple_of(t_base + jg * SUBLANES, SUBLANES)
            rows = [[ref[bi, pl.ds(t8, SUBLANES), p * LANES:(p + 1) * LANES]
                     for ref in (kk_ref, w_ref, q_ref, k_ref, r_ref)] for bi, p in chains]
            for i in range(SUBLANES):
                pick2 = (lane & (half - 1)) == jg * SUBLANES + i
                pick = pick2[:, 0:LANES]
                for grp in groups:
                    lhs = []
                    for c in grp:
                        lhs.append(_split16(s_ref[c] * rows[c][0][i:i + 1, :]))
                        lhs.append(jnp.where(pick2, vt_ref[c], zero16))
                    res = _dot(jnp.concatenate(lhs, axis=0), ones2)
                    lhs = []
                    for n, c in enumerate(grp):
                        kk_t, w_t, q_t, k_t, r_t = (x[i:i + 1, :] for x in rows[c])
                        sa = res[2 * n * HEAD_DIM:(2 * n + 1) * HEAD_DIM]
                        vb = res[(2 * n + 1) * HEAD_DIM:(2 * n + 2) * HEAD_DIM]
                        st = s_ref[c] * w_t - sa * q_t + vb * k_t
                        s_ref[c] = st
                        lhs.append(_split16(st * r_t))
                    res = _dot(jnp.concatenate(lhs, axis=0), ones2)
                    for n, c in enumerate(grp):
                        ot_ref[sb, c] = jnp.where(pick, res[n * HEAD_DIM:(n + 1) * HEAD_DIM], ot_ref[sb, c])
            return carry

        lax.fori_loop(0, _RW_SUB // SUBLANES, step_group, 0)

    for c, (bi, p) in enumerate(chains):
        subs = [ot_ref[sb, c] for sb in range(RWKV_TB // _RW_SUB)]
        top = jnp.concatenate([x[:, 0:half] for x in subs], axis=1)
        bot = jnp.concatenate([x[:, half:LANES] for x in subs], axis=1)
        o_ref[bi, :, p * LANES:(p + 1) * LANES] = jnp.concatenate([top, bot], axis=0).T


_RWL_TB = 64
_RWL_KH = HEAD_DIM // 2
_RWL_VB = HEAD_DIM // SUBLANES


def _rwkv_scan_lanes_kernel(w_ref, kk_ref, q_ref, k_ref, r_ref, v_ref, o_ref, s_ref):
    @pl.when(pl.program_id(0) == 0)
    def _():
        s_ref[...] = jnp.zeros_like(s_ref)

    def both_halves(x):
        return x + pltpu.roll(x, LANES // 2, 1)

    def step(t, carry):
        acc = [None] * _RWL_VB
        for k in range(_RWL_KH):
            kk_row = kk_ref[t, k:k + 1, :]
            for vb in range(_RWL_VB):
                term = s_ref[vb, k] * kk_row
                acc[vb] = term if acc[vb] is None else acc[vb] + term
        sa = [both_halves(a) for a in acc]
        vt = [v_ref[t, vb * SUBLANES:(vb + 1) * SUBLANES, :] for vb in range(_RWL_VB)]
        acc = [None] * _RWL_VB
        for k in range(_RWL_KH):
            w_row = w_ref[t, k:k + 1, :]
            q_row = q_ref[t, k:k + 1, :]
            k_row = k_ref[t, k:k + 1, :]
            r_row = r_ref[t, k:k + 1, :]
            for vb in range(_RWL_VB):
                st = s_ref[vb, k] * w_row - sa[vb] * q_row + vt[vb] * k_row
                s_ref[vb, k] = st
                term = st * r_row
                acc[vb] = term if acc[vb] is None else acc[vb] + term
        for vb in range(_RWL_VB):
            o_ref[t, vb * SUBLANES:(vb + 1) * SUBLANES, :] = both_halves(acc[vb])
        return carry

    lax.fori_loop(0, o_ref.shape[0], step, 0)


def _rwkv_post_kernel(o_ref, bonus_ref, g_ref, lnw_ref, lnb_ref, ones_ref, y_ref):
    ones16 = ones_ref[...]
    o = o_ref[...]
    mean = _seg_sum(o, ones16) * (1.0 / HEAD_DIM)
    cen = o - mean
    var = _seg_sum(cen * cen, ones16) * (1.0 / HEAD_DIM)
    o = cen * lax.rsqrt(var + RWKV_GN_EPS) * lnw_ref[...] + lnb_ref[...]
    y_ref[...] = (o + bonus_ref[...]) * g_ref[...]


def rwkv7_mixer(u, mu, w0, w_up, a0, a_up, g_up, k_k, k_a, r_k, ln_w, ln_b):
    b, s, _ = u.shape
    dm = RWKV_DIM
    tm = _row_tile(s, 256)
    row = lambda v: v.astype(F32).reshape(1, -1)
    const = lambda shape: pl.BlockSpec(shape, lambda *_: (0,) * len(shape))
    ones_d = _head_ones(dm)
    tile = pl.BlockSpec((1, tm, dm), lambda i, j: (i, j, 0))
    outs = pl.pallas_call(
        _rwkv_prep_kernel,
        name="rwkv_prep",
        grid=(b, s // tm),
        in_specs=[pl.BlockSpec((1, tm, RWKV_IN), lambda i, j: (i, j, 0)),
                  const((1, RWKV_IN)), const((1, dm)), const((RWKV_W_LORA, dm)), const((1, dm)),
                  const((RWKV_A_LORA, dm)), const((RWKV_G_LORA, dm)), const((1, dm)), const((1, dm)),
                  const((1, dm)), const((dm, dm))],
        out_specs=[tile] * 8,
        out_shape=[jax.ShapeDtypeStruct((b, s, dm), F32)] * 8,
        scratch_shapes=[pltpu.VMEM((SUBLANES + tm, RWKV_IN), F32)],
        compiler_params=_cparams(("parallel", "arbitrary")),
    )(u, row(mu), row(w0), w_up.astype(BF16), row(a0), a_up.astype(BF16), g_up.astype(BF16),
      row(k_k), row(k_a), row(r_k), ones_d)
    r, wdec, k2, v, kk, q, g, bonus = outs

    nchain = b * RWKV_HEADS
    assert 2 * nchain == LANES, "the scan kernel maps (key half, batch, head) onto the 128 lanes"

    def key_rows(x):
        x = x.reshape(b, s, RWKV_HEADS, 2, _RWL_KH).transpose(1, 4, 3, 0, 2)
        return x.reshape(s, _RWL_KH, LANES)

    v_rows = v.reshape(b, s, RWKV_HEADS, HEAD_DIM).transpose(1, 3, 0, 2).reshape(s, HEAD_DIM, nchain)
    v_rows = jnp.concatenate([v_rows, v_rows], axis=-1)
    tb = _row_tile(s, _RWL_TB)
    kblk = pl.BlockSpec((tb, _RWL_KH, LANES), lambda i: (i, 0, 0))
    vblk = pl.BlockSpec((tb, HEAD_DIM, LANES), lambda i: (i, 0, 0))
    o = pl.pallas_call(
        _rwkv_scan_lanes_kernel,
        name="rwkv_scan",
        grid=(s // tb,),
        in_specs=[kblk] * 5 + [vblk],
        out_specs=vblk,
        out_shape=jax.ShapeDtypeStruct((s, HEAD_DIM, LANES), F32),
        scratch_shapes=[pltpu.VMEM((_RWL_VB, _RWL_KH, SUBLANES, LANES), F32)],
        compiler_params=_cparams(("arbitrary",)),
    )(key_rows(wdec), key_rows(kk), key_rows(q), key_rows(k2), key_rows(r), v_rows)
    o = o[:, :, :nchain].reshape(s, HEAD_DIM, b, RWKV_HEADS).transpose(2, 0, 3, 1).reshape(b, s, dm)

    t = b * s
    tp = _row_tile(t, 512)
    flat = pl.BlockSpec((tp, dm), lambda i: (i, 0))
    y = pl.pallas_call(
        _rwkv_post_kernel,
        name="rwkv_post",
        grid=(t // tp,),
        in_specs=[flat, flat, flat, const((1, dm)), const((1, dm)), const((dm, dm))],
        out_specs=flat,
        out_shape=jax.ShapeDtypeStruct((t, dm), F32),
        compiler_params=_cparams(("parallel",)),
    )(o.reshape(t, dm), bonus.reshape(t, dm), g.reshape(t, dm), row(ln_w), row(ln_b), ones_d)
    return y.reshape(b, s, dm)


MOE_SLOT_BLOCK = 512
MOE_FF_CHUNK = 256
MOE_ROW_TILE = 256
_META_E0, _META_E1, _META_R0, _META_R1, _META_W0, _META_W1 = range(6)


def _moe_route_kernel(x_ref, g_ref, wr_ref, tril_ref, h_o, meta_o, cnt_o, carry_ref):
    @pl.when(pl.program_id(0) == 0)
    def _():
        carry_ref[...] = jnp.zeros_like(carry_ref)

    x = x_ref[...]
    ms = jnp.mean(x * x, axis=-1, keepdims=True)
    h = x * lax.rsqrt(ms + NORM_EPS) * g_ref[...]
    h_o[...] = h
    logits = jnp.dot(h, wr_ref[...], precision=HIGHEST, preferred_element_type=F32)
    lane = lax.broadcasted_iota(jnp.int32, logits.shape, 1)
    l1 = jnp.where(lane < N_EXPERTS, logits, NEG)
    m1 = jnp.max(l1, axis=-1, keepdims=True)
    i1 = jnp.min(jnp.where(l1 == m1, lane, LANES), axis=-1, keepdims=True)
    l2 = jnp.where(lane == i1, NEG, l1)
    m2 = jnp.max(l2, axis=-1, keepdims=True)
    i2 = jnp.min(jnp.where(l2 == m2, lane, LANES), axis=-1, keepdims=True)
    e21 = jnp.exp(m2 - m1)
    w1 = 1.0 / (1.0 + e21)
    w2 = e21 * w1
    cnt = jnp.where((lane == i1) | (lane == i2), 1.0, 0.0)
    before = _dot(tril_ref[...], cnt.astype(BF16)) + carry_ref[...]
    r1 = jnp.sum(jnp.where(lane == i1, before, 0.0), axis=-1, keepdims=True)
    r2 = jnp.sum(jnp.where(lane == i2, before, 0.0), axis=-1, keepdims=True)
    carry_ref[...] += jnp.sum(cnt, axis=0, keepdims=True)
    cnt_o[...] = carry_ref[...]
    meta = jnp.zeros(logits.shape, F32)
    for idx, val in ((_META_E0, i1.astype(F32)), (_META_E1, i2.astype(F32)), (_META_R0, r1),
                     (_META_R1, r2), (_META_W0, w1), (_META_W1, w2)):
        meta = jnp.where(lane == idx, val, meta)
    meta_o[...] = meta


def _row_copy(src_ref, src_row, dst_ref, dst_row, sem):
    return pltpu.make_async_copy(src_ref.at[pl.ds(src_row, 1), :], dst_ref.at[pl.ds(dst_row, 1), :], sem)


def _moe_scatter_kernel(d0_ref, d1_ref, h_ref, xs_in_ref, xs_ref, sem):
    del xs_in_ref
    tm = h_ref.shape[0]

    def issue(r, c):
        _row_copy(h_ref, r, xs_ref, d0_ref[r], sem).start()
        _row_copy(h_ref, r, xs_ref, d1_ref[r], sem).start()
        return c

    def drain(r, c):
        _row_copy(h_ref, r, xs_ref, d0_ref[r], sem).wait()
        _row_copy(h_ref, r, xs_ref, d1_ref[r], sem).wait()
        return c

    lax.fori_loop(0, tm, issue, 0)
    lax.fori_loop(0, tm, drain, 0)


def _moe_expert_kernel(be_ref, nused_ref, xs_ref, wg_ref, wu_ref, wd_ref, ys_ref):
    del be_ref
    i = pl.program_id(0)

    @pl.when(i < nused_ref[0])
    def _():
        x16 = xs_ref[...].astype(BF16)
        acc = jnp.zeros(ys_ref.shape, F32)
        for c in range(D_FF // MOE_FF_CHUNK):
            cols = slice(c * MOE_FF_CHUNK, (c + 1) * MOE_FF_CHUNK)
            a = _silu(_dot(x16, wg_ref[0, :, cols])) * _dot(x16, wu_ref[0, :, cols])
            acc = acc + _dot(a.astype(BF16), wd_ref[0, cols, :])
        ys_ref[...] = acc

    @pl.when(i >= nused_ref[0])
    def _():
        ys_ref[...] = jnp.zeros_like(ys_ref)


def _moe_combine_kernel(d0_ref, d1_ref, x_ref, meta_ref, ys_ref, o_ref, b0_ref, b1_ref, sem):
    tm = x_ref.shape[0]

    def issue(r, c):
        _row_copy(ys_ref, d0_ref[r], b0_ref, r, sem).start()
        _row_copy(ys_ref, d1_ref[r], b1_ref, r, sem).start()
        return c

    def drain(r, c):
        _row_copy(ys_ref, d0_ref[r], b0_ref, r, sem).wait()
        _row_copy(ys_ref, d1_ref[r], b1_ref, r, sem).wait()
        return c

    lax.fori_loop(0, tm, issue, 0)
    lax.fori_loop(0, tm, drain, 0)
    w0 = meta_ref[:, _META_W0:_META_W0 + 1]
    w1 = meta_ref[:, _META_W1:_META_W1 + 1]
    o_ref[...] = x_ref[...] + (w0 * b0_ref[...] + w1 * b1_ref[...])


def moe_ffn(x, g, router, w_gate, w_up, w_down):
    t, d = x.shape
    tm = _row_tile(t, MOE_ROW_TILE)
    blk = MOE_SLOT_BLOCK
    nblk = -(-(t * TOP_K + N_EXPERTS * (blk - 1)) // blk)
    slots = nblk * blk
    tril = jnp.asarray(np.tril(np.ones((tm, tm), np.float32), -1), BF16)
    const = lambda shape: pl.BlockSpec(shape, lambda *_: (0,) * len(shape))
    row = lambda w: pl.BlockSpec((tm, w), lambda i: (i, 0))
    h, meta, counts = pl.pallas_call(
        _moe_route_kernel,
        name="moe_route",
        grid=(t // tm,),
        in_specs=[row(d), const((1, d)), const((d, LANES)), const((tm, tm))],
        out_specs=[row(d), row(LANES), const((1, LANES))],
        out_shape=[jax.ShapeDtypeStruct((t, d), F32), jax.ShapeDtypeStruct((t, LANES), F32),
                   jax.ShapeDtypeStruct((1, LANES), F32)],
        scratch_shapes=[pltpu.VMEM((1, LANES), F32)],
        compiler_params=_cparams(("arbitrary",)),
    )(x, g.reshape(1, d), _pad_cols(router.astype(F32), LANES), tril)

    cnt = counts[0, :N_EXPERTS].astype(jnp.int32)
    pcnt = (cnt + blk - 1) // blk * blk
    pend = jnp.cumsum(pcnt)
    pstart = pend - pcnt
    e0 = meta[:, _META_E0].astype(jnp.int32)
    e1 = meta[:, _META_E1].astype(jnp.int32)
    d0 = pstart[e0] + meta[:, _META_R0].astype(jnp.int32)
    d1 = pstart[e1] + meta[:, _META_R1].astype(jnp.int32)
    blk_start = jnp.arange(nblk, dtype=jnp.int32) * blk
    blk_e = jnp.minimum(jnp.sum((pend[None, :] <= blk_start[:, None]).astype(jnp.int32), axis=1),
                        N_EXPERTS - 1).astype(jnp.int32)
    nused = (pend[-1:] // blk).astype(jnp.int32)

    smem_rows = pl.BlockSpec((tm,), lambda i: (i,), memory_space=pltpu.SMEM)
    xs = pl.pallas_call(
        _moe_scatter_kernel,
        name="moe_scatter",
        grid=(t // tm,),
        in_specs=[smem_rows, smem_rows, row(d), pl.BlockSpec(memory_space=pl.ANY)],
        out_specs=pl.BlockSpec(memory_space=pl.ANY),
        out_shape=jax.ShapeDtypeStruct((slots, d), F32),
        scratch_shapes=[pltpu.SemaphoreType.DMA(())],
        input_output_aliases={3: 0},
        compiler_params=_cparams(("arbitrary",)),
    )(d0, d1, h, jnp.zeros((slots, d), F32))

    f = w_gate.shape[2]
    ys = pl.pallas_call(
        _moe_expert_kernel,
        name="moe_expert",
        grid_spec=pltpu.PrefetchScalarGridSpec(
            num_scalar_prefetch=2,
            grid=(nblk,),
            in_specs=[pl.BlockSpec((blk, d), lambda i, be, nu: (i, 0)),
                      pl.BlockSpec((1, d, f), lambda i, be, nu: (be[i], 0, 0)),
                      pl.BlockSpec((1, d, f), lambda i, be, nu: (be[i], 0, 0)),
                      pl.BlockSpec((1, f, d), lambda i, be, nu: (be[i], 0, 0))],
            out_specs=pl.BlockSpec((blk, d), lambda i, be, nu: (i, 0)),
        ),
        out_shape=jax.ShapeDtypeStruct((slots, d), F32),
        compiler_params=_cparams(("arbitrary",)),
    )(blk_e, nused, xs, w_gate, w_up, w_down)

    return pl.pallas_call(
        _moe_combine_kernel,
        name="moe_combine",
        grid=(t // tm,),
        in_specs=[smem_rows, smem_rows, row(d), row(LANES), pl.BlockSpec(memory_space=pl.ANY)],
        out_specs=row(d),
        out_shape=jax.ShapeDtypeStruct((t, d), F32),
        scratch_shapes=[pltpu.VMEM((tm, d), F32), pltpu.VMEM((tm, d), F32), pltpu.SemaphoreType.DMA(())],
        compiler_params=_cparams(("arbitrary",)),
    )(d0, d1, x, meta, ys)


_IN_SPLITS = (SSD_D_INNER, SSD_CONV_DIM, SSD_HEADS,
              NSA_HEADS * HEAD_DIM, 6 * NSA_KV_HEADS * HEAD_DIM, 3 * NSA_HEADS,
              RWKV_IN,
              SWA_HEADS * HEAD_DIM, 2 * SWA_KV_HEADS * HEAD_DIM,
              N_BRANCHES * D_MODEL)
_IN_OFF = tuple(int(o) for o in np.cumsum((0,) + _IN_SPLITS))


def _split_in_proj(w):
    o = _IN_OFF
    seg = lambda a, b: w[:, o[a]:o[b]]
    w_ssd = jnp.concatenate([seg(0, 2), _pad_cols(seg(2, 3), LANES)], axis=1)
    w_nsa = jnp.concatenate([seg(3, 5), _pad_cols(seg(5, 6), LANES)], axis=1)
    return tuple(m.astype(BF16) for m in (w_ssd, w_nsa, seg(6, 7), seg(7, 9), seg(9, 10)))


def kernel(x, p, positions, norm_mix, w_in, ssd_conv_w, ssd_conv_b, ssd_dt_bias, ssd_a_log, ssd_d, ssd_norm, nsa_cmp_pe, nsa_cmp_w1, nsa_cmp_w2, rwkv_mu, rwkv_w0, rwkv_w_up, rwkv_a0, rwkv_a_up, rwkv_g_up, rwkv_k_k, rwkv_k_a, rwkv_r_k, rwkv_ln_w, rwkv_ln_b, swa_sinks, w_br_ssd, w_br_nsa, w_br_rwkv, w_br_swa, w_out, norm_ffn, ffn_w_gate, ffn_w_up, ffn_w_down, moe_router, moe_w_gate, moe_w_up, moe_w_down, ple_proj, ple_gate, norm_final):
    b, s, d = x.shape
    t = b * s
    depth = w_in.shape[0]
    xf = x.reshape(t, d)
    for i in range(depth):
        w_ssd, w_nsa, w_rwkv, w_swa, w_gates = _split_in_proj(w_in[i])
        g_mix = norm_mix[i]
        u_ssd = norm_matmul(xf, g_mix, w_ssd).reshape(b, s, -1)
        u_nsa = norm_matmul(xf, g_mix, w_nsa).reshape(b, s, -1)
        u_rwkv = norm_matmul(xf, g_mix, w_rwkv).reshape(b, s, -1)
        u_swa = norm_matmul(xf, g_mix, w_swa).reshape(b, s, -1)
        gate_logits = norm_matmul(xf, g_mix, w_gates)
        y_ssd = ssd_mixer(u_ssd, ssd_conv_w[i], ssd_conv_b[i], ssd_dt_bias[i], ssd_a_log[i], ssd_d[i], ssd_norm[i])
        y_nsa = nsa_mixer(u_nsa, nsa_cmp_pe[i], nsa_cmp_w1[i], nsa_cmp_w2[i])
        y_rwkv = rwkv7_mixer(u_rwkv, rwkv_mu[i], rwkv_w0[i], rwkv_w_up[i], rwkv_a0[i], rwkv_a_up[i],
                             rwkv_g_up[i], rwkv_k_k[i], rwkv_k_a[i], rwkv_r_k[i], rwkv_ln_w[i], rwkv_ln_b[i])
        y_swa = swa_mixer(u_swa, positions, swa_sinks[i])
        p_stack = jnp.stack([w_br_ssd[i], w_br_nsa[i], w_br_rwkv[i], w_br_swa[i]]).astype(BF16)
        ys = [y.reshape(t, -1) for y in (y_ssd, y_nsa, y_rwkv, y_swa)]
        xf = merge_branches(xf, ys, gate_logits, p_stack, w_out[i].astype(BF16))
        j = i // 2
        if i % 2 == 0:
            xf = dense_ffn(xf, norm_ffn[i], ffn_w_gate[j].astype(BF16), ffn_w_up[j].astype(BF16),
                           ffn_w_down[j].astype(BF16))
        else:
            xf = moe_ffn(xf, norm_ffn[i], moe_router[j], moe_w_gate[j].astype(BF16),
                         moe_w_up[j].astype(BF16), moe_w_down[j].astype(BF16))
        xf = ple(xf, p[i].reshape(t, -1), ple_gate[i].astype(BF16), ple_proj[i].astype(BF16),
                 norm_final if i == depth - 1 else None)
    return xf.reshape(b, s, d)
```

```python
import functools

import numpy as np
import jax
import jax.numpy as jnp
from jax import lax
from jax.experimental import pallas as pl
from jax.experimental.pallas import tpu as pltpu

F32 = jnp.float32
BF16 = jnp.bfloat16

D_MODEL = 1024
HEAD_DIM = 64
NORM_EPS = 1e-6
NEG = -1e30
BIG = 1e30

SSD_HEADS = 8
SSD_D_INNER = SSD_HEADS * HEAD_DIM
SSD_STATE = 128
SSD_GROUPS = 2
SSD_CONV = 4
SSD_CHUNK = 128
SSD_CONV_DIM = SSD_D_INNER + 2 * SSD_GROUPS * SSD_STATE

NSA_HEADS = 8
NSA_KV_HEADS = 2
NSA_CMP_BLOCK = 32
NSA_CMP_STRIDE = 16
NSA_CMP_HIDDEN = 64
NSA_SEL_BLOCK = 64
NSA_TOPK = 8
NSA_WINDOW = 512

RWKV_HEADS = 8
RWKV_DIM = RWKV_HEADS * HEAD_DIM
RWKV_W_LORA = 64
RWKV_A_LORA = 64
RWKV_G_LORA = 128
RWKV_IN = 3 * RWKV_DIM + RWKV_W_LORA + RWKV_A_LORA + RWKV_G_LORA
RWKV_GN_EPS = 64e-5

SWA_HEADS = 8
SWA_KV_HEADS = 2
SWA_WINDOW = 128
ROPE_THETA = 150000.0

N_BRANCHES = 4
D_FF = 2816
N_EXPERTS = 8
TOP_K = 2
PLE_DIM = 256

LANES = 128
SUBLANES = 8
VMEM_LIMIT = 56 * 1024 * 1024

HIGHEST = lax.Precision.HIGHEST


def _cparams(sem):
    return pltpu.CompilerParams(dimension_semantics=sem, vmem_limit_bytes=VMEM_LIMIT)


def _sigmoid(x):
    return 1.0 / (1.0 + jnp.exp(-x))


def _silu(x):
    return x * _sigmoid(x)


def _softplus(x):
    return jnp.maximum(x, 0.0) + jnp.log1p(jnp.exp(-jnp.abs(x)))


def _dot(a, b):
    return jnp.dot(a, b, preferred_element_type=F32)


def _dot_nt(a, b):
    return lax.dot_general(a, b, (((1,), (1,)), ((), ())), preferred_element_type=F32)


def _pad_cols(w, n):
    return jnp.pad(w, ((0, 0), (0, n - w.shape[1])))


def _row_tile(t, pref):
    while t % pref:
        pref //= 2
    return pref


def _rms_bf16(x, g):
    ms = jnp.mean(x * x, axis=-1, keepdims=True)
    return (x * lax.rsqrt(ms + NORM_EPS) * g).astype(BF16)


def _norm_matmul_kernel(x_ref, g_ref, w_ref, o_ref):
    o_ref[...] = _dot(_rms_bf16(x_ref[...], g_ref[...]), w_ref[...])


def norm_matmul(x, g, w):
    t, d = x.shape
    n = w.shape[1]
    tm = _row_tile(t, 1024)
    return pl.pallas_call(
        _norm_matmul_kernel,
        name="norm_matmul",
        grid=(t // tm,),
        in_specs=[
            pl.BlockSpec((tm, d), lambda i: (i, 0)),
            pl.BlockSpec((1, d), lambda i: (0, 0)),
            pl.BlockSpec((d, n), lambda i: (0, 0)),
        ],
        out_specs=pl.BlockSpec((tm, n), lambda i: (i, 0)),
        out_shape=jax.ShapeDtypeStruct((t, n), F32),
        compiler_params=_cparams(("parallel",)),
    )(x, g.reshape(1, d), w)


def _merge_kernel(x_ref, g_ref, wg_ref, y0_ref, y1_ref, y2_ref, y3_ref, p_ref, wo_ref, o_ref):
    x = x_ref[...]
    h = _rms_bf16(x, g_ref[...])
    acc = None
    for m, y_ref in enumerate((y0_ref, y1_ref, y2_ref, y3_ref)):
        pm = _dot(y_ref[...].astype(BF16), p_ref[m])
        gm = _sigmoid(_dot(h, wg_ref[:, m * D_MODEL:(m + 1) * D_MODEL]))
        acc = gm * pm if acc is None else acc + gm * pm
    o_ref[...] = x + _dot(acc.astype(BF16), wo_ref[...])


def merge_branches(x, g, w_gates, ys, p_stack, w_out):
    t, d = x.shape
    tm = _row_tile(t, 512)
    dm = ys[0].shape[1]
    row = lambda w: pl.BlockSpec((tm, w), lambda i: (i, 0))
    const = lambda shape: pl.BlockSpec(shape, lambda i: (0,) * len(shape))
    return pl.pallas_call(
        _merge_kernel,
        name="merge",
        grid=(t // tm,),
        in_specs=[row(d), const((1, d)), const((d, N_BRANCHES * d)), row(dm), row(dm), row(dm), row(dm),
                  const((N_BRANCHES, dm, d)), const((d, d))],
        out_specs=row(d),
        out_shape=jax.ShapeDtypeStruct((t, d), F32),
        compiler_params=_cparams(("parallel",)),
    )(x, g.reshape(1, d), w_gates, *ys, p_stack, w_out)


def _ffn_kernel(x_ref, g_ref, wg_ref, wu_ref, wd_ref, o_ref, h_ref, acc_ref):
    j = pl.program_id(1)

    @pl.when(j == 0)
    def _():
        x = x_ref[...]
        ms = jnp.mean(x * x, axis=-1, keepdims=True)
        h_ref[...] = (x * lax.rsqrt(ms + NORM_EPS) * g_ref[...]).astype(BF16)
        acc_ref[...] = jnp.zeros_like(acc_ref)

    h = h_ref[...]
    a = _silu(_dot(h, wg_ref[...])) * _dot(h, wu_ref[...])
    acc_ref[...] += _dot(a.astype(BF16), wd_ref[...])

    @pl.when(j == pl.num_programs(1) - 1)
    def _():
        o_ref[...] = x_ref[...] + acc_ref[...]


def dense_ffn(x, g, w_gate, w_up, w_down):
    t, d = x.shape
    f = w_gate.shape[1]
    tm = _row_tile(t, 512)
    tf = f // 2 if (f // 2) % LANES == 0 else f
    return pl.pallas_call(
        _ffn_kernel,
        name="dense_ffn",
        grid=(t // tm, f // tf),
        in_specs=[
            pl.BlockSpec((tm, d), lambda i, j: (i, 0)),
            pl.BlockSpec((1, d), lambda i, j: (0, 0)),
            pl.BlockSpec((d, tf), lambda i, j: (0, j)),
            pl.BlockSpec((d, tf), lambda i, j: (0, j)),
            pl.BlockSpec((tf, d), lambda i, j: (j, 0)),
        ],
        out_specs=pl.BlockSpec((tm, d), lambda i, j: (i, 0)),
        out_shape=jax.ShapeDtypeStruct((t, d), F32),
        scratch_shapes=[pltpu.VMEM((tm, d), BF16), pltpu.VMEM((tm, d), F32)],
        compiler_params=_cparams(("parallel", "arbitrary")),
    )(x, g.reshape(1, d), w_gate, w_up, w_down)


def _ple_kernel(x_ref, p_ref, wg_ref, wp_ref, o_ref):
    x = x_ref[...]
    gate = _sigmoid(_dot(x.astype(BF16), wg_ref[...]))
    o_ref[...] = x + gate * _dot(p_ref[...].astype(BF16), wp_ref[...])


def _ple_final_kernel(x_ref, p_ref, wg_ref, wp_ref, nf_ref, o_ref):
    x = x_ref[...]
    gate = _sigmoid(_dot(x.astype(BF16), wg_ref[...]))
    y = x + gate * _dot(p_ref[...].astype(BF16), wp_ref[...])
    ms = jnp.mean(y * y, axis=-1, keepdims=True)
    o_ref[...] = y * lax.rsqrt(ms + NORM_EPS) * nf_ref[...]


def ple(x, p, w_gate, w_proj, norm_final=None):
    t, d = x.shape
    pd = p.shape[1]
    tm = _row_tile(t, 512)
    in_specs = [pl.BlockSpec((tm, d), lambda i: (i, 0)),
                pl.BlockSpec((tm, pd), lambda i: (i, 0)),
                pl.BlockSpec((d, d), lambda i: (0, 0)),
                pl.BlockSpec((pd, d), lambda i: (0, 0))]
    args = [x, p, w_gate, w_proj]
    body = _ple_kernel
    if norm_final is not None:
        in_specs.append(pl.BlockSpec((1, d), lambda i: (0, 0)))
        args.append(norm_final.reshape(1, d))
        body = _ple_final_kernel
    return pl.pallas_call(
        body,
        name="ple",
        grid=(t // tm,),
        in_specs=in_specs,
        out_specs=pl.BlockSpec((tm, d), lambda i: (i, 0)),
        out_shape=jax.ShapeDtypeStruct((t, d), F32),
        compiler_params=_cparams(("parallel",)),
    )(*args)


SSD_U_COLS = SSD_D_INNER + SSD_CONV_DIM + LANES
_SSD_GN = SSD_GROUPS * SSD_STATE


def _ssd_kernel(u_ref, cw_ref, cb_ref, dtb_ref, alog_ref, dsk_ref, nw_ref, tril_ref,
                o_ref, xpad_ref, state_ref, y_ref):
    L = SSD_CHUNK
    P = HEAD_DIM
    R = SSD_HEADS // SSD_GROUPS

    @pl.when(pl.program_id(1) == 0)
    def _():
        xpad_ref[0:SUBLANES, :] = jnp.zeros((SUBLANES, SSD_CONV_DIM), F32)
        state_ref[...] = jnp.zeros_like(state_ref)

    z = u_ref[0, :, 0:SSD_D_INNER]
    xbc = u_ref[0, :, SSD_D_INNER:SSD_D_INNER + SSD_CONV_DIM]
    dt_raw = u_ref[0, :, SSD_D_INNER + SSD_CONV_DIM:SSD_U_COLS]

    xpad_ref[SUBLANES:SUBLANES + L, :] = xbc
    conv = cb_ref[...]
    for j in range(SSD_CONV):
        conv = conv + cw_ref[j:j + 1, :] * xpad_ref[pl.ds(SUBLANES - (SSD_CONV - 1) + j, L), :]
    xpad_ref[0:SUBLANES, :] = xbc[L - SUBLANES:L, :]
    act = _silu(conv)
    xs = act[:, 0:SSD_D_INNER]
    bm = act[:, SSD_D_INNER:SSD_D_INNER + _SSD_GN]
    cm = act[:, SSD_D_INNER + _SSD_GN:SSD_CONV_DIM]

    dt = _softplus(dt_raw + dtb_ref[...])
    a_neg = -jnp.exp(alog_ref[...])
    tril = tril_ref[...]
    a_cum = jnp.dot(tril, dt * a_neg, precision=HIGHEST, preferred_element_type=F32)
    a_cum_t = a_cum.T
    dt_t = dt.T
    lower = tril > 0.5

    for g in range(SSD_GROUPS):
        bg = bm[:, g * SSD_STATE:(g + 1) * SSD_STATE]
        cg = cm[:, g * SSD_STATE:(g + 1) * SSD_STATE]
        bg16 = bg.astype(BF16)
        cg16 = cg.astype(BF16)
        cb = _dot_nt(cg16, bg16)
        bgt16 = bg.T.astype(BF16)
        for r in range(R):
            h = g * R + r
            a_col = a_cum[:, h:h + 1]
            a_row = a_cum_t[h:h + 1, :]
            a_last = a_cum[L - 1:L, h:h + 1]
            xh = xs[:, h * P:(h + 1) * P]
            decay = jnp.where(lower, jnp.exp(jnp.where(lower, a_col - a_row, 0.0)), 0.0)
            w_ls = cb * decay * dt_t[h:h + 1, :]
            y_diag = _dot(w_ls.astype(BF16), xh.astype(BF16))
            xw = xh * (jnp.exp(a_last - a_col) * dt[:, h:h + 1])
            st = _dot(bgt16, xw.astype(BF16))
            prev = state_ref[h]
            y_off = _dot(cg16, prev.astype(BF16)) * jnp.exp(a_col)
            state_ref[h] = prev * jnp.exp(a_last) + st
            y_ref[:, h * P:(h + 1) * P] = y_diag + y_off + dsk_ref[:, h:h + 1] * xh

    yg = y_ref[...] * _silu(z)
    gw = SSD_D_INNER // SSD_GROUPS
    for g in range(SSD_GROUPS):
        part = yg[:, g * gw:(g + 1) * gw]
        ms = jnp.mean(part * part, axis=-1, keepdims=True)
        o_ref[0, :, g * gw:(g + 1) * gw] = part * lax.rsqrt(ms + NORM_EPS) * nw_ref[:, g * gw:(g + 1) * gw]


def _lane_row(v):
    return jnp.pad(v.astype(F32), (0, LANES - v.shape[0])).reshape(1, LANES)


def ssd_mixer(u, conv_w, conv_b, dt_bias, a_log, d_skip, norm_w):
    b, s, _ = u.shape
    L = SSD_CHUNK
    tril = jnp.asarray(np.tril(np.ones((L, L), np.float32)))
    full = lambda shape: pl.BlockSpec(shape, lambda i, c: (0,) * len(shape))
    return pl.pallas_call(
        _ssd_kernel,
        name="ssd",
        grid=(b, s // L),
        in_specs=[
            pl.BlockSpec((1, L, SSD_U_COLS), lambda i, c: (i, c, 0)),
            full((SSD_CONV, SSD_CONV_DIM)), full((1, SSD_CONV_DIM)),
            full((1, LANES)), full((1, LANES)), full((1, LANES)),
            full((1, SSD_D_INNER)), full((L, L)),
        ],
        out_specs=pl.BlockSpec((1, L, SSD_D_INNER), lambda i, c: (i, c, 0)),
        out_shape=jax.ShapeDtypeStruct((b, s, SSD_D_INNER), F32),
        scratch_shapes=[pltpu.VMEM((SUBLANES + L, SSD_CONV_DIM), F32),
                        pltpu.VMEM((SSD_HEADS, SSD_STATE, HEAD_DIM), F32),
                        pltpu.VMEM((L, SSD_D_INNER), F32)],
        compiler_params=_cparams(("parallel", "arbitrary")),
    )(u, conv_w, conv_b.reshape(1, -1), _lane_row(dt_bias), _lane_row(a_log), _lane_row(d_skip),
      norm_w.reshape(1, -1), tril)


ATTN_TQ = 128
SWA_TQ = 256
_SCALE = HEAD_DIM ** -0.5


def _stack_heads(q, g, heads_per_group):
    parts = [q[:, (g * heads_per_group + r) * HEAD_DIM:(g * heads_per_group + r + 1) * HEAD_DIM]
             for r in range(heads_per_group)]
    return jnp.concatenate(parts, axis=0)


def _row_pos(t0, tq, reps):
    row = lax.broadcasted_iota(jnp.int32, (reps * tq, 1), 0)
    return t0 + (row & (tq - 1))


def _rope(x, cosf, sinf):
    n = x.shape[1]
    half = HEAD_DIM // 2
    lane = lax.broadcasted_iota(jnp.int32, x.shape, 1)
    first = (lane & (HEAD_DIM - 1)) < half
    rot = jnp.where(first, pltpu.roll(x, n - half, 1), pltpu.roll(x, half, 1))
    return x * cosf + rot * sinf


def _rope_table_kernel(pos_ref, cos_ref, sin_ref):
    half = HEAD_DIM // 2
    lane = lax.broadcasted_iota(jnp.int32, (1, LANES), 1)
    expo = -(lane & (half - 1)).astype(F32) / half
    inv_freq = jnp.power(jnp.full((1, LANES), ROPE_THETA, F32), expo)
    sign = jnp.where((lane & (HEAD_DIM - 1)) < half, -1.0, 1.0)
    ang = pos_ref[0] * inv_freq
    cos_ref[0] = jnp.cos(ang)
    sin_ref[0] = jnp.sin(ang) * sign


def rope_tables(positions):
    b, s = positions.shape
    tm = _row_tile(s, 512)
    out = pl.BlockSpec((1, tm, LANES), lambda i, j: (i, j, 0))
    return pl.pallas_call(
        _rope_table_kernel,
        name="rope_tables",
        grid=(b, s // tm),
        in_specs=[pl.BlockSpec((1, tm, 1), lambda i, j: (i, j, 0))],
        out_specs=[out, out],
        out_shape=[jax.ShapeDtypeStruct((b, s, LANES), F32)] * 2,
        compiler_params=_cparams(("parallel", "parallel")),
    )(positions.astype(F32).reshape(b, s, 1))


def _swa_kernel(q_ref, kv_ref, cosq_ref, sinq_ref, cosk_ref, sink_ref, sinks_ref, o_ref):
    tq = q_ref.shape[1]
    W = SWA_WINDOW
    span = W + tq
    R = SWA_HEADS // SWA_KV_HEADS
    kw = SWA_KV_HEADS * HEAD_DIM
    t0 = pl.program_id(1) * tq
    start = pl.multiple_of(jnp.maximum(t0 - W, 0), SUBLANES)

    reps = q_ref.shape[2] // LANES
    qr = _rope(q_ref[0], jnp.concatenate([cosq_ref[0]] * reps, 1), jnp.concatenate([sinq_ref[0]] * reps, 1))
    qr = qr * _SCALE
    kvs = kv_ref[0, pl.ds(start, span), :]
    kr = _rope(kvs[:, 0:kw], cosk_ref[0, pl.ds(start, span), :], sink_ref[0, pl.ds(start, span), :])
    v = kvs[:, kw:2 * kw]

    rel = _row_pos(t0, tq, 1) - (start + lax.broadcasted_iota(jnp.int32, (1, span), 1))
    bias = jnp.where((rel >= 0) & (rel < W), 0.0, NEG)
    for g in range(SWA_KV_HEADS):
        qg = _stack_heads(qr, g, R).astype(BF16)
        s = _dot_nt(qg, kr[:, g * HEAD_DIM:(g + 1) * HEAD_DIM].astype(BF16)).reshape(R, tq, span) + bias
        snk = jnp.concatenate([jnp.full((1, tq, 1), sinks_ref[g * R + r], F32) for r in range(R)], 0)
        m = jnp.maximum(jnp.max(s, axis=-1, keepdims=True), snk)
        e = jnp.exp(s - m)
        den = jnp.sum(e, axis=-1, keepdims=True) + jnp.exp(snk - m)
        o = _dot(e.reshape(R * tq, span).astype(BF16), v[:, g * HEAD_DIM:(g + 1) * HEAD_DIM].astype(BF16))
        o = o.reshape(R, tq, HEAD_DIM) / den
        for r in range(R):
            h = g * R + r
            o_ref[0, :, h * HEAD_DIM:(h + 1) * HEAD_DIM] = o[r]


def swa_mixer(u, rope_cos, rope_sin, sinks):
    b, s, _ = u.shape
    tq = SWA_TQ
    qw = SWA_HEADS * HEAD_DIM
    kvw = 2 * SWA_KV_HEADS * HEAD_DIM
    tile = pl.BlockSpec((1, tq, LANES), lambda i, j: (i, j, 0))
    full = pl.BlockSpec((1, s, LANES), lambda i, j: (i, 0, 0))
    return pl.pallas_call(
        _swa_kernel,
        name="swa",
        grid=(b, s // tq),
        in_specs=[
            pl.BlockSpec((1, tq, qw), lambda i, j: (i, j, 0)),
            pl.BlockSpec((1, s, kvw), lambda i, j: (i, 0, qw // kvw)),
            tile, tile, full, full,
            pl.BlockSpec(memory_space=pltpu.SMEM),
        ],
        out_specs=pl.BlockSpec((1, tq, qw), lambda i, j: (i, j, 0)),
        out_shape=jax.ShapeDtypeStruct((b, s, qw), F32),
        compiler_params=_cparams(("parallel", "arbitrary")),
    )(u, u, rope_cos, rope_sin, rope_cos, rope_sin, sinks.astype(F32))


NSA_U_COLS = NSA_HEADS * HEAD_DIM + 6 * NSA_KV_HEADS * HEAD_DIM + LANES
_NSA_KVW = NSA_KV_HEADS * HEAD_DIM
_NSA_QBLK = NSA_HEADS * HEAD_DIM // _NSA_KVW
_NSA_R = NSA_HEADS // NSA_KV_HEADS
_CMP_HALF = NSA_CMP_BLOCK // 2
NSA_KEY_CHUNK = 512


def _nsa_compress_kernel(k_ref, v_ref, pe_ref, w1_ref, w2_ref, kc_ref, vc_ref):
    nc = kc_ref.shape[1]
    for idx, (x_ref, o_ref) in enumerate(((k_ref, kc_ref), (v_ref, vc_ref))):
        ha = jnp.zeros((nc, _NSA_KVW), F32)
        hb = jnp.zeros((nc, _NSA_KVW), F32)
        for l in range(_CMP_HALF):
            y = x_ref[0, pl.ds(l, nc, stride=NSA_CMP_STRIDE), :]
            ha = ha + _dot((y + pe_ref[idx, l:l + 1, :]).astype(BF16), w1_ref[idx, l])
            hb = hb + _dot((y + pe_ref[idx, _CMP_HALF + l:_CMP_HALF + l + 1, :]).astype(BF16),
                           w1_ref[idx, _CMP_HALF + l])
        hid = _silu(ha + pltpu.roll(hb, nc - 1, 0))
        o_ref[0] = _dot(hid.astype(BF16), w2_ref[idx])


def _nsa_select_kernel(q_ref, kc_ref, vc_ref, ov_ref, ocmp_ref, sel_ref):
    tq = q_ref.shape[1]
    nc = kc_ref.shape[1]
    nsel = ov_ref.shape[0]
    k_eff = min(NSA_TOPK, nsel)
    t0 = pl.program_id(1) * tq
    t_row = _row_pos(t0, tq, _NSA_R)
    cmp_end = lax.broadcasted_iota(jnp.int32, (1, nc), 1) * NSA_CMP_STRIDE + (NSA_CMP_BLOCK - 1)
    cvalid = cmp_end <= t_row
    jidx = lax.broadcasted_iota(jnp.int32, (nsel, 1), 0)
    blk_t = (t0 + lax.broadcasted_iota(jnp.int32, (1, tq), 1)) // NSA_SEL_BLOCK
    jvalid = jidx <= blk_t
    forced = (jidx == 0) | (jidx == blk_t)
    q = q_ref[0]
    sel_rows = []
    for g in range(NSA_KV_HEADS):
        qg = _stack_heads(q, g, _NSA_R).astype(BF16)
        kc = kc_ref[0, :, g * HEAD_DIM:(g + 1) * HEAD_DIM].astype(BF16)
        vc = vc_ref[0, :, g * HEAD_DIM:(g + 1) * HEAD_DIM].astype(BF16)
        s = _dot_nt(qg, kc) * _SCALE
        m = jnp.max(jnp.where(cvalid, s, NEG), axis=-1, keepdims=True)
        e = jnp.exp(jnp.where(cvalid, s - m, NEG))
        den = jnp.sum(e, axis=-1, keepdims=True)
        p = e / jnp.where(den > 0.0, den, 1.0)
        o = _dot(p.astype(BF16), vc)
        psum = p[0:tq]
        for r in range(_NSA_R):
            h = g * _NSA_R + r
            ocmp_ref[0, :, h * HEAD_DIM:(h + 1) * HEAD_DIM] = o[r * tq:(r + 1) * tq]
            if r:
                psum = psum + p[r * tq:(r + 1) * tq]
        imp_t = lax.dot_general(ov_ref[...], psum, (((1,), (1,)), ((), ())),
                                precision=HIGHEST, preferred_element_type=F32)
        score = jnp.where(forced, BIG, jnp.where(jvalid, imp_t, NEG))
        cnt = jnp.zeros((nsel, tq), F32)
        for i in range(nsel):
            si = score[i:i + 1, :]
            beats = (si > score) | ((si == score) & (jidx > i))
            cnt = cnt + jnp.where(beats, 1.0, 0.0)
        sel_rows.append(jnp.where(cnt < k_eff, 1.0, 0.0))
    pad = LANES - NSA_KV_HEADS * nsel
    sel_t = jnp.concatenate(sel_rows + [jnp.zeros((pad, tq), F32)], axis=0)
    sel_ref[0] = sel_t.T


def _nsa_attend_kernel(q_ref, sel_ref, gate_ref, ocmp_ref, ks_ref, vs_ref, kw_ref, vw_ref, ex_ref, o_ref):
    tq = q_ref.shape[1]
    s_len = ks_ref.shape[1]
    W = NSA_WINDOW
    span = min(W + tq, s_len)
    t0 = pl.program_id(1) * tq
    start = pl.multiple_of(jnp.maximum(t0 - W, 0), SUBLANES)
    t_q = _row_pos(t0, tq, 1)
    rel = t_q - (start + lax.broadcasted_iota(jnp.int32, (1, span), 1))
    win_bias = jnp.where((rel >= 0) & (rel < W), 0.0, NEG)
    q = q_ref[0] * _SCALE
    sig = _sigmoid(gate_ref[0])
    sel16 = sel_ref[0].astype(BF16)

    def attend(qg, k, v, bias):
        n = k.shape[0]
        s = _dot_nt(qg, k.astype(BF16)).reshape(_NSA_R, tq, n) + bias
        e = jnp.exp(s - jnp.max(s, axis=-1, keepdims=True))
        den = jnp.sum(e, axis=-1, keepdims=True)
        o = _dot(e.reshape(_NSA_R * tq, n).astype(BF16), v.astype(BF16))
        return o.reshape(_NSA_R, tq, HEAD_DIM) / den

    chunk = min(NSA_KEY_CHUNK, s_len)
    for c in range(s_len // chunk):
        klen = (c + 1) * chunk

        @pl.when(t0 // chunk == c)
        def _():
            kidx = lax.broadcasted_iota(jnp.int32, (1, klen), 1)
            for g in range(NSA_KV_HEADS):
                cols = slice(g * HEAD_DIM, (g + 1) * HEAD_DIM)
                qg = _stack_heads(q, g, _NSA_R).astype(BF16)
                chosen = _dot(sel16, ex_ref[g, :, 0:klen]) > 0.5
                sel_bias = jnp.where(chosen & (kidx <= t_q), 0.0, NEG)
                o_sel = attend(qg, ks_ref[0, 0:klen, cols], vs_ref[0, 0:klen, cols], sel_bias)
                o_win = attend(qg, kw_ref[0, pl.ds(start, span), cols], vw_ref[0, pl.ds(start, span), cols],
                               win_bias)
                for r in range(_NSA_R):
                    h = g * _NSA_R + r
                    hc = slice(h * HEAD_DIM, (h + 1) * HEAD_DIM)
                    o_ref[0, :, hc] = (sig[:, h:h + 1] * ocmp_ref[0, :, hc]
                                       + sig[:, NSA_HEADS + h:NSA_HEADS + h + 1] * o_sel[r]
                                       + sig[:, 2 * NSA_HEADS + h:2 * NSA_HEADS + h + 1] * o_win[r])


def _block_diag2(w):
    z = jnp.zeros_like(w)
    return jnp.concatenate([jnp.concatenate([w, z], -1), jnp.concatenate([z, w], -1)], -2)


def nsa_mixer(u, cmp_pe, cmp_w1, cmp_w2):
    b, s, _ = u.shape
    tq = ATTN_TQ
    nc = s // NSA_CMP_STRIDE
    nsel = s // NSA_SEL_BLOCK
    qw = NSA_HEADS * HEAD_DIM
    pe2 = jnp.concatenate([cmp_pe, cmp_pe], -1)
    w1bd = _block_diag2(cmp_w1).astype(BF16)
    w2bd = _block_diag2(cmp_w2).astype(BF16)
    const = lambda shape: pl.BlockSpec(shape, lambda *_: (0,) * len(shape))
    col = lambda c: pl.BlockSpec((1, s, _NSA_KVW), lambda i, *_: (i, 0, _NSA_QBLK + c))

    kc, vc = pl.pallas_call(
        _nsa_compress_kernel,
        name="nsa_compress",
        grid=(b,),
        in_specs=[col(0), col(1), const(pe2.shape), const(w1bd.shape), const(w2bd.shape)],
        out_specs=[pl.BlockSpec((1, nc, _NSA_KVW), lambda i: (i, 0, 0))] * 2,
        out_shape=[jax.ShapeDtypeStruct((b, nc, _NSA_KVW), F32)] * 2,
        compiler_params=_cparams(("parallel",)),
    )(u, u, pe2, w1bd, w2bd)

    c_start = np.arange(nc) * NSA_CMP_STRIDE
    s_start = np.arange(nsel) * NSA_SEL_BLOCK
    ov_t = ((c_start[None, :] < s_start[:, None] + NSA_SEL_BLOCK)
            & (c_start[None, :] + NSA_CMP_BLOCK > s_start[:, None])
            & (np.arange(nc)[None, :] < nc - 1)).astype(np.float32)
    qspec = pl.BlockSpec((1, tq, qw), lambda i, j: (i, j, 0))
    o_cmp, sel = pl.pallas_call(
        _nsa_select_kernel,
        name="nsa_select",
        grid=(b, s // tq),
        in_specs=[qspec,
                  pl.BlockSpec((1, nc, _NSA_KVW), lambda i, j: (i, 0, 0)),
                  pl.BlockSpec((1, nc, _NSA_KVW), lambda i, j: (i, 0, 0)),
                  const(ov_t.shape)],
        out_specs=[qspec, pl.BlockSpec((1, tq, LANES), lambda i, j: (i, j, 0))],
        out_shape=[jax.ShapeDtypeStruct((b, s, qw), F32), jax.ShapeDtypeStruct((b, s, LANES), F32)],
        compiler_params=_cparams(("parallel", "arbitrary")),
    )(u, kc, vc, jnp.asarray(ov_t))

    expand = np.zeros((NSA_KV_HEADS, LANES, s), np.float32)
    for g in range(NSA_KV_HEADS):
        expand[g, g * nsel + np.arange(s) // NSA_SEL_BLOCK, np.arange(s)] = 1.0
    gate_blk = NSA_U_COLS // LANES - 1
    return pl.pallas_call(
        _nsa_attend_kernel,
        name="nsa_attend",
        grid=(b, s // tq),
        in_specs=[qspec,
                  pl.BlockSpec((1, tq, LANES), lambda i, j: (i, j, 0)),
                  pl.BlockSpec((1, tq, LANES), lambda i, j: (i, j, gate_blk)),
                  qspec,
                  col(2), col(3), col(4), col(5),
                  const(expand.shape)],
        out_specs=qspec,
        out_shape=jax.ShapeDtypeStruct((b, s, qw), F32),
        compiler_params=_cparams(("parallel", "arbitrary")),
    )(u, sel, u, o_cmp, u, u, u, u, jnp.asarray(expand, BF16))


def _seg_sum(x, ones16):
    hi = x.astype(BF16)
    lo = (x - hi.astype(F32)).astype(BF16)
    return _dot(hi, ones16) + _dot(lo, ones16)


def _head_ones(width):
    idx = np.arange(width) // HEAD_DIM
    return jnp.asarray((idx[:, None] == idx[None, :]).astype(np.float32), BF16)


_RW_R, _RW_K, _RW_V = 0, RWKV_DIM, 2 * RWKV_DIM
_RW_WD = 3 * RWKV_DIM
_RW_AD = _RW_WD + RWKV_W_LORA
_RW_GD = _RW_AD + RWKV_A_LORA


def _rwkv_prep_kernel(u_ref, mu_ref, w0_ref, wup_ref, a0_ref, aup_ref, gup_ref, kk_ref, ka_ref, rk_ref,
                      ones_ref, r_o, w_o, k_o, v_o, kk_o, q_o, g_o, bonus_o, up_ref):
    tm = u_ref.shape[1]

    @pl.when(pl.program_id(1) == 0)
    def _():
        up_ref[0:SUBLANES, :] = jnp.zeros((SUBLANES, RWKV_IN), F32)

    u = u_ref[0]
    up_ref[SUBLANES:SUBLANES + tm, :] = u
    prev = up_ref[pl.ds(SUBLANES - 1, tm), :]
    up_ref[0:SUBLANES, :] = u[tm - SUBLANES:tm, :]
    x = u + (prev - u) * mu_ref[...]
    r = x[:, _RW_R:_RW_R + RWKV_DIM]
    k = x[:, _RW_K:_RW_K + RWKV_DIM]
    v = x[:, _RW_V:_RW_V + RWKV_DIM]
    wd = x[:, _RW_WD:_RW_AD]
    ad = x[:, _RW_AD:_RW_GD]
    gd = x[:, _RW_GD:RWKV_IN]
    w = -_softplus(-(w0_ref[...] + _dot(jnp.tanh(wd).astype(BF16), wup_ref[...]))) - 0.5
    a = _sigmoid(a0_ref[...] + _dot(ad.astype(BF16), aup_ref[...]))
    ones16 = ones_ref[...]
    kk = k * kk_ref[...]
    kk = kk / jnp.maximum(jnp.sqrt(_seg_sum(kk * kk, ones16)), 1e-12)
    k2 = k * (1.0 + (a - 1.0) * ka_ref[...])
    r_o[0] = r
    w_o[0] = jnp.exp(-jnp.exp(w))
    k_o[0] = k2
    v_o[0] = v
    kk_o[0] = kk
    q_o[0] = kk * a
    g_o[0] = _dot(_sigmoid(gd).astype(BF16), gup_ref[...])
    bonus_o[0] = _seg_sum(r * k2 * rk_ref[...], ones16) * v


_RWS_TB = 64
_RWS_VB = HEAD_DIM // 2 // SUBLANES
_RWS_NACC = 2
_RWS_VECS = 5


def _rwkv_scan_vhalf_kernel(kk_ref, w_ref, q_ref, k_ref, r_ref, v_ref, o_ref, s_ref, rows_ref):
    @pl.when(pl.program_id(0) == 0)
    def _():
        s_ref[...] = jnp.zeros_like(s_ref)

    low = lax.broadcasted_iota(jnp.int32, (HEAD_DIM, LANES), 1) < LANES // 2

    def reduce_acc(acc, vb):
        tot = acc[(vb, 0)]
        for i in range(1, _RWS_NACC):
            tot = tot + acc[(vb, i)]
        return tot

    def pair(tp, carry):
        for a, ref in enumerate((kk_ref, w_ref, q_ref, k_ref, r_ref)):
            x = ref[tp]
            xr = pltpu.roll(x, LANES // 2, 1)
            rows_ref[0, a] = jnp.where(low, x, xr)
            rows_ref[1, a] = jnp.where(low, xr, x)
        for par in range(2):
            t = 2 * tp + par
            acc = {}
            for k in range(HEAD_DIM):
                kk_row = rows_ref[par, 0, k:k + 1, :]
                for vb in range(_RWS_VB):
                    term = s_ref[vb, k] * kk_row
                    key = (vb, k % _RWS_NACC)
                    acc[key] = acc[key] + term if key in acc else term
            sa = [reduce_acc(acc, vb) for vb in range(_RWS_VB)]
            vt = [v_ref[t, vb * SUBLANES:(vb + 1) * SUBLANES, :] for vb in range(_RWS_VB)]
            acc = {}
            for k in range(HEAD_DIM):
                w_row, q_row, k_row, r_row = (rows_ref[par, a, k:k + 1, :] for a in range(1, _RWS_VECS))
                for vb in range(_RWS_VB):
                    st = s_ref[vb, k] * w_row - sa[vb] * q_row + vt[vb] * k_row
                    s_ref[vb, k] = st
                    term = st * r_row
                    key = (vb, k % _RWS_NACC)
                    acc[key] = acc[key] + term if key in acc else term
            for vb in range(_RWS_VB):
                o_ref[t, vb * SUBLANES:(vb + 1) * SUBLANES, :] = reduce_acc(acc, vb)
        return carry

    lax.fori_loop(0, o_ref.shape[0] // 2, pair, 0)


def _rwkv_post_kernel(o_ref, bonus_ref, g_ref, lnw_ref, lnb_ref, ones_ref, y_ref):
    ones16 = ones_ref[...]
    o = o_ref[...]
    mean = _seg_sum(o, ones16) * (1.0 / HEAD_DIM)
    cen = o - mean
    var = _seg_sum(cen * cen, ones16) * (1.0 / HEAD_DIM)
    o = cen * lax.rsqrt(var + RWKV_GN_EPS) * lnw_ref[...] + lnb_ref[...]
    y_ref[...] = (o + bonus_ref[...]) * g_ref[...]


def rwkv7_mixer(u, mu, w0, w_up, a0, a_up, g_up, k_k, k_a, r_k, ln_w, ln_b):
    b, s, _ = u.shape
    dm = RWKV_DIM
    tm = _row_tile(s, 256)
    row = lambda v: v.astype(F32).reshape(1, -1)
    const = lambda shape: pl.BlockSpec(shape, lambda *_: (0,) * len(shape))
    ones_d = _head_ones(dm)
    tile = pl.BlockSpec((1, tm, dm), lambda i, j: (i, j, 0))
    outs = pl.pallas_call(
        _rwkv_prep_kernel,
        name="rwkv_prep",
        grid=(b, s // tm),
        in_specs=[pl.BlockSpec((1, tm, RWKV_IN), lambda i, j: (i, j, 0)),
                  const((1, RWKV_IN)), const((1, dm)), const((RWKV_W_LORA, dm)), const((1, dm)),
                  const((RWKV_A_LORA, dm)), const((RWKV_G_LORA, dm)), const((1, dm)), const((1, dm)),
                  const((1, dm)), const((dm, dm))],
        out_specs=[tile] * 8,
        out_shape=[jax.ShapeDtypeStruct((b, s, dm), F32)] * 8,
        scratch_shapes=[pltpu.VMEM((SUBLANES + tm, RWKV_IN), F32)],
        compiler_params=_cparams(("parallel", "arbitrary")),
    )(u, row(mu), row(w0), w_up.astype(BF16), row(a0), a_up.astype(BF16), g_up.astype(BF16),
      row(k_k), row(k_a), row(r_k), ones_d)
    r, wdec, k2, v, kk, q, g, bonus = outs

    nchain = b * RWKV_HEADS
    assert 2 * nchain == LANES and s % 2 == 0, "the scan kernel maps (half, batch, head) onto the 128 lanes"
    vh = HEAD_DIM // 2

    def key_pairs(x):
        x = x.reshape(b, s // 2, 2, RWKV_HEADS, HEAD_DIM).transpose(1, 4, 2, 0, 3)
        return x.reshape(s // 2, HEAD_DIM, LANES)

    def value_rows(x):
        x = x.reshape(b, s, RWKV_HEADS, 2, vh).transpose(1, 4, 3, 0, 2)
        return x.reshape(s, vh, LANES)

    tb = _row_tile(s, _RWS_TB)
    kblk = pl.BlockSpec((tb // 2, HEAD_DIM, LANES), lambda i: (i, 0, 0))
    vblk = pl.BlockSpec((tb, vh, LANES), lambda i: (i, 0, 0))
    o = pl.pallas_call(
        _rwkv_scan_vhalf_kernel,
        name="rwkv_scan",
        grid=(s // tb,),
        in_specs=[kblk] * _RWS_VECS + [vblk],
        out_specs=vblk,
        out_shape=jax.ShapeDtypeStruct((s, vh, LANES), F32),
        scratch_shapes=[pltpu.VMEM((_RWS_VB, HEAD_DIM, SUBLANES, LANES), F32),
                        pltpu.VMEM((2, _RWS_VECS, HEAD_DIM, LANES), F32)],
        compiler_params=_cparams(("arbitrary",)),
    )(key_pairs(kk), key_pairs(wdec), key_pairs(q), key_pairs(k2), key_pairs(r), value_rows(v))
    o = o.reshape(s, vh, 2, b, RWKV_HEADS).transpose(3, 0, 4, 2, 1).reshape(b, s, dm)

    t = b * s
    tp = _row_tile(t, 512)
    flat = pl.BlockSpec((tp, dm), lambda i: (i, 0))
    y = pl.pallas_call(
        _rwkv_post_kernel,
        name="rwkv_post",
        grid=(t // tp,),
        in_specs=[flat, flat, flat, const((1, dm)), const((1, dm)), const((dm, dm))],
        out_specs=flat,
        out_shape=jax.ShapeDtypeStruct((t, dm), F32),
        compiler_params=_cparams(("parallel",)),
    )(o.reshape(t, dm), bonus.reshape(t, dm), g.reshape(t, dm), row(ln_w), row(ln_b), ones_d)
    return y.reshape(b, s, dm)


MOE_SLOT_BLOCK = 512
MOE_FF_CHUNK = 256
MOE_ROW_TILE = 256
MOE_DMA_UNROLL = 8
_META_E0, _META_E1, _META_R0, _META_R1, _META_W0, _META_W1 = range(6)


def _moe_route_kernel(x_ref, g_ref, wr_ref, tril_ref, h_o, meta_o, cnt_o, carry_ref):
    @pl.when(pl.program_id(0) == 0)
    def _():
        carry_ref[...] = jnp.zeros_like(carry_ref)

    x = x_ref[...]
    ms = jnp.mean(x * x, axis=-1, keepdims=True)
    h = x * lax.rsqrt(ms + NORM_EPS) * g_ref[...]
    h_o[...] = h
    logits = jnp.dot(h, wr_ref[...], precision=HIGHEST, preferred_element_type=F32)
    lane = lax.broadcasted_iota(jnp.int32, logits.shape, 1)
    l1 = jnp.where(lane < N_EXPERTS, logits, NEG)
    m1 = jnp.max(l1, axis=-1, keepdims=True)
    i1 = jnp.min(jnp.where(l1 == m1, lane, LANES), axis=-1, keepdims=True)
    l2 = jnp.where(lane == i1, NEG, l1)
    m2 = jnp.max(l2, axis=-1, keepdims=True)
    i2 = jnp.min(jnp.where(l2 == m2, lane, LANES), axis=-1, keepdims=True)
    e21 = jnp.exp(m2 - m1)
    w1 = 1.0 / (1.0 + e21)
    w2 = e21 * w1
    cnt = jnp.where((lane == i1) | (lane == i2), 1.0, 0.0)
    before = _dot(tril_ref[...], cnt.astype(BF16)) + carry_ref[...]
    r1 = jnp.sum(jnp.where(lane == i1, before, 0.0), axis=-1, keepdims=True)
    r2 = jnp.sum(jnp.where(lane == i2, before, 0.0), axis=-1, keepdims=True)
    carry_ref[...] += jnp.sum(cnt, axis=0, keepdims=True)
    cnt_o[...] = carry_ref[...]
    meta = jnp.zeros(logits.shape, F32)
    for idx, val in ((_META_E0, i1.astype(F32)), (_META_E1, i2.astype(F32)), (_META_R0, r1),
                     (_META_R1, r2), (_META_W0, w1), (_META_W1, w2)):
        meta = jnp.where(lane == idx, val, meta)
    meta_o[...] = meta


def _row_copy(src_ref, src_row, dst_ref, dst_row, sem):
    return pltpu.make_async_copy(src_ref.at[pl.ds(src_row, 1), :], dst_ref.at[pl.ds(dst_row, 1), :], sem)


def _moe_scatter_kernel(d0_ref, d1_ref, h_ref, xs_in_ref, xs_ref, sem):
    del xs_in_ref
    tm = h_ref.shape[0]

    def issue(r, c):
        _row_copy(h_ref, r, xs_ref, d0_ref[r], sem).start(priority=0)
        _row_copy(h_ref, r, xs_ref, d1_ref[r], sem).start(priority=1)
        return c

    def drain(r, c):
        _row_copy(h_ref, r, xs_ref, d0_ref[r], sem).wait()
        _row_copy(h_ref, r, xs_ref, d1_ref[r], sem).wait()
        return c

    lax.fori_loop(0, tm, issue, 0, unroll=MOE_DMA_UNROLL)
    lax.fori_loop(0, tm, drain, 0, unroll=MOE_DMA_UNROLL)


def _moe_expert_kernel(be_ref, nused_ref, xs_ref, wg_ref, wu_ref, wd_ref, ys_ref):
    del be_ref
    i = pl.program_id(0)

    @pl.when(i < nused_ref[0])
    def _():
        x16 = xs_ref[...].astype(BF16)
        acc = jnp.zeros(ys_ref.shape, F32)
        for c in range(D_FF // MOE_FF_CHUNK):
            cols = slice(c * MOE_FF_CHUNK, (c + 1) * MOE_FF_CHUNK)
            a = _silu(_dot(x16, wg_ref[0, :, cols])) * _dot(x16, wu_ref[0, :, cols])
            acc = acc + _dot(a.astype(BF16), wd_ref[0, cols, :])
        ys_ref[...] = acc

    @pl.when(i >= nused_ref[0])
    def _():
        ys_ref[...] = jnp.zeros_like(ys_ref)


def _moe_combine_kernel(d0_ref, d1_ref, x_ref, meta_ref, ys_ref, o_ref, b0_ref, b1_ref, sem):
    tm = x_ref.shape[0]

    def issue(r, c):
        _row_copy(ys_ref, d0_ref[r], b0_ref, r, sem).start(priority=0)
        _row_copy(ys_ref, d1_ref[r], b1_ref, r, sem).start(priority=1)
        return c

    def drain(r, c):
        _row_copy(ys_ref, d0_ref[r], b0_ref, r, sem).wait()
        _row_copy(ys_ref, d1_ref[r], b1_ref, r, sem).wait()
        return c

    lax.fori_loop(0, tm, issue, 0, unroll=MOE_DMA_UNROLL)
    lax.fori_loop(0, tm, drain, 0, unroll=MOE_DMA_UNROLL)
    w0 = meta_ref[:, _META_W0:_META_W0 + 1]
    w1 = meta_ref[:, _META_W1:_META_W1 + 1]
    o_ref[...] = x_ref[...] + (w0 * b0_ref[...] + w1 * b1_ref[...])


def moe_ffn(x, g, router, w_gate, w_up, w_down):
    t, d = x.shape
    tm = _row_tile(t, MOE_ROW_TILE)
    blk = MOE_SLOT_BLOCK
    nblk = -(-(t * TOP_K + N_EXPERTS * (blk - 1)) // blk)
    slots = nblk * blk
    tril = jnp.asarray(np.tril(np.ones((tm, tm), np.float32), -1), BF16)
    const = lambda shape: pl.BlockSpec(shape, lambda *_: (0,) * len(shape))
    row = lambda w: pl.BlockSpec((tm, w), lambda i: (i, 0))
    h, meta, counts = pl.pallas_call(
        _moe_route_kernel,
        name="moe_route",
        grid=(t // tm,),
        in_specs=[row(d), const((1, d)), const((d, LANES)), const((tm, tm))],
        out_specs=[row(d), row(LANES), const((1, LANES))],
        out_shape=[jax.ShapeDtypeStruct((t, d), F32), jax.ShapeDtypeStruct((t, LANES), F32),
                   jax.ShapeDtypeStruct((1, LANES), F32)],
        scratch_shapes=[pltpu.VMEM((1, LANES), F32)],
        compiler_params=_cparams(("arbitrary",)),
    )(x, g.reshape(1, d), _pad_cols(router.astype(F32), LANES), tril)

    cnt = counts[0, :N_EXPERTS].astype(jnp.int32)
    pcnt = (cnt + blk - 1) // blk * blk
    pend = jnp.cumsum(pcnt)
    pstart = pend - pcnt
    e0 = meta[:, _META_E0].astype(jnp.int32)
    e1 = meta[:, _META_E1].astype(jnp.int32)
    d0 = pstart[e0] + meta[:, _META_R0].astype(jnp.int32)
    d1 = pstart[e1] + meta[:, _META_R1].astype(jnp.int32)
    blk_start = jnp.arange(nblk, dtype=jnp.int32) * blk
    blk_e = jnp.minimum(jnp.sum((pend[None, :] <= blk_start[:, None]).astype(jnp.int32), axis=1),
                        N_EXPERTS - 1).astype(jnp.int32)
    nused = (pend[-1:] // blk).astype(jnp.int32)

    smem_rows = pl.BlockSpec((tm,), lambda i: (i,), memory_space=pltpu.SMEM)
    xs = pl.pallas_call(
        _moe_scatter_kernel,
        name="moe_scatter",
        grid=(t // tm,),
        in_specs=[smem_rows, smem_rows, row(d), pl.BlockSpec(memory_space=pl.ANY)],
        out_specs=pl.BlockSpec(memory_space=pl.ANY),
        out_shape=jax.ShapeDtypeStruct((slots, d), F32),
        scratch_shapes=[pltpu.SemaphoreType.DMA(())],
        input_output_aliases={3: 0},
        compiler_params=_cparams(("arbitrary",)),
    )(d0, d1, h, jnp.zeros((slots, d), F32))

    f = w_gate.shape[2]
    ys = pl.pallas_call(
        _moe_expert_kernel,
        name="moe_expert",
        grid_spec=pltpu.PrefetchScalarGridSpec(
            num_scalar_prefetch=2,
            grid=(nblk,),
            in_specs=[pl.BlockSpec((blk, d), lambda i, be, nu: (i, 0)),
                      pl.BlockSpec((1, d, f), lambda i, be, nu: (be[i], 0, 0)),
                      pl.BlockSpec((1, d, f), lambda i, be, nu: (be[i], 0, 0)),
                      pl.BlockSpec((1, f, d), lambda i, be, nu: (be[i], 0, 0))],
            out_specs=pl.BlockSpec((blk, d), lambda i, be, nu: (i, 0)),
        ),
        out_shape=jax.ShapeDtypeStruct((slots, d), F32),
        compiler_params=_cparams(("arbitrary",)),
    )(blk_e, nused, xs, w_gate, w_up, w_down)

    return pl.pallas_call(
        _moe_combine_kernel,
        name="moe_combine",
        grid=(t // tm,),
        in_specs=[smem_rows, smem_rows, row(d), row(LANES), pl.BlockSpec(memory_space=pl.ANY)],
        out_specs=row(d),
        out_shape=jax.ShapeDtypeStruct((t, d), F32),
        scratch_shapes=[pltpu.VMEM((tm, d), F32), pltpu.VMEM((tm, d), F32), pltpu.SemaphoreType.DMA(())],
        compiler_params=_cparams(("arbitrary",)),
    )(d0, d1, x, meta, ys)


_IN_SPLITS = (SSD_D_INNER, SSD_CONV_DIM, SSD_HEADS,
              NSA_HEADS * HEAD_DIM, 6 * NSA_KV_HEADS * HEAD_DIM, 3 * NSA_HEADS,
              RWKV_IN,
              SWA_HEADS * HEAD_DIM, 2 * SWA_KV_HEADS * HEAD_DIM,
              N_BRANCHES * D_MODEL)
_IN_OFF = tuple(int(o) for o in np.cumsum((0,) + _IN_SPLITS))


def _split_in_proj(w):
    o = _IN_OFF
    seg = lambda a, b: w[:, o[a]:o[b]]
    w_ssd = jnp.concatenate([seg(0, 2), _pad_cols(seg(2, 3), LANES)], axis=1)
    w_nsa = jnp.concatenate([seg(3, 5), _pad_cols(seg(5, 6), LANES)], axis=1)
    return tuple(m.astype(BF16) for m in (w_ssd, w_nsa, seg(6, 7), seg(7, 9), seg(9, 10)))


def kernel(x, p, positions, norm_mix, w_in, ssd_conv_w, ssd_conv_b, ssd_dt_bias, ssd_a_log, ssd_d, ssd_norm, nsa_cmp_pe, nsa_cmp_w1, nsa_cmp_w2, rwkv_mu, rwkv_w0, rwkv_w_up, rwkv_a0, rwkv_a_up, rwkv_g_up, rwkv_k_k, rwkv_k_a, rwkv_r_k, rwkv_ln_w, rwkv_ln_b, swa_sinks, w_br_ssd, w_br_nsa, w_br_rwkv, w_br_swa, w_out, norm_ffn, ffn_w_gate, ffn_w_up, ffn_w_down, moe_router, moe_w_gate, moe_w_up, moe_w_down, ple_proj, ple_gate, norm_final):
    b, s, d = x.shape
    t = b * s
    depth = w_in.shape[0]
    xf = x.reshape(t, d)
    rope_cos, rope_sin = rope_tables(positions)
    for i in range(depth):
        w_ssd, w_nsa, w_rwkv, w_swa, w_gates = _split_in_proj(w_in[i])
        g_mix = norm_mix[i]
        u_ssd = norm_matmul(xf, g_mix, w_ssd).reshape(b, s, -1)
        u_nsa = norm_matmul(xf, g_mix, w_nsa).reshape(b, s, -1)
        u_rwkv = norm_matmul(xf, g_mix, w_rwkv).reshape(b, s, -1)
        u_swa = norm_matmul(xf, g_mix, w_swa).reshape(b, s, -1)
        y_ssd = ssd_mixer(u_ssd, ssd_conv_w[i], ssd_conv_b[i], ssd_dt_bias[i], ssd_a_log[i], ssd_d[i], ssd_norm[i])
        y_nsa = nsa_mixer(u_nsa, nsa_cmp_pe[i], nsa_cmp_w1[i], nsa_cmp_w2[i])
        y_rwkv = rwkv7_mixer(u_rwkv, rwkv_mu[i], rwkv_w0[i], rwkv_w_up[i], rwkv_a0[i], rwkv_a_up[i],
                             rwkv_g_up[i], rwkv_k_k[i], rwkv_k_a[i], rwkv_r_k[i], rwkv_ln_w[i], rwkv_ln_b[i])
        y_swa = swa_mixer(u_swa, rope_cos, rope_sin, swa_sinks[i])
        p_stack = jnp.stack([w_br_ssd[i], w_br_nsa[i], w_br_rwkv[i], w_br_swa[i]]).astype(BF16)
        ys = [y.reshape(t, -1) for y in (y_ssd, y_nsa, y_rwkv, y_swa)]
        xf = merge_branches(xf, g_mix, w_gates, ys, p_stack, w_out[i].astype(BF16))
        j = i // 2
        if i % 2 == 0:
            xf = dense_ffn(xf, norm_ffn[i], ffn_w_gate[j].astype(BF16), ffn_w_up[j].astype(BF16),
                           ffn_w_down[j].astype(BF16))
        else:
            xf = moe_ffn(xf, norm_ffn[i], moe_router[j], moe_w_gate[j].astype(BF16),
                         moe_w_up[j].astype(BF16), moe_w_down[j].astype(BF16))
        xf = ple(xf, p[i].reshape(t, -1), ple_gate[i].astype(BF16), ple_proj[i].astype(BF16),
                 norm_final if i == depth - 1 else None)
    return xf.reshape(b, s, d)
```

```python
import functools

import numpy as np
import jax
import jax.numpy as jnp
from jax import lax
from jax.experimental import pallas as pl
from jax.experimental.pallas import tpu as pltpu

F32 = jnp.float32
BF16 = jnp.bfloat16

D_MODEL = 1024
HEAD_DIM = 64
NORM_EPS = 1e-6
NEG = -1e30
BIG = 1e30

SSD_HEADS = 8
SSD_D_INNER = SSD_HEADS * HEAD_DIM
SSD_STATE = 128
SSD_GROUPS = 2
SSD_CONV = 4
SSD_CHUNK = 128
SSD_CONV_DIM = SSD_D_INNER + 2 * SSD_GROUPS * SSD_STATE

NSA_HEADS = 8
NSA_KV_HEADS = 2
NSA_CMP_BLOCK = 32
NSA_CMP_STRIDE = 16
NSA_CMP_HIDDEN = 64
NSA_SEL_BLOCK = 64
NSA_TOPK = 8
NSA_WINDOW = 512

RWKV_HEADS = 8
RWKV_DIM = RWKV_HEADS * HEAD_DIM
RWKV_W_LORA = 64
RWKV_A_LORA = 64
RWKV_G_LORA = 128
RWKV_IN = 3 * RWKV_DIM + RWKV_W_LORA + RWKV_A_LORA + RWKV_G_LORA
RWKV_GN_EPS = 64e-5

SWA_HEADS = 8
SWA_KV_HEADS = 2
SWA_WINDOW = 128
ROPE_THETA = 150000.0

N_BRANCHES = 4
D_FF = 2816
N_EXPERTS = 8
TOP_K = 2
PLE_DIM = 256

LANES = 128
SUBLANES = 8
VMEM_LIMIT = 56 * 1024 * 1024

HIGHEST = lax.Precision.HIGHEST


def _cparams(sem):
    return pltpu.CompilerParams(dimension_semantics=sem, vmem_limit_bytes=VMEM_LIMIT)


def _sigmoid(x):
    return 1.0 / (1.0 + jnp.exp(-x))


def _silu(x):
    return x * _sigmoid(x)


def _softplus(x):
    return jnp.maximum(x, 0.0) + jnp.log1p(jnp.exp(-jnp.abs(x)))


def _dot(a, b):
    return jnp.dot(a, b, preferred_element_type=F32)


def _dot_nt(a, b):
    return lax.dot_general(a, b, (((1,), (1,)), ((), ())), preferred_element_type=F32)


def _pad_cols(w, n):
    return jnp.pad(w, ((0, 0), (0, n - w.shape[1])))


def _row_tile(t, pref):
    while t % pref:
        pref //= 2
    return pref


def _rms_bf16(x, g):
    ms = jnp.mean(x * x, axis=-1, keepdims=True)
    return (x * lax.rsqrt(ms + NORM_EPS) * g).astype(BF16)


def _norm_matmul_kernel(x_ref, g_ref, w_ref, o_ref):
    o_ref[...] = _dot(_rms_bf16(x_ref[...], g_ref[...]), w_ref[...])


def norm_matmul(x, g, w):
    t, d = x.shape
    n = w.shape[1]
    tm = _row_tile(t, 1024)
    return pl.pallas_call(
        _norm_matmul_kernel,
        name="norm_matmul",
        grid=(t // tm,),
        in_specs=[
            pl.BlockSpec((tm, d), lambda i: (i, 0)),
            pl.BlockSpec((1, d), lambda i: (0, 0)),
            pl.BlockSpec((d, n), lambda i: (0, 0)),
        ],
        out_specs=pl.BlockSpec((tm, n), lambda i: (i, 0)),
        out_shape=jax.ShapeDtypeStruct((t, n), F32),
        compiler_params=_cparams(("parallel",)),
    )(x, g.reshape(1, d), w)


def _merge_kernel(x_ref, g_ref, wg_ref, y0_ref, y1_ref, y2_ref, y3_ref, p_ref, wo_ref, o_ref):
    x = x_ref[...]
    h = _rms_bf16(x, g_ref[...])
    acc = None
    for m, y_ref in enumerate((y0_ref, y1_ref, y2_ref, y3_ref)):
        pm = _dot(y_ref[...].astype(BF16), p_ref[m])
        gm = _sigmoid(_dot(h, wg_ref[:, m * D_MODEL:(m + 1) * D_MODEL]))
        acc = gm * pm if acc is None else acc + gm * pm
    o_ref[...] = x + _dot(acc.astype(BF16), wo_ref[...])


def merge_branches(x, g, w_gates, ys, p_stack, w_out):
    t, d = x.shape
    tm = _row_tile(t, 512)
    dm = ys[0].shape[1]
    row = lambda w: pl.BlockSpec((tm, w), lambda i: (i, 0))
    const = lambda shape: pl.BlockSpec(shape, lambda i: (0,) * len(shape))
    return pl.pallas_call(
        _merge_kernel,
        name="merge",
        grid=(t // tm,),
        in_specs=[row(d), const((1, d)), const((d, N_BRANCHES * d)), row(dm), row(dm), row(dm), row(dm),
                  const((N_BRANCHES, dm, d)), const((d, d))],
        out_specs=row(d),
        out_shape=jax.ShapeDtypeStruct((t, d), F32),
        compiler_params=_cparams(("parallel",)),
    )(x, g.reshape(1, d), w_gates, *ys, p_stack, w_out)


def _ffn_kernel(x_ref, g_ref, wg_ref, wu_ref, wd_ref, o_ref, h_ref, acc_ref):
    j = pl.program_id(1)

    @pl.when(j == 0)
    def _():
        x = x_ref[...]
        ms = jnp.mean(x * x, axis=-1, keepdims=True)
        h_ref[...] = (x * lax.rsqrt(ms + NORM_EPS) * g_ref[...]).astype(BF16)
        acc_ref[...] = jnp.zeros_like(acc_ref)

    h = h_ref[...]
    a = _silu(_dot(h, wg_ref[...])) * _dot(h, wu_ref[...])
    acc_ref[...] += _dot(a.astype(BF16), wd_ref[...])

    @pl.when(j == pl.num_programs(1) - 1)
    def _():
        o_ref[...] = x_ref[...] + acc_ref[...]


def dense_ffn(x, g, w_gate, w_up, w_down):
    t, d = x.shape
    f = w_gate.shape[1]
    tm = _row_tile(t, 512)
    tf = f // 2 if (f // 2) % LANES == 0 else f
    return pl.pallas_call(
        _ffn_kernel,
        name="dense_ffn",
        grid=(t // tm, f // tf),
        in_specs=[
            pl.BlockSpec((tm, d), lambda i, j: (i, 0)),
            pl.BlockSpec((1, d), lambda i, j: (0, 0)),
            pl.BlockSpec((d, tf), lambda i, j: (0, j)),
            pl.BlockSpec((d, tf), lambda i, j: (0, j)),
            pl.BlockSpec((tf, d), lambda i, j: (j, 0)),
        ],
        out_specs=pl.BlockSpec((tm, d), lambda i, j: (i, 0)),
        out_shape=jax.ShapeDtypeStruct((t, d), F32),
        scratch_shapes=[pltpu.VMEM((tm, d), BF16), pltpu.VMEM((tm, d), F32)],
        compiler_params=_cparams(("parallel", "arbitrary")),
    )(x, g.reshape(1, d), w_gate, w_up, w_down)


def _ple_kernel(x_ref, p_ref, wg_ref, wp_ref, o_ref):
    x = x_ref[...]
    gate = _sigmoid(_dot(x.astype(BF16), wg_ref[...]))
    o_ref[...] = x + gate * _dot(p_ref[...].astype(BF16), wp_ref[...])


def _ple_final_kernel(x_ref, p_ref, wg_ref, wp_ref, nf_ref, o_ref):
    x = x_ref[...]
    gate = _sigmoid(_dot(x.astype(BF16), wg_ref[...]))
    y = x + gate * _dot(p_ref[...].astype(BF16), wp_ref[...])
    ms = jnp.mean(y * y, axis=-1, keepdims=True)
    o_ref[...] = y * lax.rsqrt(ms + NORM_EPS) * nf_ref[...]


def ple(x, p, w_gate, w_proj, norm_final=None):
    t, d = x.shape
    pd = p.shape[1]
    tm = _row_tile(t, 512)
    in_specs = [pl.BlockSpec((tm, d), lambda i: (i, 0)),
                pl.BlockSpec((tm, pd), lambda i: (i, 0)),
                pl.BlockSpec((d, d), lambda i: (0, 0)),
                pl.BlockSpec((pd, d), lambda i: (0, 0))]
    args = [x, p, w_gate, w_proj]
    body = _ple_kernel
    if norm_final is not None:
        in_specs.append(pl.BlockSpec((1, d), lambda i: (0, 0)))
        args.append(norm_final.reshape(1, d))
        body = _ple_final_kernel
    return pl.pallas_call(
        body,
        name="ple",
        grid=(t // tm,),
        in_specs=in_specs,
        out_specs=pl.BlockSpec((tm, d), lambda i: (i, 0)),
        out_shape=jax.ShapeDtypeStruct((t, d), F32),
        compiler_params=_cparams(("parallel",)),
    )(*args)


SSD_U_COLS = SSD_D_INNER + SSD_CONV_DIM + LANES
_SSD_GN = SSD_GROUPS * SSD_STATE


def _ssd_kernel(u_ref, cw_ref, cb_ref, dtb_ref, alog_ref, dsk_ref, nw_ref, tril_ref,
                o_ref, xpad_ref, state_ref, y_ref):
    L = SSD_CHUNK
    P = HEAD_DIM
    R = SSD_HEADS // SSD_GROUPS

    @pl.when(pl.program_id(1) == 0)
    def _():
        xpad_ref[0:SUBLANES, :] = jnp.zeros((SUBLANES, SSD_CONV_DIM), F32)
        state_ref[...] = jnp.zeros_like(state_ref)

    z = u_ref[0, :, 0:SSD_D_INNER]
    xbc = u_ref[0, :, SSD_D_INNER:SSD_D_INNER + SSD_CONV_DIM]
    dt_raw = u_ref[0, :, SSD_D_INNER + SSD_CONV_DIM:SSD_U_COLS]

    xpad_ref[SUBLANES:SUBLANES + L, :] = xbc
    conv = cb_ref[...]
    for j in range(SSD_CONV):
        conv = conv + cw_ref[j:j + 1, :] * xpad_ref[pl.ds(SUBLANES - (SSD_CONV - 1) + j, L), :]
    xpad_ref[0:SUBLANES, :] = xbc[L - SUBLANES:L, :]
    act = _silu(conv)
    xs = act[:, 0:SSD_D_INNER]
    bm = act[:, SSD_D_INNER:SSD_D_INNER + _SSD_GN]
    cm = act[:, SSD_D_INNER + _SSD_GN:SSD_CONV_DIM]

    dt = _softplus(dt_raw + dtb_ref[...])
    a_neg = -jnp.exp(alog_ref[...])
    tril = tril_ref[...]
    a_cum = jnp.dot(tril, dt * a_neg, precision=HIGHEST, preferred_element_type=F32)
    a_cum_t = a_cum.T
    dt_t = dt.T
    lower = tril > 0.5

    for g in range(SSD_GROUPS):
        bg = bm[:, g * SSD_STATE:(g + 1) * SSD_STATE]
        cg = cm[:, g * SSD_STATE:(g + 1) * SSD_STATE]
        bg16 = bg.astype(BF16)
        cg16 = cg.astype(BF16)
        cb = _dot_nt(cg16, bg16)
        bgt16 = bg.T.astype(BF16)
        for r in range(R):
            h = g * R + r
            a_col = a_cum[:, h:h + 1]
            a_row = a_cum_t[h:h + 1, :]
            a_last = a_cum[L - 1:L, h:h + 1]
            xh = xs[:, h * P:(h + 1) * P]
            decay = jnp.where(lower, jnp.exp(jnp.where(lower, a_col - a_row, 0.0)), 0.0)
            w_ls = cb * decay * dt_t[h:h + 1, :]
            y_diag = _dot(w_ls.astype(BF16), xh.astype(BF16))
            xw = xh * (jnp.exp(a_last - a_col) * dt[:, h:h + 1])
            st = _dot(bgt16, xw.astype(BF16))
            prev = state_ref[h]
            y_off = _dot(cg16, prev.astype(BF16)) * jnp.exp(a_col)
            state_ref[h] = prev * jnp.exp(a_last) + st
            y_ref[:, h * P:(h + 1) * P] = y_diag + y_off + dsk_ref[:, h:h + 1] * xh

    yg = y_ref[...] * _silu(z)
    gw = SSD_D_INNER // SSD_GROUPS
    for g in range(SSD_GROUPS):
        part = yg[:, g * gw:(g + 1) * gw]
        ms = jnp.mean(part * part, axis=-1, keepdims=True)
        o_ref[0, :, g * gw:(g + 1) * gw] = part * lax.rsqrt(ms + NORM_EPS) * nw_ref[:, g * gw:(g + 1) * gw]


def _lane_row(v):
    return jnp.pad(v.astype(F32), (0, LANES - v.shape[0])).reshape(1, LANES)


def ssd_mixer(u, conv_w, conv_b, dt_bias, a_log, d_skip, norm_w):
    b, s, _ = u.shape
    L = SSD_CHUNK
    tril = jnp.asarray(np.tril(np.ones((L, L), np.float32)))
    full = lambda shape: pl.BlockSpec(shape, lambda i, c: (0,) * len(shape))
    return pl.pallas_call(
        _ssd_kernel,
        name="ssd",
        grid=(b, s // L),
        in_specs=[
            pl.BlockSpec((1, L, SSD_U_COLS), lambda i, c: (i, c, 0)),
            full((SSD_CONV, SSD_CONV_DIM)), full((1, SSD_CONV_DIM)),
            full((1, LANES)), full((1, LANES)), full((1, LANES)),
            full((1, SSD_D_INNER)), full((L, L)),
        ],
        out_specs=pl.BlockSpec((1, L, SSD_D_INNER), lambda i, c: (i, c, 0)),
        out_shape=jax.ShapeDtypeStruct((b, s, SSD_D_INNER), F32),
        scratch_shapes=[pltpu.VMEM((SUBLANES + L, SSD_CONV_DIM), F32),
                        pltpu.VMEM((SSD_HEADS, SSD_STATE, HEAD_DIM), F32),
                        pltpu.VMEM((L, SSD_D_INNER), F32)],
        compiler_params=_cparams(("parallel", "arbitrary")),
    )(u, conv_w, conv_b.reshape(1, -1), _lane_row(dt_bias), _lane_row(a_log), _lane_row(d_skip),
      norm_w.reshape(1, -1), tril)


ATTN_TQ = 128
SWA_TQ = 256
_SCALE = HEAD_DIM ** -0.5


def _stack_heads(q, g, heads_per_group):
    parts = [q[:, (g * heads_per_group + r) * HEAD_DIM:(g * heads_per_group + r + 1) * HEAD_DIM]
             for r in range(heads_per_group)]
    return jnp.concatenate(parts, axis=0)


def _row_pos(t0, tq, reps):
    row = lax.broadcasted_iota(jnp.int32, (reps * tq, 1), 0)
    return t0 + (row & (tq - 1))


def _rope(x, cosf, sinf):
    n = x.shape[1]
    half = HEAD_DIM // 2
    lane = lax.broadcasted_iota(jnp.int32, x.shape, 1)
    first = (lane & (HEAD_DIM - 1)) < half
    rot = jnp.where(first, pltpu.roll(x, n - half, 1), pltpu.roll(x, half, 1))
    return x * cosf + rot * sinf


def _rope_table_kernel(pos_ref, cos_ref, sin_ref):
    half = HEAD_DIM // 2
    lane = lax.broadcasted_iota(jnp.int32, (1, LANES), 1)
    expo = -(lane & (half - 1)).astype(F32) / half
    inv_freq = jnp.power(jnp.full((1, LANES), ROPE_THETA, F32), expo)
    sign = jnp.where((lane & (HEAD_DIM - 1)) < half, -1.0, 1.0)
    ang = pos_ref[0] * inv_freq
    cos_ref[0] = jnp.cos(ang)
    sin_ref[0] = jnp.sin(ang) * sign


def rope_tables(positions):
    b, s = positions.shape
    tm = _row_tile(s, 512)
    out = pl.BlockSpec((1, tm, LANES), lambda i, j: (i, j, 0))
    return pl.pallas_call(
        _rope_table_kernel,
        name="rope_tables",
        grid=(b, s // tm),
        in_specs=[pl.BlockSpec((1, tm, 1), lambda i, j: (i, j, 0))],
        out_specs=[out, out],
        out_shape=[jax.ShapeDtypeStruct((b, s, LANES), F32)] * 2,
        compiler_params=_cparams(("parallel", "parallel")),
    )(positions.astype(F32).reshape(b, s, 1))


def _swa_kernel(q_ref, kv_ref, cosq_ref, sinq_ref, cosk_ref, sink_ref, sinks_ref, o_ref):
    tq = q_ref.shape[1]
    W = SWA_WINDOW
    span = W + tq
    R = SWA_HEADS // SWA_KV_HEADS
    kw = SWA_KV_HEADS * HEAD_DIM
    t0 = pl.program_id(1) * tq
    start = pl.multiple_of(jnp.maximum(t0 - W, 0), SUBLANES)

    reps = q_ref.shape[2] // LANES
    qr = _rope(q_ref[0], jnp.concatenate([cosq_ref[0]] * reps, 1), jnp.concatenate([sinq_ref[0]] * reps, 1))
    qr = qr * _SCALE
    kvs = kv_ref[0, pl.ds(start, span), :]
    kr = _rope(kvs[:, 0:kw], cosk_ref[0, pl.ds(start, span), :], sink_ref[0, pl.ds(start, span), :])
    v = kvs[:, kw:2 * kw]

    rel = _row_pos(t0, tq, 1) - (start + lax.broadcasted_iota(jnp.int32, (1, span), 1))
    bias = jnp.where((rel >= 0) & (rel < W), 0.0, NEG)
    for g in range(SWA_KV_HEADS):
        qg = _stack_heads(qr, g, R).astype(BF16)
        s = _dot_nt(qg, kr[:, g * HEAD_DIM:(g + 1) * HEAD_DIM].astype(BF16)).reshape(R, tq, span) + bias
        snk = jnp.concatenate([jnp.full((1, tq, 1), sinks_ref[g * R + r], F32) for r in range(R)], 0)
        m = jnp.maximum(jnp.max(s, axis=-1, keepdims=True), snk)
        e = jnp.exp(s - m)
        den = jnp.sum(e, axis=-1, keepdims=True) + jnp.exp(snk - m)
        o = _dot(e.reshape(R * tq, span).astype(BF16), v[:, g * HEAD_DIM:(g + 1) * HEAD_DIM].astype(BF16))
        o = o.reshape(R, tq, HEAD_DIM) / den
        for r in range(R):
            h = g * R + r
            o_ref[0, :, h * HEAD_DIM:(h + 1) * HEAD_DIM] = o[r]


def swa_mixer(u, rope_cos, rope_sin, sinks):
    b, s, _ = u.shape
    tq = SWA_TQ
    qw = SWA_HEADS * HEAD_DIM
    kvw = 2 * SWA_KV_HEADS * HEAD_DIM
    tile = pl.BlockSpec((1, tq, LANES), lambda i, j: (i, j, 0))
    full = pl.BlockSpec((1, s, LANES), lambda i, j: (i, 0, 0))
    return pl.pallas_call(
        _swa_kernel,
        name="swa",
        grid=(b, s // tq),
        in_specs=[
            pl.BlockSpec((1, tq, qw), lambda i, j: (i, j, 0)),
            pl.BlockSpec((1, s, kvw), lambda i, j: (i, 0, qw // kvw)),
            tile, tile, full, full,
            pl.BlockSpec(memory_space=pltpu.SMEM),
        ],
        out_specs=pl.BlockSpec((1, tq, qw), lambda i, j: (i, j, 0)),
        out_shape=jax.ShapeDtypeStruct((b, s, qw), F32),
        compiler_params=_cparams(("parallel", "arbitrary")),
    )(u, u, rope_cos, rope_sin, rope_cos, rope_sin, sinks.astype(F32))


NSA_U_COLS = NSA_HEADS * HEAD_DIM + 6 * NSA_KV_HEADS * HEAD_DIM + LANES
_NSA_KVW = NSA_KV_HEADS * HEAD_DIM
_NSA_QBLK = NSA_HEADS * HEAD_DIM // _NSA_KVW
_NSA_R = NSA_HEADS // NSA_KV_HEADS
_CMP_HALF = NSA_CMP_BLOCK // 2
NSA_KEY_CHUNK = 512


def _nsa_compress_kernel(k_ref, v_ref, pe_ref, w1_ref, w2_ref, kc_ref, vc_ref):
    nc = kc_ref.shape[1]
    for idx, (x_ref, o_ref) in enumerate(((k_ref, kc_ref), (v_ref, vc_ref))):
        ha = jnp.zeros((nc, _NSA_KVW), F32)
        hb = jnp.zeros((nc, _NSA_KVW), F32)
        for l in range(_CMP_HALF):
            y = x_ref[0, pl.ds(l, nc, stride=NSA_CMP_STRIDE), :]
            ha = ha + _dot((y + pe_ref[idx, l:l + 1, :]).astype(BF16), w1_ref[idx, l])
            hb = hb + _dot((y + pe_ref[idx, _CMP_HALF + l:_CMP_HALF + l + 1, :]).astype(BF16),
                           w1_ref[idx, _CMP_HALF + l])
        hid = _silu(ha + pltpu.roll(hb, nc - 1, 0))
        o_ref[0] = _dot(hid.astype(BF16), w2_ref[idx])


def _nsa_select_kernel(q_ref, kc_ref, vc_ref, ov_ref, ocmp_ref, sel_ref):
    tq = q_ref.shape[1]
    nc = kc_ref.shape[1]
    nsel = ov_ref.shape[0]
    k_eff = min(NSA_TOPK, nsel)
    t0 = pl.program_id(1) * tq
    t_row = _row_pos(t0, tq, _NSA_R)
    cmp_end = lax.broadcasted_iota(jnp.int32, (1, nc), 1) * NSA_CMP_STRIDE + (NSA_CMP_BLOCK - 1)
    cvalid = cmp_end <= t_row
    jidx = lax.broadcasted_iota(jnp.int32, (nsel, 1), 0)
    blk_t = (t0 + lax.broadcasted_iota(jnp.int32, (1, tq), 1)) // NSA_SEL_BLOCK
    jvalid = jidx <= blk_t
    forced = (jidx == 0) | (jidx == blk_t)
    q = q_ref[0]
    sel_rows = []
    for g in range(NSA_KV_HEADS):
        qg = _stack_heads(q, g, _NSA_R).astype(BF16)
        kc = kc_ref[0, :, g * HEAD_DIM:(g + 1) * HEAD_DIM].astype(BF16)
        vc = vc_ref[0, :, g * HEAD_DIM:(g + 1) * HEAD_DIM].astype(BF16)
        s = _dot_nt(qg, kc) * _SCALE
        m = jnp.max(jnp.where(cvalid, s, NEG), axis=-1, keepdims=True)
        e = jnp.exp(jnp.where(cvalid, s - m, NEG))
        den = jnp.sum(e, axis=-1, keepdims=True)
        p = e / jnp.where(den > 0.0, den, 1.0)
        o = _dot(p.astype(BF16), vc)
        psum = p[0:tq]
        for r in range(_NSA_R):
            h = g * _NSA_R + r
            ocmp_ref[0, :, h * HEAD_DIM:(h + 1) * HEAD_DIM] = o[r * tq:(r + 1) * tq]
            if r:
                psum = psum + p[r * tq:(r + 1) * tq]
        imp_t = lax.dot_general(ov_ref[...], psum, (((1,), (1,)), ((), ())),
                                precision=HIGHEST, preferred_element_type=F32)
        score = jnp.where(forced, BIG, jnp.where(jvalid, imp_t, NEG))
        cnt = jnp.zeros((nsel, tq), F32)
        for i in range(nsel):
            si = score[i:i + 1, :]
            beats = (si > score) | ((si == score) & (jidx > i))
            cnt = cnt + jnp.where(beats, 1.0, 0.0)
        sel_rows.append(jnp.where(cnt < k_eff, 1.0, 0.0))
    pad = LANES - NSA_KV_HEADS * nsel
    sel_t = jnp.concatenate(sel_rows + [jnp.zeros((pad, tq), F32)], axis=0)
    sel_ref[0] = sel_t.T


def _nsa_attend_kernel(q_ref, sel_ref, gate_ref, ocmp_ref, ks_ref, vs_ref, kw_ref, vw_ref, ex_ref, o_ref):
    tq = q_ref.shape[1]
    s_len = ks_ref.shape[1]
    W = NSA_WINDOW
    span = min(W + tq, s_len)
    t0 = pl.program_id(1) * tq
    start = pl.multiple_of(jnp.maximum(t0 - W, 0), SUBLANES)
    t_q = _row_pos(t0, tq, 1)
    rel = t_q - (start + lax.broadcasted_iota(jnp.int32, (1, span), 1))
    win_bias = jnp.where((rel >= 0) & (rel < W), 0.0, NEG)
    q = q_ref[0] * _SCALE
    sig = _sigmoid(gate_ref[0])
    sel16 = sel_ref[0].astype(BF16)

    def attend(qg, k, v, bias):
        n = k.shape[0]
        s = _dot_nt(qg, k.astype(BF16)).reshape(_NSA_R, tq, n) + bias
        e = jnp.exp(s - jnp.max(s, axis=-1, keepdims=True))
        den = jnp.sum(e, axis=-1, keepdims=True)
        o = _dot(e.reshape(_NSA_R * tq, n).astype(BF16), v.astype(BF16))
        return o.reshape(_NSA_R, tq, HEAD_DIM) / den

    chunk = min(NSA_KEY_CHUNK, s_len)
    for c in range(s_len // chunk):
        klen = (c + 1) * chunk

        @pl.when(t0 // chunk == c)
        def _():
            kidx = lax.broadcasted_iota(jnp.int32, (1, klen), 1)
            for g in range(NSA_KV_HEADS):
                cols = slice(g * HEAD_DIM, (g + 1) * HEAD_DIM)
                qg = _stack_heads(q, g, _NSA_R).astype(BF16)
                chosen = _dot(sel16, ex_ref[g, :, 0:klen]) > 0.5
                sel_bias = jnp.where(chosen & (kidx <= t_q), 0.0, NEG)
                o_sel = attend(qg, ks_ref[0, 0:klen, cols], vs_ref[0, 0:klen, cols], sel_bias)
                o_win = attend(qg, kw_ref[0, pl.ds(start, span), cols], vw_ref[0, pl.ds(start, span), cols],
                               win_bias)
                for r in range(_NSA_R):
                    h = g * _NSA_R + r
                    hc = slice(h * HEAD_DIM, (h + 1) * HEAD_DIM)
                    o_ref[0, :, hc] = (sig[:, h:h + 1] * ocmp_ref[0, :, hc]
                                       + sig[:, NSA_HEADS + h:NSA_HEADS + h + 1] * o_sel[r]
                                       + sig[:, 2 * NSA_HEADS + h:2 * NSA_HEADS + h + 1] * o_win[r])


def _block_diag2(w):
    z = jnp.zeros_like(w)
    return jnp.concatenate([jnp.concatenate([w, z], -1), jnp.concatenate([z, w], -1)], -2)


def nsa_mixer(u, cmp_pe, cmp_w1, cmp_w2):
    b, s, _ = u.shape
    tq = ATTN_TQ
    nc = s // NSA_CMP_STRIDE
    nsel = s // NSA_SEL_BLOCK
    qw = NSA_HEADS * HEAD_DIM
    pe2 = jnp.concatenate([cmp_pe, cmp_pe], -1)
    w1bd = _block_diag2(cmp_w1).astype(BF16)
    w2bd = _block_diag2(cmp_w2).astype(BF16)
    const = lambda shape: pl.BlockSpec(shape, lambda *_: (0,) * len(shape))
    col = lambda c: pl.BlockSpec((1, s, _NSA_KVW), lambda i, *_: (i, 0, _NSA_QBLK + c))

    kc, vc = pl.pallas_call(
        _nsa_compress_kernel,
        name="nsa_compress",
        grid=(b,),
        in_specs=[col(0), col(1), const(pe2.shape), const(w1bd.shape), const(w2bd.shape)],
        out_specs=[pl.BlockSpec((1, nc, _NSA_KVW), lambda i: (i, 0, 0))] * 2,
        out_shape=[jax.ShapeDtypeStruct((b, nc, _NSA_KVW), F32)] * 2,
        compiler_params=_cparams(("parallel",)),
    )(u, u, pe2, w1bd, w2bd)

    c_start = np.arange(nc) * NSA_CMP_STRIDE
    s_start = np.arange(nsel) * NSA_SEL_BLOCK
    ov_t = ((c_start[None, :] < s_start[:, None] + NSA_SEL_BLOCK)
            & (c_start[None, :] + NSA_CMP_BLOCK > s_start[:, None])
            & (np.arange(nc)[None, :] < nc - 1)).astype(np.float32)
    qspec = pl.BlockSpec((1, tq, qw), lambda i, j: (i, j, 0))
    o_cmp, sel = pl.pallas_call(
        _nsa_select_kernel,
        name="nsa_select",
        grid=(b, s // tq),
        in_specs=[qspec,
                  pl.BlockSpec((1, nc, _NSA_KVW), lambda i, j: (i, 0, 0)),
                  pl.BlockSpec((1, nc, _NSA_KVW), lambda i, j: (i, 0, 0)),
                  const(ov_t.shape)],
        out_specs=[qspec, pl.BlockSpec((1, tq, LANES), lambda i, j: (i, j, 0))],
        out_shape=[jax.ShapeDtypeStruct((b, s, qw), F32), jax.ShapeDtypeStruct((b, s, LANES), F32)],
        compiler_params=_cparams(("parallel", "arbitrary")),
    )(u, kc, vc, jnp.asarray(ov_t))

    expand = np.zeros((NSA_KV_HEADS, LANES, s), np.float32)
    for g in range(NSA_KV_HEADS):
        expand[g, g * nsel + np.arange(s) // NSA_SEL_BLOCK, np.arange(s)] = 1.0
    gate_blk = NSA_U_COLS // LANES - 1
    return pl.pallas_call(
        _nsa_attend_kernel,
        name="nsa_attend",
        grid=(b, s // tq),
        in_specs=[qspec,
                  pl.BlockSpec((1, tq, LANES), lambda i, j: (i, j, 0)),
                  pl.BlockSpec((1, tq, LANES), lambda i, j: (i, j, gate_blk)),
                  qspec,
                  col(2), col(3), col(4), col(5),
                  const(expand.shape)],
        out_specs=qspec,
        out_shape=jax.ShapeDtypeStruct((b, s, qw), F32),
        compiler_params=_cparams(("parallel", "arbitrary")),
    )(u, sel, u, o_cmp, u, u, u, u, jnp.asarray(expand, BF16))


def _seg_sum(x, ones16):
    hi = x.astype(BF16)
    lo = (x - hi.astype(F32)).astype(BF16)
    return _dot(hi, ones16) + _dot(lo, ones16)


def _head_ones(width):
    idx = np.arange(width) // HEAD_DIM
    return jnp.asarray((idx[:, None] == idx[None, :]).astype(np.float32), BF16)


_RW_R, _RW_K, _RW_V = 0, RWKV_DIM, 2 * RWKV_DIM
_RW_WD = 3 * RWKV_DIM
_RW_AD = _RW_WD + RWKV_W_LORA
_RW_GD = _RW_AD + RWKV_A_LORA


def _rwkv_prep_kernel(u_ref, mu_ref, w0_ref, wup_ref, a0_ref, aup_ref, gup_ref, kk_ref, ka_ref, rk_ref,
                      ones_ref, r_o, w_o, k_o, v_o, kk_o, q_o, g_o, bonus_o, up_ref):
    tm = u_ref.shape[1]

    @pl.when(pl.program_id(1) == 0)
    def _():
        up_ref[0:SUBLANES, :] = jnp.zeros((SUBLANES, RWKV_IN), F32)

    u = u_ref[0]
    up_ref[SUBLANES:SUBLANES + tm, :] = u
    prev = up_ref[pl.ds(SUBLANES - 1, tm), :]
    up_ref[0:SUBLANES, :] = u[tm - SUBLANES:tm, :]
    x = u + (prev - u) * mu_ref[...]
    r = x[:, _RW_R:_RW_R + RWKV_DIM]
    k = x[:, _RW_K:_RW_K + RWKV_DIM]
    v = x[:, _RW_V:_RW_V + RWKV_DIM]
    wd = x[:, _RW_WD:_RW_AD]
    ad = x[:, _RW_AD:_RW_GD]
    gd = x[:, _RW_GD:RWKV_IN]
    w = -_softplus(-(w0_ref[...] + _dot(jnp.tanh(wd).astype(BF16), wup_ref[...]))) - 0.5
    a = _sigmoid(a0_ref[...] + _dot(ad.astype(BF16), aup_ref[...]))
    ones16 = ones_ref[...]
    kk = k * kk_ref[...]
    kk = kk / jnp.maximum(jnp.sqrt(_seg_sum(kk * kk, ones16)), 1e-12)
    k2 = k * (1.0 + (a - 1.0) * ka_ref[...])
    r_o[0] = r
    w_o[0] = jnp.exp(-jnp.exp(w))
    k_o[0] = k2
    v_o[0] = v
    kk_o[0] = kk
    q_o[0] = kk * a
    g_o[0] = _dot(_sigmoid(gd).astype(BF16), gup_ref[...])
    bonus_o[0] = _seg_sum(r * k2 * rk_ref[...], ones16) * v


_RWS_TB = 64
_RWS_VB = HEAD_DIM // 2 // SUBLANES
_RWS_NACC = 2
_RWS_VECS = 5


def _rwkv_scan_vhalf_kernel(kk_ref, w_ref, q_ref, k_ref, r_ref, v_ref, o_ref, s_ref, rows_ref):
    @pl.when(pl.program_id(0) == 0)
    def _():
        s_ref[...] = jnp.zeros_like(s_ref)

    low = lax.broadcasted_iota(jnp.int32, (SUBLANES, LANES), 1) < LANES // 2

    def reduce_acc(acc, vb):
        tot = acc[(vb, 0)]
        for i in range(1, _RWS_NACC):
            tot = tot + acc[(vb, i)]
        return tot

    def group(tg, carry):
        t8 = pl.multiple_of(tg * SUBLANES, SUBLANES)
        for a, ref in enumerate((kk_ref, w_ref, q_ref, k_ref, r_ref)):
            for i in range(HEAD_DIM // 2):
                x = ref[i, pl.ds(t8, SUBLANES), :]
                xr = pltpu.roll(x, LANES // 2, 1)
                rows_ref[a, 0, i] = jnp.where(low, x, xr)
                rows_ref[a, 1, i] = jnp.where(low, xr, x)
        for tl in range(SUBLANES):
            t = t8 + tl
            key_row = lambda a, k: rows_ref[a, k % 2, k // 2, tl:tl + 1, :]
            acc = {}
            for k in range(HEAD_DIM):
                kk_row = key_row(0, k)
                for vb in range(_RWS_VB):
                    term = s_ref[vb, k] * kk_row
                    key = (vb, k % _RWS_NACC)
                    acc[key] = acc[key] + term if key in acc else term
            sa = [reduce_acc(acc, vb) for vb in range(_RWS_VB)]
            vt = [v_ref[t, vb * SUBLANES:(vb + 1) * SUBLANES, :] for vb in range(_RWS_VB)]
            acc = {}
            for k in range(HEAD_DIM):
                w_row, q_row, k_row, r_row = (key_row(a, k) for a in range(1, _RWS_VECS))
                for vb in range(_RWS_VB):
                    st = s_ref[vb, k] * w_row - sa[vb] * q_row + vt[vb] * k_row
                    s_ref[vb, k] = st
                    term = st * r_row
                    key = (vb, k % _RWS_NACC)
                    acc[key] = acc[key] + term if key in acc else term
            for vb in range(_RWS_VB):
                o_ref[t, vb * SUBLANES:(vb + 1) * SUBLANES, :] = reduce_acc(acc, vb)
        return carry

    lax.fori_loop(0, o_ref.shape[0] // SUBLANES, group, 0)


_RL_TB = 128
_RL_ROWS = HEAD_DIM // 2


def _scan_row_base(i, half, interleaved):
    return 2 * i + half if interleaved else half * _RL_ROWS + i


def _to_scan_kernel(*refs, interleaved):
    n = len(interleaved)
    ins, outs, y_ref = refs[:n], refs[n:2 * n], refs[2 * n]
    nb, tb = ins[0].shape[0], ins[0].shape[1]
    for a in range(n):
        for b in range(nb):
            y_ref[b] = ins[a][b].T
        for i in range(_RL_ROWS):
            parts = [y_ref[b, pl.ds(_scan_row_base(i, half, interleaved[a]), RWKV_HEADS, stride=HEAD_DIM), :]
                     for half in range(2) for b in range(nb)]
            rows = jnp.concatenate(parts, axis=0).T
            if interleaved[a]:
                outs[a][i] = rows
            else:
                outs[a][pl.ds(i, tb, stride=_RL_ROWS), :] = rows


def _from_scan_kernel(o_ref, out_ref, y_ref):
    nb, tb = out_ref.shape[0], out_ref.shape[1]
    for i in range(_RL_ROWS):
        zt = o_ref[pl.ds(i, tb, stride=_RL_ROWS), :].T
        for half in range(2):
            for b in range(nb):
                r0 = (half * nb + b) * RWKV_HEADS
                y_ref[b, pl.ds(_scan_row_base(i, half, False), RWKV_HEADS, stride=HEAD_DIM), :] = zt[r0:r0 + RWKV_HEADS]
    for b in range(nb):
        out_ref[b] = y_ref[b].T


def to_scan_layout(arrays, interleaved):
    b, s, dm = arrays[0].shape
    tb = _row_tile(s, _RL_TB)
    n = len(arrays)
    key_spec = pl.BlockSpec((_RL_ROWS, tb, LANES), lambda i: (0, i, 0))
    val_spec = pl.BlockSpec((tb * _RL_ROWS, LANES), lambda i: (i, 0))
    key_shape = jax.ShapeDtypeStruct((_RL_ROWS, s, LANES), F32)
    val_shape = jax.ShapeDtypeStruct((s * _RL_ROWS, LANES), F32)
    outs = pl.pallas_call(
        functools.partial(_to_scan_kernel, interleaved=tuple(interleaved)),
        name="rwkv_to_scan",
        grid=(s // tb,),
        in_specs=[pl.BlockSpec((b, tb, dm), lambda i: (0, i, 0))] * n,
        out_specs=[key_spec if f else val_spec for f in interleaved],
        out_shape=[key_shape if f else val_shape for f in interleaved],
        scratch_shapes=[pltpu.VMEM((b, dm, tb), F32)],
        compiler_params=_cparams(("parallel",)),
    )(*arrays)
    return [o if f else o.reshape(s, _RL_ROWS, LANES) for o, f in zip(outs, interleaved)]


def from_scan_layout(o, b):
    s = o.shape[0]
    dm = RWKV_DIM
    tb = _row_tile(s, _RL_TB)
    return pl.pallas_call(
        _from_scan_kernel,
        name="rwkv_from_scan",
        grid=(s // tb,),
        in_specs=[pl.BlockSpec((tb * _RL_ROWS, LANES), lambda i: (i, 0))],
        out_specs=pl.BlockSpec((b, tb, dm), lambda i: (0, i, 0)),
        out_shape=jax.ShapeDtypeStruct((b, s, dm), F32),
        scratch_shapes=[pltpu.VMEM((b, dm, tb), F32)],
        compiler_params=_cparams(("parallel",)),
    )(o.reshape(s * _RL_ROWS, LANES))


def _rwkv_post_kernel(o_ref, bonus_ref, g_ref, lnw_ref, lnb_ref, ones_ref, y_ref):
    ones16 = ones_ref[...]
    o = o_ref[...]
    mean = _seg_sum(o, ones16) * (1.0 / HEAD_DIM)
    cen = o - mean
    var = _seg_sum(cen * cen, ones16) * (1.0 / HEAD_DIM)
    o = cen * lax.rsqrt(var + RWKV_GN_EPS) * lnw_ref[...] + lnb_ref[...]
    y_ref[...] = (o + bonus_ref[...]) * g_ref[...]


def rwkv7_mixer(u, mu, w0, w_up, a0, a_up, g_up, k_k, k_a, r_k, ln_w, ln_b):
    b, s, _ = u.shape
    dm = RWKV_DIM
    tm = _row_tile(s, 256)
    row = lambda v: v.astype(F32).reshape(1, -1)
    const = lambda shape: pl.BlockSpec(shape, lambda *_: (0,) * len(shape))
    ones_d = _head_ones(dm)
    tile = pl.BlockSpec((1, tm, dm), lambda i, j: (i, j, 0))
    outs = pl.pallas_call(
        _rwkv_prep_kernel,
        name="rwkv_prep",
        grid=(b, s // tm),
        in_specs=[pl.BlockSpec((1, tm, RWKV_IN), lambda i, j: (i, j, 0)),
                  const((1, RWKV_IN)), const((1, dm)), const((RWKV_W_LORA, dm)), const((1, dm)),
                  const((RWKV_A_LORA, dm)), const((RWKV_G_LORA, dm)), const((1, dm)), const((1, dm)),
                  const((1, dm)), const((dm, dm))],
        out_specs=[tile] * 8,
        out_shape=[jax.ShapeDtypeStruct((b, s, dm), F32)] * 8,
        scratch_shapes=[pltpu.VMEM((SUBLANES + tm, RWKV_IN), F32)],
        compiler_params=_cparams(("parallel", "arbitrary")),
    )(u, row(mu), row(w0), w_up.astype(BF16), row(a0), a_up.astype(BF16), g_up.astype(BF16),
      row(k_k), row(k_a), row(r_k), ones_d)
    r, wdec, k2, v, kk, q, g, bonus = outs

    nchain = b * RWKV_HEADS
    assert 2 * nchain == LANES and s % 2 == 0, "the scan kernel maps (half, batch, head) onto the 128 lanes"
    scan_in = to_scan_layout([kk, wdec, q, k2, r, v], interleaved=[True] * _RWS_VECS + [False])
    tb = _row_tile(s, _RWS_TB)
    kblk = pl.BlockSpec((_RL_ROWS, tb, LANES), lambda i: (0, i, 0))
    vblk = pl.BlockSpec((tb, _RL_ROWS, LANES), lambda i: (i, 0, 0))
    o = pl.pallas_call(
        _rwkv_scan_vhalf_kernel,
        name="rwkv_scan",
        grid=(s // tb,),
        in_specs=[kblk] * _RWS_VECS + [vblk],
        out_specs=vblk,
        out_shape=jax.ShapeDtypeStruct((s, _RL_ROWS, LANES), F32),
        scratch_shapes=[pltpu.VMEM((_RWS_VB, HEAD_DIM, SUBLANES, LANES), F32),
                        pltpu.VMEM((_RWS_VECS, 2, _RL_ROWS, SUBLANES, LANES), F32)],
        compiler_params=_cparams(("arbitrary",)),
    )(*scan_in)
    o = from_scan_layout(o, b)

    t = b * s
    tp = _row_tile(t, 512)
    flat = pl.BlockSpec((tp, dm), lambda i: (i, 0))
    y = pl.pallas_call(
        _rwkv_post_kernel,
        name="rwkv_post",
        grid=(t // tp,),
        in_specs=[flat, flat, flat, const((1, dm)), const((1, dm)), const((dm, dm))],
        out_specs=flat,
        out_shape=jax.ShapeDtypeStruct((t, dm), F32),
        compiler_params=_cparams(("parallel",)),
    )(o.reshape(t, dm), bonus.reshape(t, dm), g.reshape(t, dm), row(ln_w), row(ln_b), ones_d)
    return y.reshape(b, s, dm)


MOE_SLOT_BLOCK = 512
MOE_FF_CHUNK = 256
MOE_ROW_TILE = 256
MOE_DMA_UNROLL = 8
_META_E0, _META_E1, _META_R0, _META_R1, _META_W0, _META_W1 = range(6)


def _moe_route_kernel(x_ref, g_ref, wr_ref, tril_ref, h_o, meta_o, cnt_o, carry_ref):
    @pl.when(pl.program_id(0) == 0)
    def _():
        carry_ref[...] = jnp.zeros_like(carry_ref)

    x = x_ref[...]
    ms = jnp.mean(x * x, axis=-1, keepdims=True)
    h = x * lax.rsqrt(ms + NORM_EPS) * g_ref[...]
    h_o[...] = h
    logits = jnp.dot(h, wr_ref[...], precision=HIGHEST, preferred_element_type=F32)
    lane = lax.broadcasted_iota(jnp.int32, logits.shape, 1)
    l1 = jnp.where(lane < N_EXPERTS, logits, NEG)
    m1 = jnp.max(l1, axis=-1, keepdims=True)
    i1 = jnp.min(jnp.where(l1 == m1, lane, LANES), axis=-1, keepdims=True)
    l2 = jnp.where(lane == i1, NEG, l1)
    m2 = jnp.max(l2, axis=-1, keepdims=True)
    i2 = jnp.min(jnp.where(l2 == m2, lane, LANES), axis=-1, keepdims=True)
    e21 = jnp.exp(m2 - m1)
    w1 = 1.0 / (1.0 + e21)
    w2 = e21 * w1
    cnt = jnp.where((lane == i1) | (lane == i2), 1.0, 0.0)
    before = _dot(tril_ref[...], cnt.astype(BF16)) + carry_ref[...]
    r1 = jnp.sum(jnp.where(lane == i1, before, 0.0), axis=-1, keepdims=True)
    r2 = jnp.sum(jnp.where(lane == i2, before, 0.0), axis=-1, keepdims=True)
    carry_ref[...] += jnp.sum(cnt, axis=0, keepdims=True)
    cnt_o[...] = carry_ref[...]
    meta = jnp.zeros(logits.shape, F32)
    for idx, val in ((_META_E0, i1.astype(F32)), (_META_E1, i2.astype(F32)), (_META_R0, r1),
                     (_META_R1, r2), (_META_W0, w1), (_META_W1, w2)):
        meta = jnp.where(lane == idx, val, meta)
    meta_o[...] = meta


def _row_copy(src_ref, src_row, dst_ref, dst_row, sem):
    return pltpu.make_async_copy(src_ref.at[pl.ds(src_row, 1), :], dst_ref.at[pl.ds(dst_row, 1), :], sem)


def _moe_scatter_kernel(d0_ref, d1_ref, h_ref, xs_in_ref, xs_ref, sem):
    del xs_in_ref
    tm = h_ref.shape[0]

    def issue(r, c):
        _row_copy(h_ref, r, xs_ref, d0_ref[r], sem).start(priority=0)
        _row_copy(h_ref, r, xs_ref, d1_ref[r], sem).start(priority=1)
        return c

    def drain(r, c):
        _row_copy(h_ref, r, xs_ref, d0_ref[r], sem).wait()
        _row_copy(h_ref, r, xs_ref, d1_ref[r], sem).wait()
        return c

    lax.fori_loop(0, tm, issue, 0, unroll=MOE_DMA_UNROLL)
    lax.fori_loop(0, tm, drain, 0, unroll=MOE_DMA_UNROLL)


def _moe_expert_kernel(be_ref, nused_ref, xs_ref, wg_ref, wu_ref, wd_ref, ys_ref):
    del be_ref
    i = pl.program_id(0)

    @pl.when(i < nused_ref[0])
    def _():
        x16 = xs_ref[...].astype(BF16)
        acc = jnp.zeros(ys_ref.shape, F32)
        for c in range(D_FF // MOE_FF_CHUNK):
            cols = slice(c * MOE_FF_CHUNK, (c + 1) * MOE_FF_CHUNK)
            a = _silu(_dot(x16, wg_ref[0, :, cols])) * _dot(x16, wu_ref[0, :, cols])
            acc = acc + _dot(a.astype(BF16), wd_ref[0, cols, :])
        ys_ref[...] = acc

    @pl.when(i >= nused_ref[0])
    def _():
        ys_ref[...] = jnp.zeros_like(ys_ref)


def _moe_combine_kernel(d0_ref, d1_ref, x_ref, meta_ref, ys_ref, o_ref, b0_ref, b1_ref, sem):
    tm = x_ref.shape[0]

    def issue(r, c):
        _row_copy(ys_ref, d0_ref[r], b0_ref, r, sem).start(priority=0)
        _row_copy(ys_ref, d1_ref[r], b1_ref, r, sem).start(priority=1)
        return c

    def drain(r, c):
        _row_copy(ys_ref, d0_ref[r], b0_ref, r, sem).wait()
        _row_copy(ys_ref, d1_ref[r], b1_ref, r, sem).wait()
        return c

    lax.fori_loop(0, tm, issue, 0, unroll=MOE_DMA_UNROLL)
    lax.fori_loop(0, tm, drain, 0, unroll=MOE_DMA_UNROLL)
    w0 = meta_ref[:, _META_W0:_META_W0 + 1]
    w1 = meta_ref[:, _META_W1:_META_W1 + 1]
    o_ref[...] = x_ref[...] + (w0 * b0_ref[...] + w1 * b1_ref[...])


def moe_ffn(x, g, router, w_gate, w_up, w_down):
    t, d = x.shape
    tm = _row_tile(t, MOE_ROW_TILE)
    blk = MOE_SLOT_BLOCK
    nblk = -(-(t * TOP_K + N_EXPERTS * (blk - 1)) // blk)
    slots = nblk * blk
    tril = jnp.asarray(np.tril(np.ones((tm, tm), np.float32), -1), BF16)
    const = lambda shape: pl.BlockSpec(shape, lambda *_: (0,) * len(shape))
    row = lambda w: pl.BlockSpec((tm, w), lambda i: (i, 0))
    h, meta, counts = pl.pallas_call(
        _moe_route_kernel,
        name="moe_route",
        grid=(t // tm,),
        in_specs=[row(d), const((1, d)), const((d, LANES)), const((tm, tm))],
        out_specs=[row(d), row(LANES), const((1, LANES))],
        out_shape=[jax.ShapeDtypeStruct((t, d), F32), jax.ShapeDtypeStruct((t, LANES), F32),
                   jax.ShapeDtypeStruct((1, LANES), F32)],
        scratch_shapes=[pltpu.VMEM((1, LANES), F32)],
        compiler_params=_cparams(("arbitrary",)),
    )(x, g.reshape(1, d), _pad_cols(router.astype(F32), LANES), tril)

    cnt = counts[0, :N_EXPERTS].astype(jnp.int32)
    pcnt = (cnt + blk - 1) // blk * blk
    pend = jnp.cumsum(pcnt)
    pstart = pend - pcnt
    e0 = meta[:, _META_E0].astype(jnp.int32)
    e1 = meta[:, _META_E1].astype(jnp.int32)
    d0 = pstart[e0] + meta[:, _META_R0].astype(jnp.int32)
    d1 = pstart[e1] + meta[:, _META_R1].astype(jnp.int32)
    blk_start = jnp.arange(nblk, dtype=jnp.int32) * blk
    blk_e = jnp.minimum(jnp.sum((pend[None, :] <= blk_start[:, None]).astype(jnp.int32), axis=1),
                        N_EXPERTS - 1).astype(jnp.int32)
    nused = (pend[-1:] // blk).astype(jnp.int32)

    smem_rows = pl.BlockSpec((tm,), lambda i: (i,), memory_space=pltpu.SMEM)
    xs = pl.pallas_call(
        _moe_scatter_kernel,
        name="moe_scatter",
        grid=(t // tm,),
        in_specs=[smem_rows, smem_rows, row(d), pl.BlockSpec(memory_space=pl.ANY)],
        out_specs=pl.BlockSpec(memory_space=pl.ANY),
        out_shape=jax.ShapeDtypeStruct((slots, d), F32),
        scratch_shapes=[pltpu.SemaphoreType.DMA(())],
        input_output_aliases={3: 0},
        compiler_params=_cparams(("arbitrary",)),
    )(d0, d1, h, jnp.zeros((slots, d), F32))

    f = w_gate.shape[2]
    ys = pl.pallas_call(
        _moe_expert_kernel,
        name="moe_expert",
        grid_spec=pltpu.PrefetchScalarGridSpec(
            num_scalar_prefetch=2,
            grid=(nblk,),
            in_specs=[pl.BlockSpec((blk, d), lambda i, be, nu: (i, 0)),
                      pl.BlockSpec((1, d, f), lambda i, be, nu: (be[i], 0, 0)),
                      pl.BlockSpec((1, d, f), lambda i, be, nu: (be[i], 0, 0)),
                      pl.BlockSpec((1, f, d), lambda i, be, nu: (be[i], 0, 0))],
            out_specs=pl.BlockSpec((blk, d), lambda i, be, nu: (i, 0)),
        ),
        out_shape=jax.ShapeDtypeStruct((slots, d), F32),
        compiler_params=_cparams(("arbitrary",)),
    )(blk_e, nused, xs, w_gate, w_up, w_down)

    return pl.pallas_call(
        _moe_combine_kernel,
        name="moe_combine",
        grid=(t // tm,),
        in_specs=[smem_rows, smem_rows, row(d), row(LANES), pl.BlockSpec(memory_space=pl.ANY)],
        out_specs=row(d),
        out_shape=jax.ShapeDtypeStruct((t, d), F32),
        scratch_shapes=[pltpu.VMEM((tm, d), F32), pltpu.VMEM((tm, d), F32), pltpu.SemaphoreType.DMA(())],
        compiler_params=_cparams(("arbitrary",)),
    )(d0, d1, x, meta, ys)


_IN_SPLITS = (SSD_D_INNER, SSD_CONV_DIM, SSD_HEADS,
              NSA_HEADS * HEAD_DIM, 6 * NSA_KV_HEADS * HEAD_DIM, 3 * NSA_HEADS,
              RWKV_IN,
              SWA_HEADS * HEAD_DIM, 2 * SWA_KV_HEADS * HEAD_DIM,
              N_BRANCHES * D_MODEL)
_IN_OFF = tuple(int(o) for o in np.cumsum((0,) + _IN_SPLITS))


def _split_in_proj(w):
    o = _IN_OFF
    seg = lambda a, b: w[:, o[a]:o[b]]
    w_ssd = jnp.concatenate([seg(0, 2), _pad_cols(seg(2, 3), LANES)], axis=1)
    w_nsa = jnp.concatenate([seg(3, 5), _pad_cols(seg(5, 6), LANES)], axis=1)
    return tuple(m.astype(BF16) for m in (w_ssd, w_nsa, seg(6, 7), seg(7, 9), seg(9, 10)))


def kernel(x, p, positions, norm_mix, w_in, ssd_conv_w, ssd_conv_b, ssd_dt_bias, ssd_a_log, ssd_d, ssd_norm, nsa_cmp_pe, nsa_cmp_w1, nsa_cmp_w2, rwkv_mu, rwkv_w0, rwkv_w_up, rwkv_a0, rwkv_a_up, rwkv_g_up, rwkv_k_k, rwkv_k_a, rwkv_r_k, rwkv_ln_w, rwkv_ln_b, swa_sinks, w_br_ssd, w_br_nsa, w_br_rwkv, w_br_swa, w_out, norm_ffn, ffn_w_gate, ffn_w_up, ffn_w_down, moe_router, moe_w_gate, moe_w_up, moe_w_down, ple_proj, ple_gate, norm_final):
    b, s, d = x.shape
    t = b * s
    depth = w_in.shape[0]
    xf = x.reshape(t, d)
    rope_cos, rope_sin = rope_tables(positions)
    for i in range(depth):
        w_ssd, w_nsa, w_rwkv, w_swa, w_gates = _split_in_proj(w_in[i])
        g_mix = norm_mix[i]
        u_ssd = norm_matmul(xf, g_mix, w_ssd).reshape(b, s, -1)
        u_nsa = norm_matmul(xf, g_mix, w_nsa).reshape(b, s, -1)
        u_rwkv = norm_matmul(xf, g_mix, w_rwkv).reshape(b, s, -1)
        u_swa = norm_matmul(xf, g_mix, w_swa).reshape(b, s, -1)
        y_ssd = ssd_mixer(u_ssd, ssd_conv_w[i], ssd_conv_b[i], ssd_dt_bias[i], ssd_a_log[i], ssd_d[i], ssd_norm[i])
        y_nsa = nsa_mixer(u_nsa, nsa_cmp_pe[i], nsa_cmp_w1[i], nsa_cmp_w2[i])
        y_rwkv = rwkv7_mixer(u_rwkv, rwkv_mu[i], rwkv_w0[i], rwkv_w_up[i], rwkv_a0[i], rwkv_a_up[i],
                             rwkv_g_up[i], rwkv_k_k[i], rwkv_k_a[i], rwkv_r_k[i], rwkv_ln_w[i], rwkv_ln_b[i])
        y_swa = swa_mixer(u_swa, rope_cos, rope_sin, swa_sinks[i])
        p_stack = jnp.stack([w_br_ssd[i], w_br_nsa[i], w_br_rwkv[i], w_br_swa[i]]).astype(BF16)
        ys = [y.reshape(t, -1) for y in (y_ssd, y_nsa, y_rwkv, y_swa)]
        xf = merge_branches(xf, g_mix, w_gates, ys, p_stack, w_out[i].astype(BF16))
        j = i // 2
        if i % 2 == 0:
            xf = dense_ffn(xf, norm_ffn[i], ffn_w_gate[j].astype(BF16), ffn_w_up[j].astype(BF16),
                           ffn_w_down[j].astype(BF16))
        else:
            xf = moe_ffn(xf, norm_ffn[i], moe_router[j], moe_w_gate[j].astype(BF16),
                         moe_w_up[j].astype(BF16), moe_w_down[j].astype(BF16))
        xf = ple(xf, p[i].reshape(t, -1), ple_gate[i].astype(BF16), ple_proj[i].astype(BF16),
                 norm_final if i == depth - 1 else None)
    return xf.reshape(b, s, d)
```

```python
import functools

import numpy as np
import jax
import jax.numpy as jnp
from jax import lax
from jax.experimental import pallas as pl
from jax.experimental.pallas import tpu as pltpu

F32 = jnp.float32
BF16 = jnp.bfloat16

D_MODEL = 1024
HEAD_DIM = 64
NORM_EPS = 1e-6
NEG = -1e30
BIG = 1e30

SSD_HEADS = 8
SSD_D_INNER = SSD_HEADS * HEAD_DIM
SSD_STATE = 128
SSD_GROUPS = 2
SSD_CONV = 4
SSD_CHUNK = 128
SSD_CONV_DIM = SSD_D_INNER + 2 * SSD_GROUPS * SSD_STATE

NSA_HEADS = 8
NSA_KV_HEADS = 2
NSA_CMP_BLOCK = 32
NSA_CMP_STRIDE = 16
NSA_CMP_HIDDEN = 64
NSA_SEL_BLOCK = 64
NSA_TOPK = 8
NSA_WINDOW = 512

RWKV_HEADS = 8
RWKV_DIM = RWKV_HEADS * HEAD_DIM
RWKV_W_LORA = 64
RWKV_A_LORA = 64
RWKV_G_LORA = 128
RWKV_IN = 3 * RWKV_DIM + RWKV_W_LORA + RWKV_A_LORA + RWKV_G_LORA
RWKV_GN_EPS = 64e-5

SWA_HEADS = 8
SWA_KV_HEADS = 2
SWA_WINDOW = 128
ROPE_THETA = 150000.0

N_BRANCHES = 4
D_FF = 2816
N_EXPERTS = 8
TOP_K = 2
PLE_DIM = 256

LANES = 128
SUBLANES = 8
VMEM_LIMIT = 56 * 1024 * 1024

HIGHEST = lax.Precision.HIGHEST


def _cparams(sem):
    return pltpu.CompilerParams(dimension_semantics=sem, vmem_limit_bytes=VMEM_LIMIT)


def _sigmoid(x):
    return 1.0 / (1.0 + jnp.exp(-x))


def _silu(x):
    return x * _sigmoid(x)


def _softplus(x):
    return jnp.maximum(x, 0.0) + jnp.log1p(jnp.exp(-jnp.abs(x)))


def _dot(a, b):
    return jnp.dot(a, b, preferred_element_type=F32)


def _dot_nt(a, b):
    return lax.dot_general(a, b, (((1,), (1,)), ((), ())), preferred_element_type=F32)


def _pad_cols(w, n):
    return jnp.pad(w, ((0, 0), (0, n - w.shape[1])))


def _row_tile(t, pref):
    while t % pref:
        pref //= 2
    return pref


def _rms_bf16(x, g):
    ms = jnp.mean(x * x, axis=-1, keepdims=True)
    return (x * lax.rsqrt(ms + NORM_EPS) * g).astype(BF16)


def _norm_matmul_kernel(x_ref, g_ref, w_ref, o_ref):
    o_ref[...] = _dot(_rms_bf16(x_ref[...], g_ref[...]), w_ref[...])


def norm_matmul(x, g, w):
    t, d = x.shape
    n = w.shape[1]
    tm = _row_tile(t, 1024)
    return pl.pallas_call(
        _norm_matmul_kernel,
        name="norm_matmul",
        grid=(t // tm,),
        in_specs=[
            pl.BlockSpec((tm, d), lambda i: (i, 0)),
            pl.BlockSpec((1, d), lambda i: (0, 0)),
            pl.BlockSpec((d, n), lambda i: (0, 0)),
        ],
        out_specs=pl.BlockSpec((tm, n), lambda i: (i, 0)),
        out_shape=jax.ShapeDtypeStruct((t, n), F32),
        compiler_params=_cparams(("parallel",)),
    )(x, g.reshape(1, d), w)


def _merge_kernel(x_ref, g_ref, wg_ref, y0_ref, y1_ref, y2_ref, y3_ref, p_ref, wo_ref, o_ref):
    x = x_ref[...]
    h = _rms_bf16(x, g_ref[...])
    acc = None
    for m, y_ref in enumerate((y0_ref, y1_ref, y2_ref, y3_ref)):
        pm = _dot(y_ref[...].astype(BF16), p_ref[m])
        gm = _sigmoid(_dot(h, wg_ref[:, m * D_MODEL:(m + 1) * D_MODEL]))
        acc = gm * pm if acc is None else acc + gm * pm
    o_ref[...] = x + _dot(acc.astype(BF16), wo_ref[...])


def merge_branches(x, g, w_gates, ys, p_stack, w_out):
    t, d = x.shape
    tm = _row_tile(t, 512)
    dm = ys[0].shape[1]
    row = lambda w: pl.BlockSpec((tm, w), lambda i: (i, 0))
    const = lambda shape: pl.BlockSpec(shape, lambda i: (0,) * len(shape))
    return pl.pallas_call(
        _merge_kernel,
        name="merge",
        grid=(t // tm,),
        in_specs=[row(d), const((1, d)), const((d, N_BRANCHES * d)), row(dm), row(dm), row(dm), row(dm),
                  const((N_BRANCHES, dm, d)), const((d, d))],
        out_specs=row(d),
        out_shape=jax.ShapeDtypeStruct((t, d), F32),
        compiler_params=_cparams(("parallel",)),
    )(x, g.reshape(1, d), w_gates, *ys, p_stack, w_out)


def _ple_epilogue(y, p_ref, pg_ref, pp_ref, nf_ref):
    gate = _sigmoid(_dot(y.astype(BF16), pg_ref[...]))
    y = y + gate * _dot(p_ref[...].astype(BF16), pp_ref[...])
    if nf_ref is not None:
        ms = jnp.mean(y * y, axis=-1, keepdims=True)
        y = y * lax.rsqrt(ms + NORM_EPS) * nf_ref[...]
    return y


def _ple_operands(p, ple_gate, ple_proj, norm_final, tm, index_map):
    d = ple_gate.shape[0]
    const = lambda shape: pl.BlockSpec(shape, lambda *_: (0,) * len(shape))
    specs = [pl.BlockSpec((tm, p.shape[1]), index_map), const((d, d)), const((p.shape[1], d))]
    args = [p, ple_gate, ple_proj]
    if norm_final is not None:
        specs.append(const((1, d)))
        args.append(norm_final.reshape(1, d))
    return specs, args


def _ffn_kernel(x_ref, g_ref, wg_ref, wu_ref, wd_ref, p_ref, pg_ref, pp_ref, *rest):
    nf_ref = rest[0] if len(rest) == 4 else None
    o_ref, h_ref, acc_ref = rest[-3:]
    j = pl.program_id(1)

    @pl.when(j == 0)
    def _():
        h_ref[...] = _rms_bf16(x_ref[...], g_ref[...])
        acc_ref[...] = jnp.zeros_like(acc_ref)

    h = h_ref[...]
    a = _silu(_dot(h, wg_ref[...])) * _dot(h, wu_ref[...])
    acc_ref[...] += _dot(a.astype(BF16), wd_ref[...])

    @pl.when(j == pl.num_programs(1) - 1)
    def _():
        o_ref[...] = _ple_epilogue(x_ref[...] + acc_ref[...], p_ref, pg_ref, pp_ref, nf_ref)


def dense_ffn(x, g, w_gate, w_up, w_down, p, ple_gate, ple_proj, norm_final=None):
    t, d = x.shape
    f = w_gate.shape[1]
    tm = _row_tile(t, 512)
    tf = f // 2 if (f // 2) % LANES == 0 else f
    ple_specs, ple_args = _ple_operands(p, ple_gate, ple_proj, norm_final, tm, lambda i, j: (i, 0))
    return pl.pallas_call(
        _ffn_kernel,
        name="dense_ffn",
        grid=(t // tm, f // tf),
        in_specs=[
            pl.BlockSpec((tm, d), lambda i, j: (i, 0)),
            pl.BlockSpec((1, d), lambda i, j: (0, 0)),
            pl.BlockSpec((d, tf), lambda i, j: (0, j)),
            pl.BlockSpec((d, tf), lambda i, j: (0, j)),
            pl.BlockSpec((tf, d), lambda i, j: (j, 0)),
        ] + ple_specs,
        out_specs=pl.BlockSpec((tm, d), lambda i, j: (i, 0)),
        out_shape=jax.ShapeDtypeStruct((t, d), F32),
        scratch_shapes=[pltpu.VMEM((tm, d), BF16), pltpu.VMEM((tm, d), F32)],
        compiler_params=_cparams(("parallel", "arbitrary")),
    )(x, g.reshape(1, d), w_gate, w_up, w_down, *ple_args)


SSD_U_COLS = SSD_D_INNER + SSD_CONV_DIM + LANES
_SSD_GN = SSD_GROUPS * SSD_STATE


def _ssd_kernel(u_ref, cw_ref, cb_ref, dtb_ref, alog_ref, dsk_ref, nw_ref, tril_ref,
                o_ref, xpad_ref, state_ref, y_ref):
    L = SSD_CHUNK
    P = HEAD_DIM
    R = SSD_HEADS // SSD_GROUPS

    @pl.when(pl.program_id(1) == 0)
    def _():
        xpad_ref[0:SUBLANES, :] = jnp.zeros((SUBLANES, SSD_CONV_DIM), F32)
        state_ref[...] = jnp.zeros_like(state_ref)

    z = u_ref[0, :, 0:SSD_D_INNER]
    xbc = u_ref[0, :, SSD_D_INNER:SSD_D_INNER + SSD_CONV_DIM]
    dt_raw = u_ref[0, :, SSD_D_INNER + SSD_CONV_DIM:SSD_U_COLS]

    xpad_ref[SUBLANES:SUBLANES + L, :] = xbc
    conv = cb_ref[...]
    for j in range(SSD_CONV):
        conv = conv + cw_ref[j:j + 1, :] * xpad_ref[pl.ds(SUBLANES - (SSD_CONV - 1) + j, L), :]
    xpad_ref[0:SUBLANES, :] = xbc[L - SUBLANES:L, :]
    act = _silu(conv)
    xs = act[:, 0:SSD_D_INNER]
    bm = act[:, SSD_D_INNER:SSD_D_INNER + _SSD_GN]
    cm = act[:, SSD_D_INNER + _SSD_GN:SSD_CONV_DIM]

    dt = _softplus(dt_raw + dtb_ref[...])
    a_neg = -jnp.exp(alog_ref[...])
    tril = tril_ref[...]
    a_cum = jnp.dot(tril, dt * a_neg, precision=HIGHEST, preferred_element_type=F32)
    a_cum_t = a_cum.T
    dt_t = dt.T
    lower = tril > 0.5

    for g in range(SSD_GROUPS):
        bg = bm[:, g * SSD_STATE:(g + 1) * SSD_STATE]
        cg = cm[:, g * SSD_STATE:(g + 1) * SSD_STATE]
        bg16 = bg.astype(BF16)
        cg16 = cg.astype(BF16)
        cb = _dot_nt(cg16, bg16)
        bgt16 = bg.T.astype(BF16)
        for r in range(R):
            h = g * R + r
            a_col = a_cum[:, h:h + 1]
            a_row = a_cum_t[h:h + 1, :]
            a_last = a_cum[L - 1:L, h:h + 1]
            xh = xs[:, h * P:(h + 1) * P]
            decay = jnp.where(lower, jnp.exp(jnp.where(lower, a_col - a_row, 0.0)), 0.0)
            w_ls = cb * decay * dt_t[h:h + 1, :]
            y_diag = _dot(w_ls.astype(BF16), xh.astype(BF16))
            xw = xh * (jnp.exp(a_last - a_col) * dt[:, h:h + 1])
            st = _dot(bgt16, xw.astype(BF16))
            prev = state_ref[h]
            y_off = _dot(cg16, prev.astype(BF16)) * jnp.exp(a_col)
            state_ref[h] = prev * jnp.exp(a_last) + st
            y_ref[:, h * P:(h + 1) * P] = y_diag + y_off + dsk_ref[:, h:h + 1] * xh

    yg = y_ref[...] * _silu(z)
    gw = SSD_D_INNER // SSD_GROUPS
    for g in range(SSD_GROUPS):
        part = yg[:, g * gw:(g + 1) * gw]
        ms = jnp.mean(part * part, axis=-1, keepdims=True)
        o_ref[0, :, g * gw:(g + 1) * gw] = part * lax.rsqrt(ms + NORM_EPS) * nw_ref[:, g * gw:(g + 1) * gw]


def _lane_row(v):
    return jnp.pad(v.astype(F32), (0, LANES - v.shape[0])).reshape(1, LANES)


def ssd_mixer(u, conv_w, conv_b, dt_bias, a_log, d_skip, norm_w):
    b, s, _ = u.shape
    L = SSD_CHUNK
    tril = jnp.asarray(np.tril(np.ones((L, L), np.float32)))
    full = lambda shape: pl.BlockSpec(shape, lambda i, c: (0,) * len(shape))
    return pl.pallas_call(
        _ssd_kernel,
        name="ssd",
        grid=(b, s // L),
        in_specs=[
            pl.BlockSpec((1, L, SSD_U_COLS), lambda i, c: (i, c, 0)),
            full((SSD_CONV, SSD_CONV_DIM)), full((1, SSD_CONV_DIM)),
            full((1, LANES)), full((1, LANES)), full((1, LANES)),
            full((1, SSD_D_INNER)), full((L, L)),
        ],
        out_specs=pl.BlockSpec((1, L, SSD_D_INNER), lambda i, c: (i, c, 0)),
        out_shape=jax.ShapeDtypeStruct((b, s, SSD_D_INNER), F32),
        scratch_shapes=[pltpu.VMEM((SUBLANES + L, SSD_CONV_DIM), F32),
                        pltpu.VMEM((SSD_HEADS, SSD_STATE, HEAD_DIM), F32),
                        pltpu.VMEM((L, SSD_D_INNER), F32)],
        compiler_params=_cparams(("parallel", "arbitrary")),
    )(u, conv_w, conv_b.reshape(1, -1), _lane_row(dt_bias), _lane_row(a_log), _lane_row(d_skip),
      norm_w.reshape(1, -1), tril)


ATTN_TQ = 128
SWA_TQ = 256
_SCALE = HEAD_DIM ** -0.5


def _stack_heads(q, g, heads_per_group):
    parts = [q[:, (g * heads_per_group + r) * HEAD_DIM:(g * heads_per_group + r + 1) * HEAD_DIM]
             for r in range(heads_per_group)]
    return jnp.concatenate(parts, axis=0)


def _row_pos(t0, tq, reps):
    row = lax.broadcasted_iota(jnp.int32, (reps * tq, 1), 0)
    return t0 + (row & (tq - 1))


def _rope(x, cosf, sinf):
    n = x.shape[1]
    half = HEAD_DIM // 2
    lane = lax.broadcasted_iota(jnp.int32, x.shape, 1)
    first = (lane & (HEAD_DIM - 1)) < half
    rot = jnp.where(first, pltpu.roll(x, n - half, 1), pltpu.roll(x, half, 1))
    return x * cosf + rot * sinf


def _rope_table_kernel(pos_ref, cos_ref, sin_ref):
    half = HEAD_DIM // 2
    lane = lax.broadcasted_iota(jnp.int32, (1, LANES), 1)
    expo = -(lane & (half - 1)).astype(F32) / half
    inv_freq = jnp.power(jnp.full((1, LANES), ROPE_THETA, F32), expo)
    sign = jnp.where((lane & (HEAD_DIM - 1)) < half, -1.0, 1.0)
    ang = pos_ref[0] * inv_freq
    cos_ref[0] = jnp.cos(ang)
    sin_ref[0] = jnp.sin(ang) * sign


def rope_tables(positions):
    b, s = positions.shape
    tm = _row_tile(s, 512)
    out = pl.BlockSpec((1, tm, LANES), lambda i, j: (i, j, 0))
    return pl.pallas_call(
        _rope_table_kernel,
        name="rope_tables",
        grid=(b, s // tm),
        in_specs=[pl.BlockSpec((1, tm, 1), lambda i, j: (i, j, 0))],
        out_specs=[out, out],
        out_shape=[jax.ShapeDtypeStruct((b, s, LANES), F32)] * 2,
        compiler_params=_cparams(("parallel", "parallel")),
    )(positions.astype(F32).reshape(b, s, 1))


def _swa_kernel(q_ref, kv_ref, cosq_ref, sinq_ref, cosk_ref, sink_ref, sinks_ref, o_ref):
    tq = q_ref.shape[1]
    W = SWA_WINDOW
    span = W + tq
    R = SWA_HEADS // SWA_KV_HEADS
    kw = SWA_KV_HEADS * HEAD_DIM
    t0 = pl.program_id(1) * tq
    start = pl.multiple_of(jnp.maximum(t0 - W, 0), SUBLANES)

    reps = q_ref.shape[2] // LANES
    qr = _rope(q_ref[0], jnp.concatenate([cosq_ref[0]] * reps, 1), jnp.concatenate([sinq_ref[0]] * reps, 1))
    qr = qr * _SCALE
    kvs = kv_ref[0, pl.ds(start, span), :]
    kr = _rope(kvs[:, 0:kw], cosk_ref[0, pl.ds(start, span), :], sink_ref[0, pl.ds(start, span), :])
    v = kvs[:, kw:2 * kw]

    rel = _row_pos(t0, tq, 1) - (start + lax.broadcasted_iota(jnp.int32, (1, span), 1))
    bias = jnp.where((rel >= 0) & (rel < W), 0.0, NEG)
    for g in range(SWA_KV_HEADS):
        qg = _stack_heads(qr, g, R).astype(BF16)
        s = _dot_nt(qg, kr[:, g * HEAD_DIM:(g + 1) * HEAD_DIM].astype(BF16)).reshape(R, tq, span) + bias
        snk = jnp.concatenate([jnp.full((1, tq, 1), sinks_ref[g * R + r], F32) for r in range(R)], 0)
        m = jnp.maximum(jnp.max(s, axis=-1, keepdims=True), snk)
        e = jnp.exp(s - m)
        den = jnp.sum(e, axis=-1, keepdims=True) + jnp.exp(snk - m)
        o = _dot(e.reshape(R * tq, span).astype(BF16), v[:, g * HEAD_DIM:(g + 1) * HEAD_DIM].astype(BF16))
        o = o.reshape(R, tq, HEAD_DIM) / den
        for r in range(R):
            h = g * R + r
            o_ref[0, :, h * HEAD_DIM:(h + 1) * HEAD_DIM] = o[r]


def swa_mixer(u, rope_cos, rope_sin, sinks):
    b, s, _ = u.shape
    tq = SWA_TQ
    qw = SWA_HEADS * HEAD_DIM
    kvw = 2 * SWA_KV_HEADS * HEAD_DIM
    tile = pl.BlockSpec((1, tq, LANES), lambda i, j: (i, j, 0))
    full = pl.BlockSpec((1, s, LANES), lambda i, j: (i, 0, 0))
    return pl.pallas_call(
        _swa_kernel,
        name="swa",
        grid=(b, s // tq),
        in_specs=[
            pl.BlockSpec((1, tq, qw), lambda i, j: (i, j, 0)),
            pl.BlockSpec((1, s, kvw), lambda i, j: (i, 0, qw // kvw)),
            tile, tile, full, full,
            pl.BlockSpec(memory_space=pltpu.SMEM),
        ],
        out_specs=pl.BlockSpec((1, tq, qw), lambda i, j: (i, j, 0)),
        out_shape=jax.ShapeDtypeStruct((b, s, qw), F32),
        compiler_params=_cparams(("parallel", "arbitrary")),
    )(u, u, rope_cos, rope_sin, rope_cos, rope_sin, sinks.astype(F32))


NSA_U_COLS = NSA_HEADS * HEAD_DIM + 6 * NSA_KV_HEADS * HEAD_DIM + LANES
_NSA_KVW = NSA_KV_HEADS * HEAD_DIM
_NSA_QBLK = NSA_HEADS * HEAD_DIM // _NSA_KVW
_NSA_R = NSA_HEADS // NSA_KV_HEADS
_CMP_HALF = NSA_CMP_BLOCK // 2
NSA_KEY_CHUNK = 512
NSA_SELECT_TQ = 256


def _nsa_compress_kernel(k_ref, v_ref, pe_ref, w1_ref, w2_ref, kc_ref, vc_ref):
    nc = kc_ref.shape[1]
    for idx, (x_ref, o_ref) in enumerate(((k_ref, kc_ref), (v_ref, vc_ref))):
        ha = jnp.zeros((nc, _NSA_KVW), F32)
        hb = jnp.zeros((nc, _NSA_KVW), F32)
        for l in range(_CMP_HALF):
            y = x_ref[0, pl.ds(l, nc, stride=NSA_CMP_STRIDE), :]
            ha = ha + _dot((y + pe_ref[idx, l:l + 1, :]).astype(BF16), w1_ref[idx, l])
            hb = hb + _dot((y + pe_ref[idx, _CMP_HALF + l:_CMP_HALF + l + 1, :]).astype(BF16),
                           w1_ref[idx, _CMP_HALF + l])
        hid = _silu(ha + pltpu.roll(hb, nc - 1, 0))
        o_ref[0] = _dot(hid.astype(BF16), w2_ref[idx])


def _nsa_select_kernel(q_ref, kc_ref, vc_ref, ov_ref, ocmp_ref, sel_ref):
    tq = q_ref.shape[1]
    nc = kc_ref.shape[1]
    nsel = ov_ref.shape[0]
    k_eff = min(NSA_TOPK, nsel)
    t0 = pl.program_id(1) * tq
    t_row = _row_pos(t0, tq, _NSA_R)
    cmp_end = lax.broadcasted_iota(jnp.int32, (1, nc), 1) * NSA_CMP_STRIDE + (NSA_CMP_BLOCK - 1)
    cvalid = cmp_end <= t_row
    jidx = lax.broadcasted_iota(jnp.int32, (nsel, 1), 0)
    blk_t = (t0 + lax.broadcasted_iota(jnp.int32, (1, tq), 1)) // NSA_SEL_BLOCK
    jvalid = jidx <= blk_t
    forced = (jidx == 0) | (jidx == blk_t)
    q = q_ref[0]
    sel_rows = []
    for g in range(NSA_KV_HEADS):
        qg = _stack_heads(q, g, _NSA_R).astype(BF16)
        kc = kc_ref[0, :, g * HEAD_DIM:(g + 1) * HEAD_DIM].astype(BF16)
        vc = vc_ref[0, :, g * HEAD_DIM:(g + 1) * HEAD_DIM].astype(BF16)
        s = _dot_nt(qg, kc) * _SCALE
        m = jnp.max(jnp.where(cvalid, s, NEG), axis=-1, keepdims=True)
        e = jnp.exp(jnp.where(cvalid, s - m, NEG))
        den = jnp.sum(e, axis=-1, keepdims=True)
        p = e / jnp.where(den > 0.0, den, 1.0)
        o = _dot(p.astype(BF16), vc)
        psum = p[0:tq]
        for r in range(_NSA_R):
            h = g * _NSA_R + r
            ocmp_ref[0, :, h * HEAD_DIM:(h + 1) * HEAD_DIM] = o[r * tq:(r + 1) * tq]
            if r:
                psum = psum + p[r * tq:(r + 1) * tq]
        imp_t = lax.dot_general(ov_ref[...], psum, (((1,), (1,)), ((), ())),
                                precision=HIGHEST, preferred_element_type=F32)
        score = jnp.where(forced, BIG, jnp.where(jvalid, imp_t, NEG))
        cnt = jnp.zeros((nsel, tq), F32)
        for i in range(nsel):
            si = score[i:i + 1, :]
            beats = (si > score) | ((si == score) & (jidx > i))
            cnt = cnt + jnp.where(beats, 1.0, 0.0)
        sel_rows.append(jnp.where(cnt < k_eff, 1.0, 0.0))
    pad = LANES - NSA_KV_HEADS * nsel
    sel_t = jnp.concatenate(sel_rows + [jnp.zeros((pad, tq), F32)], axis=0)
    sel_ref[0] = sel_t.T


def _nsa_attend_kernel(q_ref, sel_ref, gate_ref, ocmp_ref, ks_ref, vs_ref, kw_ref, vw_ref, ex_ref, o_ref):
    tq = q_ref.shape[1]
    s_len = ks_ref.shape[1]
    W = NSA_WINDOW
    span = min(W + tq, s_len)
    t0 = pl.program_id(1) * tq
    start = pl.multiple_of(jnp.maximum(t0 - W, 0), SUBLANES)
    t_q = _row_pos(t0, tq, 1)
    rel = t_q - (start + lax.broadcasted_iota(jnp.int32, (1, span), 1))
    win_bias = jnp.where((rel >= 0) & (rel < W), 0.0, NEG)
    q = q_ref[0] * _SCALE
    sig = _sigmoid(gate_ref[0])
    sel16 = sel_ref[0].astype(BF16)

    def attend(qg, k, v, bias):
        n = k.shape[0]
        s = _dot_nt(qg, k.astype(BF16)).reshape(_NSA_R, tq, n) + bias
        e = jnp.exp(s - jnp.max(s, axis=-1, keepdims=True))
        den = jnp.sum(e, axis=-1, keepdims=True)
        o = _dot(e.reshape(_NSA_R * tq, n).astype(BF16), v.astype(BF16))
        return o.reshape(_NSA_R, tq, HEAD_DIM) / den

    chunk = min(NSA_KEY_CHUNK, s_len)
    for c in range(s_len // chunk):
        klen = (c + 1) * chunk

        @pl.when(t0 // chunk == c)
        def _():
            kidx = lax.broadcasted_iota(jnp.int32, (1, klen), 1)
            for g in range(NSA_KV_HEADS):
                cols = slice(g * HEAD_DIM, (g + 1) * HEAD_DIM)
                qg = _stack_heads(q, g, _NSA_R).astype(BF16)
                chosen = _dot(sel16, ex_ref[g, :, 0:klen]) > 0.5
                sel_bias = jnp.where(chosen & (kidx <= t_q), 0.0, NEG)
                o_sel = attend(qg, ks_ref[0, 0:klen, cols], vs_ref[0, 0:klen, cols], sel_bias)
                o_win = attend(qg, kw_ref[0, pl.ds(start, span), cols], vw_ref[0, pl.ds(start, span), cols],
                               win_bias)
                for r in range(_NSA_R):
                    h = g * _NSA_R + r
                    hc = slice(h * HEAD_DIM, (h + 1) * HEAD_DIM)
                    o_ref[0, :, hc] = (sig[:, h:h + 1] * ocmp_ref[0, :, hc]
                                       + sig[:, NSA_HEADS + h:NSA_HEADS + h + 1] * o_sel[r]
                                       + sig[:, 2 * NSA_HEADS + h:2 * NSA_HEADS + h + 1] * o_win[r])


def _block_diag2(w):
    z = jnp.zeros_like(w)
    return jnp.concatenate([jnp.concatenate([w, z], -1), jnp.concatenate([z, w], -1)], -2)


def nsa_mixer(u, cmp_pe, cmp_w1, cmp_w2):
    b, s, _ = u.shape
    tq = ATTN_TQ
    nc = s // NSA_CMP_STRIDE
    nsel = s // NSA_SEL_BLOCK
    qw = NSA_HEADS * HEAD_DIM
    pe2 = jnp.concatenate([cmp_pe, cmp_pe], -1)
    w1bd = _block_diag2(cmp_w1).astype(BF16)
    w2bd = _block_diag2(cmp_w2).astype(BF16)
    const = lambda shape: pl.BlockSpec(shape, lambda *_: (0,) * len(shape))
    col = lambda c: pl.BlockSpec((1, s, _NSA_KVW), lambda i, *_: (i, 0, _NSA_QBLK + c))

    kc, vc = pl.pallas_call(
        _nsa_compress_kernel,
        name="nsa_compress",
        grid=(b,),
        in_specs=[col(0), col(1), const(pe2.shape), const(w1bd.shape), const(w2bd.shape)],
        out_specs=[pl.BlockSpec((1, nc, _NSA_KVW), lambda i: (i, 0, 0))] * 2,
        out_shape=[jax.ShapeDtypeStruct((b, nc, _NSA_KVW), F32)] * 2,
        compiler_params=_cparams(("parallel",)),
    )(u, u, pe2, w1bd, w2bd)

    c_start = np.arange(nc) * NSA_CMP_STRIDE
    s_start = np.arange(nsel) * NSA_SEL_BLOCK
    ov_t = ((c_start[None, :] < s_start[:, None] + NSA_SEL_BLOCK)
            & (c_start[None, :] + NSA_CMP_BLOCK > s_start[:, None])
            & (np.arange(nc)[None, :] < nc - 1)).astype(np.float32)
    qspec = pl.BlockSpec((1, tq, qw), lambda i, j: (i, j, 0))
    ts = _row_tile(s, NSA_SELECT_TQ)
    sel_q = pl.BlockSpec((1, ts, qw), lambda i, j: (i, j, 0))
    o_cmp, sel = pl.pallas_call(
        _nsa_select_kernel,
        name="nsa_select",
        grid=(b, s // ts),
        in_specs=[sel_q,
                  pl.BlockSpec((1, nc, _NSA_KVW), lambda i, j: (i, 0, 0)),
                  pl.BlockSpec((1, nc, _NSA_KVW), lambda i, j: (i, 0, 0)),
                  const(ov_t.shape)],
        out_specs=[sel_q, pl.BlockSpec((1, ts, LANES), lambda i, j: (i, j, 0))],
        out_shape=[jax.ShapeDtypeStruct((b, s, qw), F32), jax.ShapeDtypeStruct((b, s, LANES), F32)],
        compiler_params=_cparams(("parallel", "arbitrary")),
    )(u, kc, vc, jnp.asarray(ov_t))

    expand = np.zeros((NSA_KV_HEADS, LANES, s), np.float32)
    for g in range(NSA_KV_HEADS):
        expand[g, g * nsel + np.arange(s) // NSA_SEL_BLOCK, np.arange(s)] = 1.0
    gate_blk = NSA_U_COLS // LANES - 1
    return pl.pallas_call(
        _nsa_attend_kernel,
        name="nsa_attend",
        grid=(b, s // tq),
        in_specs=[qspec,
                  pl.BlockSpec((1, tq, LANES), lambda i, j: (i, j, 0)),
                  pl.BlockSpec((1, tq, LANES), lambda i, j: (i, j, gate_blk)),
                  qspec,
                  col(2), col(3), col(4), col(5),
                  const(expand.shape)],
        out_specs=qspec,
        out_shape=jax.ShapeDtypeStruct((b, s, qw), F32),
        compiler_params=_cparams(("parallel", "arbitrary")),
    )(u, sel, u, o_cmp, u, u, u, u, jnp.asarray(expand, BF16))


def _seg_sum(x, ones16):
    hi = x.astype(BF16)
    lo = (x - hi.astype(F32)).astype(BF16)
    return _dot(hi, ones16) + _dot(lo, ones16)


def _head_ones(width):
    idx = np.arange(width) // HEAD_DIM
    return jnp.asarray((idx[:, None] == idx[None, :]).astype(np.float32), BF16)


_RW_R, _RW_K, _RW_V = 0, RWKV_DIM, 2 * RWKV_DIM
_RW_WD = 3 * RWKV_DIM
_RW_AD = _RW_WD + RWKV_W_LORA
_RW_GD = _RW_AD + RWKV_A_LORA


def _rwkv_prep_kernel(u_ref, mu_ref, w0_ref, wup_ref, a0_ref, aup_ref, gup_ref, kk_ref, ka_ref, rk_ref,
                      ones_ref, r_o, w_o, k_o, v_o, kk_o, q_o, g_o, bonus_o, up_ref):
    tm = u_ref.shape[1]

    @pl.when(pl.program_id(1) == 0)
    def _():
        up_ref[0:SUBLANES, :] = jnp.zeros((SUBLANES, RWKV_IN), F32)

    u = u_ref[0]
    up_ref[SUBLANES:SUBLANES + tm, :] = u
    prev = up_ref[pl.ds(SUBLANES - 1, tm), :]
    up_ref[0:SUBLANES, :] = u[tm - SUBLANES:tm, :]
    x = u + (prev - u) * mu_ref[...]
    r = x[:, _RW_R:_RW_R + RWKV_DIM]
    k = x[:, _RW_K:_RW_K + RWKV_DIM]
    v = x[:, _RW_V:_RW_V + RWKV_DIM]
    wd = x[:, _RW_WD:_RW_AD]
    ad = x[:, _RW_AD:_RW_GD]
    gd = x[:, _RW_GD:RWKV_IN]
    w = -_softplus(-(w0_ref[...] + _dot(jnp.tanh(wd).astype(BF16), wup_ref[...]))) - 0.5
    a = _sigmoid(a0_ref[...] + _dot(ad.astype(BF16), aup_ref[...]))
    ones16 = ones_ref[...]
    kk = k * kk_ref[...]
    kk = kk / jnp.maximum(jnp.sqrt(_seg_sum(kk * kk, ones16)), 1e-12)
    k2 = k * (1.0 + (a - 1.0) * ka_ref[...])
    r_o[0] = r
    w_o[0] = jnp.exp(-jnp.exp(w))
    k_o[0] = k2
    v_o[0] = v
    kk_o[0] = kk
    q_o[0] = kk * a
    g_o[0] = _dot(_sigmoid(gd).astype(BF16), gup_ref[...])
    bonus_o[0] = _seg_sum(r * k2 * rk_ref[...], ones16) * v


_RWS_TB = 64
_RWS_VB = HEAD_DIM // 2 // SUBLANES
_RWS_NACC = 2
_RWS_VECS = 5


def _rwkv_scan_vhalf_kernel(kk_ref, w_ref, q_ref, k_ref, r_ref, v_ref, o_ref, s_ref, rows_ref):
    @pl.when(pl.program_id(0) == 0)
    def _():
        s_ref[...] = jnp.zeros_like(s_ref)

    low = lax.broadcasted_iota(jnp.int32, (SUBLANES, LANES), 1) < LANES // 2

    def reduce_acc(acc, vb):
        tot = acc[(vb, 0)]
        for i in range(1, _RWS_NACC):
            tot = tot + acc[(vb, i)]
        return tot

    def group(tg, carry):
        t8 = pl.multiple_of(tg * SUBLANES, SUBLANES)
        for a, ref in enumerate((kk_ref, w_ref, q_ref, k_ref, r_ref)):
            for i in range(HEAD_DIM // 2):
                x = ref[i, pl.ds(t8, SUBLANES), :]
                xr = pltpu.roll(x, LANES // 2, 1)
                rows_ref[a, 0, i] = jnp.where(low, x, xr)
                rows_ref[a, 1, i] = jnp.where(low, xr, x)
        for tl in range(SUBLANES):
            t = t8 + tl
            key_row = lambda a, k: rows_ref[a, k % 2, k // 2, tl:tl + 1, :]
            acc = {}
            for k in range(HEAD_DIM):
                kk_row = key_row(0, k)
                for vb in range(_RWS_VB):
                    term = s_ref[vb, k] * kk_row
                    key = (vb, k % _RWS_NACC)
                    acc[key] = acc[key] + term if key in acc else term
            sa = [reduce_acc(acc, vb) for vb in range(_RWS_VB)]
            vt = [v_ref[t, vb * SUBLANES:(vb + 1) * SUBLANES, :] for vb in range(_RWS_VB)]
            acc = {}
            for k in range(HEAD_DIM):
                w_row, q_row, k_row, r_row = (key_row(a, k) for a in range(1, _RWS_VECS))
                for vb in range(_RWS_VB):
                    st = s_ref[vb, k] * w_row - sa[vb] * q_row + vt[vb] * k_row
                    s_ref[vb, k] = st
                    term = st * r_row
                    key = (vb, k % _RWS_NACC)
                    acc[key] = acc[key] + term if key in acc else term
            for vb in range(_RWS_VB):
                o_ref[t, vb * SUBLANES:(vb + 1) * SUBLANES, :] = reduce_acc(acc, vb)
        return carry

    lax.fori_loop(0, o_ref.shape[0] // SUBLANES, group, 0)


_RL_TB = 128
_RL_ROWS = HEAD_DIM // 2


def _scan_row_base(i, half, interleaved):
    return 2 * i + half if interleaved else half * _RL_ROWS + i


def _to_scan_kernel(*refs, interleaved):
    n = len(interleaved)
    ins, outs, y_ref = refs[:n], refs[n:2 * n], refs[2 * n]
    nb, tb = ins[0].shape[0], ins[0].shape[1]
    for a in range(n):
        for b in range(nb):
            y_ref[b] = ins[a][b].T
        for i in range(_RL_ROWS):
            parts = [y_ref[b, pl.ds(_scan_row_base(i, half, interleaved[a]), RWKV_HEADS, stride=HEAD_DIM), :]
                     for half in range(2) for b in range(nb)]
            rows = jnp.concatenate(parts, axis=0).T
            if interleaved[a]:
                outs[a][i] = rows
            else:
                outs[a][pl.ds(i, tb, stride=_RL_ROWS), :] = rows


def _from_scan_kernel(o_ref, bonus_ref, g_ref, lnw_ref, lnb_ref, out_ref, y_ref):
    nb, tb = out_ref.shape[0], out_ref.shape[1]
    for i in range(_RL_ROWS):
        zt = o_ref[pl.ds(i, tb, stride=_RL_ROWS), :].T
        for half in range(2):
            for b in range(nb):
                r0 = (half * nb + b) * RWKV_HEADS
                y_ref[b, pl.ds(_scan_row_base(i, half, False), RWKV_HEADS, stride=HEAD_DIM), :] = zt[r0:r0 + RWKV_HEADS]
    for b in range(nb):
        for h in range(RWKV_HEADS):
            rows = slice(h * HEAD_DIM, (h + 1) * HEAD_DIM)
            blk = y_ref[b, rows, :]
            cen = blk - jnp.mean(blk, axis=0, keepdims=True)
            var = jnp.mean(cen * cen, axis=0, keepdims=True)
            y_ref[b, rows, :] = cen * lax.rsqrt(var + RWKV_GN_EPS)
        o = y_ref[b].T * lnw_ref[...] + lnb_ref[...]
        out_ref[b] = (o + bonus_ref[b]) * g_ref[b]


def to_scan_layout(arrays, interleaved):
    b, s, dm = arrays[0].shape
    tb = _row_tile(s, _RL_TB)
    n = len(arrays)
    key_spec = pl.BlockSpec((_RL_ROWS, tb, LANES), lambda i: (0, i, 0))
    val_spec = pl.BlockSpec((tb * _RL_ROWS, LANES), lambda i: (i, 0))
    key_shape = jax.ShapeDtypeStruct((_RL_ROWS, s, LANES), F32)
    val_shape = jax.ShapeDtypeStruct((s * _RL_ROWS, LANES), F32)
    outs = pl.pallas_call(
        functools.partial(_to_scan_kernel, interleaved=tuple(interleaved)),
        name="rwkv_to_scan",
        grid=(s // tb,),
        in_specs=[pl.BlockSpec((b, tb, dm), lambda i: (0, i, 0))] * n,
        out_specs=[key_spec if f else val_spec for f in interleaved],
        out_shape=[key_shape if f else val_shape for f in interleaved],
        scratch_shapes=[pltpu.VMEM((b, dm, tb), F32)],
        compiler_params=_cparams(("parallel",)),
    )(*arrays)
    return [o if f else o.reshape(s, _RL_ROWS, LANES) for o, f in zip(outs, interleaved)]


def from_scan_layout(o, bonus, g, ln_w, ln_b):
    s = o.shape[0]
    b, _, dm = bonus.shape
    tb = _row_tile(s, _RL_TB)
    tile = pl.BlockSpec((b, tb, dm), lambda i: (0, i, 0))
    vec = pl.BlockSpec((1, dm), lambda i: (0, 0))
    return pl.pallas_call(
        _from_scan_kernel,
        name="rwkv_from_scan",
        grid=(s // tb,),
        in_specs=[pl.BlockSpec((tb * _RL_ROWS, LANES), lambda i: (i, 0)), tile, tile, vec, vec],
        out_specs=tile,
        out_shape=jax.ShapeDtypeStruct((b, s, dm), F32),
        scratch_shapes=[pltpu.VMEM((b, dm, tb), F32)],
        compiler_params=_cparams(("parallel",)),
    )(o.reshape(s * _RL_ROWS, LANES), bonus, g, ln_w.astype(F32).reshape(1, dm), ln_b.astype(F32).reshape(1, dm))


def rwkv7_mixer(u, mu, w0, w_up, a0, a_up, g_up, k_k, k_a, r_k, ln_w, ln_b):
    b, s, _ = u.shape
    dm = RWKV_DIM
    tm = _row_tile(s, 256)
    row = lambda v: v.astype(F32).reshape(1, -1)
    const = lambda shape: pl.BlockSpec(shape, lambda *_: (0,) * len(shape))
    ones_d = _head_ones(dm)
    tile = pl.BlockSpec((1, tm, dm), lambda i, j: (i, j, 0))
    outs = pl.pallas_call(
        _rwkv_prep_kernel,
        name="rwkv_prep",
        grid=(b, s // tm),
        in_specs=[pl.BlockSpec((1, tm, RWKV_IN), lambda i, j: (i, j, 0)),
                  const((1, RWKV_IN)), const((1, dm)), const((RWKV_W_LORA, dm)), const((1, dm)),
                  const((RWKV_A_LORA, dm)), const((RWKV_G_LORA, dm)), const((1, dm)), const((1, dm)),
                  const((1, dm)), const((dm, dm))],
        out_specs=[tile] * 8,
        out_shape=[jax.ShapeDtypeStruct((b, s, dm), F32)] * 8,
        scratch_shapes=[pltpu.VMEM((SUBLANES + tm, RWKV_IN), F32)],
        compiler_params=_cparams(("parallel", "arbitrary")),
    )(u, row(mu), row(w0), w_up.astype(BF16), row(a0), a_up.astype(BF16), g_up.astype(BF16),
      row(k_k), row(k_a), row(r_k), ones_d)
    r, wdec, k2, v, kk, q, g, bonus = outs

    nchain = b * RWKV_HEADS
    assert 2 * nchain == LANES and s % 2 == 0, "the scan kernel maps (half, batch, head) onto the 128 lanes"
    scan_in = to_scan_layout([kk, wdec, q, k2, r, v], interleaved=[True] * _RWS_VECS + [False])
    tb = _row_tile(s, _RWS_TB)
    kblk = pl.BlockSpec((_RL_ROWS, tb, LANES), lambda i: (0, i, 0))
    vblk = pl.BlockSpec((tb, _RL_ROWS, LANES), lambda i: (i, 0, 0))
    o = pl.pallas_call(
        _rwkv_scan_vhalf_kernel,
        name="rwkv_scan",
        grid=(s // tb,),
        in_specs=[kblk] * _RWS_VECS + [vblk],
        out_specs=vblk,
        out_shape=jax.ShapeDtypeStruct((s, _RL_ROWS, LANES), F32),
        scratch_shapes=[pltpu.VMEM((_RWS_VB, HEAD_DIM, SUBLANES, LANES), F32),
                        pltpu.VMEM((_RWS_VECS, 2, _RL_ROWS, SUBLANES, LANES), F32)],
        compiler_params=_cparams(("arbitrary",)),
    )(*scan_in)
    return from_scan_layout(o, bonus, g, ln_w, ln_b)


MOE_SLOT_BLOCK = 512
MOE_FF_CHUNK = 256
MOE_ROW_TILE = 256
MOE_DMA_UNROLL = 8
_META_E0, _META_E1, _META_R0, _META_R1, _META_W0, _META_W1 = range(6)


def _moe_route_kernel(x_ref, g_ref, wr_ref, tril_ref, h_o, meta_o, cnt_o, carry_ref):
    @pl.when(pl.program_id(0) == 0)
    def _():
        carry_ref[...] = jnp.zeros_like(carry_ref)

    x = x_ref[...]
    ms = jnp.mean(x * x, axis=-1, keepdims=True)
    h = x * lax.rsqrt(ms + NORM_EPS) * g_ref[...]
    h_o[...] = h
    logits = jnp.dot(h, wr_ref[...], precision=HIGHEST, preferred_element_type=F32)
    lane = lax.broadcasted_iota(jnp.int32, logits.shape, 1)
    l1 = jnp.where(lane < N_EXPERTS, logits, NEG)
    m1 = jnp.max(l1, axis=-1, keepdims=True)
    i1 = jnp.min(jnp.where(l1 == m1, lane, LANES), axis=-1, keepdims=True)
    l2 = jnp.where(lane == i1, NEG, l1)
    m2 = jnp.max(l2, axis=-1, keepdims=True)
    i2 = jnp.min(jnp.where(l2 == m2, lane, LANES), axis=-1, keepdims=True)
    e21 = jnp.exp(m2 - m1)
    w1 = 1.0 / (1.0 + e21)
    w2 = e21 * w1
    cnt = jnp.where((lane == i1) | (lane == i2), 1.0, 0.0)
    before = _dot(tril_ref[...], cnt.astype(BF16)) + carry_ref[...]
    r1 = jnp.sum(jnp.where(lane == i1, before, 0.0), axis=-1, keepdims=True)
    r2 = jnp.sum(jnp.where(lane == i2, before, 0.0), axis=-1, keepdims=True)
    carry_ref[...] += jnp.sum(cnt, axis=0, keepdims=True)
    cnt_o[...] = carry_ref[...]
    meta = jnp.zeros(logits.shape, F32)
    for idx, val in ((_META_E0, i1.astype(F32)), (_META_E1, i2.astype(F32)), (_META_R0, r1),
                     (_META_R1, r2), (_META_W0, w1), (_META_W1, w2)):
        meta = jnp.where(lane == idx, val, meta)
    meta_o[...] = meta


def _row_copy(src_ref, src_row, dst_ref, dst_row, sem):
    return pltpu.make_async_copy(src_ref.at[pl.ds(src_row, 1), :], dst_ref.at[pl.ds(dst_row, 1), :], sem)


def _moe_scatter_kernel(d0_ref, d1_ref, h_ref, xs_in_ref, xs_ref, sem):
    del xs_in_ref
    tm = h_ref.shape[0]

    def issue(r, c):
        _row_copy(h_ref, r, xs_ref, d0_ref[r], sem).start(priority=0)
        _row_copy(h_ref, r, xs_ref, d1_ref[r], sem).start(priority=1)
        return c

    def drain(r, c):
        _row_copy(h_ref, r, xs_ref, d0_ref[r], sem).wait()
        _row_copy(h_ref, r, xs_ref, d1_ref[r], sem).wait()
        return c

    lax.fori_loop(0, tm, issue, 0, unroll=MOE_DMA_UNROLL)
    lax.fori_loop(0, tm, drain, 0, unroll=MOE_DMA_UNROLL)


def _moe_expert_kernel(be_ref, nused_ref, xs_ref, wg_ref, wu_ref, wd_ref, ys_ref):
    del be_ref
    i = pl.program_id(0)

    @pl.when(i < nused_ref[0])
    def _():
        x16 = xs_ref[...].astype(BF16)
        acc = jnp.zeros(ys_ref.shape, F32)
        for c in range(D_FF // MOE_FF_CHUNK):
            cols = slice(c * MOE_FF_CHUNK, (c + 1) * MOE_FF_CHUNK)
            a = _silu(_dot(x16, wg_ref[0, :, cols])) * _dot(x16, wu_ref[0, :, cols])
            acc = acc + _dot(a.astype(BF16), wd_ref[0, cols, :])
        ys_ref[...] = acc

    @pl.when(i >= nused_ref[0])
    def _():
        ys_ref[...] = jnp.zeros_like(ys_ref)


def _moe_combine_kernel(d0_ref, d1_ref, x_ref, meta_ref, ys_ref, p_ref, pg_ref, pp_ref, *rest):
    nf_ref = rest[0] if len(rest) == 5 else None
    o_ref, b0_ref, b1_ref, sem = rest[-4:]
    tm = x_ref.shape[0]

    def issue(r, c):
        _row_copy(ys_ref, d0_ref[r], b0_ref, r, sem).start(priority=0)
        _row_copy(ys_ref, d1_ref[r], b1_ref, r, sem).start(priority=1)
        return c

    def drain(r, c):
        _row_copy(ys_ref, d0_ref[r], b0_ref, r, sem).wait()
        _row_copy(ys_ref, d1_ref[r], b1_ref, r, sem).wait()
        return c

    lax.fori_loop(0, tm, issue, 0, unroll=MOE_DMA_UNROLL)
    lax.fori_loop(0, tm, drain, 0, unroll=MOE_DMA_UNROLL)
    w0 = meta_ref[:, _META_W0:_META_W0 + 1]
    w1 = meta_ref[:, _META_W1:_META_W1 + 1]
    y = x_ref[...] + (w0 * b0_ref[...] + w1 * b1_ref[...])
    o_ref[...] = _ple_epilogue(y, p_ref, pg_ref, pp_ref, nf_ref)


def moe_ffn(x, g, router, w_gate, w_up, w_down, p, ple_gate, ple_proj, norm_final=None):
    t, d = x.shape
    tm = _row_tile(t, MOE_ROW_TILE)
    blk = MOE_SLOT_BLOCK
    nblk = -(-(t * TOP_K + N_EXPERTS * (blk - 1)) // blk)
    slots = nblk * blk
    tril = jnp.asarray(np.tril(np.ones((tm, tm), np.float32), -1), BF16)
    const = lambda shape: pl.BlockSpec(shape, lambda *_: (0,) * len(shape))
    row = lambda w: pl.BlockSpec((tm, w), lambda i: (i, 0))
    h, meta, counts = pl.pallas_call(
        _moe_route_kernel,
        name="moe_route",
        grid=(t // tm,),
        in_specs=[row(d), const((1, d)), const((d, LANES)), const((tm, tm))],
        out_specs=[row(d), row(LANES), const((1, LANES))],
        out_shape=[jax.ShapeDtypeStruct((t, d), F32), jax.ShapeDtypeStruct((t, LANES), F32),
                   jax.ShapeDtypeStruct((1, LANES), F32)],
        scratch_shapes=[pltpu.VMEM((1, LANES), F32)],
        compiler_params=_cparams(("arbitrary",)),
    )(x, g.reshape(1, d), _pad_cols(router.astype(F32), LANES), tril)

    cnt = counts[0, :N_EXPERTS].astype(jnp.int32)
    pcnt = (cnt + blk - 1) // blk * blk
    pend = jnp.cumsum(pcnt)
    pstart = pend - pcnt
    e0 = meta[:, _META_E0].astype(jnp.int32)
    e1 = meta[:, _META_E1].astype(jnp.int32)
    d0 = pstart[e0] + meta[:, _META_R0].astype(jnp.int32)
    d1 = pstart[e1] + meta[:, _META_R1].astype(jnp.int32)
    blk_start = jnp.arange(nblk, dtype=jnp.int32) * blk
    blk_e = jnp.minimum(jnp.sum((pend[None, :] <= blk_start[:, None]).astype(jnp.int32), axis=1),
                        N_EXPERTS - 1).astype(jnp.int32)
    nused = (pend[-1:] // blk).astype(jnp.int32)

    smem_rows = pl.BlockSpec((tm,), lambda i: (i,), memory_space=pltpu.SMEM)
    xs = pl.pallas_call(
        _moe_scatter_kernel,
        name="moe_scatter",
        grid=(t // tm,),
        in_specs=[smem_rows, smem_rows, row(d), pl.BlockSpec(memory_space=pl.ANY)],
        out_specs=pl.BlockSpec(memory_space=pl.ANY),
        out_shape=jax.ShapeDtypeStruct((slots, d), F32),
        scratch_shapes=[pltpu.SemaphoreType.DMA(())],
        input_output_aliases={3: 0},
        compiler_params=_cparams(("arbitrary",)),
    )(d0, d1, h, jnp.zeros((slots, d), F32))

    f = w_gate.shape[2]
    ys = pl.pallas_call(
        _moe_expert_kernel,
        name="moe_expert",
        grid_spec=pltpu.PrefetchScalarGridSpec(
            num_scalar_prefetch=2,
            grid=(nblk,),
            in_specs=[pl.BlockSpec((blk, d), lambda i, be, nu: (i, 0)),
                      pl.BlockSpec((1, d, f), lambda i, be, nu: (be[i], 0, 0)),
                      pl.BlockSpec((1, d, f), lambda i, be, nu: (be[i], 0, 0)),
                      pl.BlockSpec((1, f, d), lambda i, be, nu: (be[i], 0, 0))],
            out_specs=pl.BlockSpec((blk, d), lambda i, be, nu: (i, 0)),
        ),
        out_shape=jax.ShapeDtypeStruct((slots, d), F32),
        compiler_params=_cparams(("arbitrary",)),
    )(blk_e, nused, xs, w_gate, w_up, w_down)

    ple_specs, ple_args = _ple_operands(p, ple_gate, ple_proj, norm_final, tm, lambda i: (i, 0))
    return pl.pallas_call(
        _moe_combine_kernel,
        name="moe_combine",
        grid=(t // tm,),
        in_specs=[smem_rows, smem_rows, row(d), row(LANES), pl.BlockSpec(memory_space=pl.ANY)] + ple_specs,
        out_specs=row(d),
        out_shape=jax.ShapeDtypeStruct((t, d), F32),
        scratch_shapes=[pltpu.VMEM((tm, d), F32), pltpu.VMEM((tm, d), F32), pltpu.SemaphoreType.DMA(())],
        compiler_params=_cparams(("arbitrary",)),
    )(d0, d1, x, meta, ys, *ple_args)


_IN_SPLITS = (SSD_D_INNER, SSD_CONV_DIM, SSD_HEADS,
              NSA_HEADS * HEAD_DIM, 6 * NSA_KV_HEADS * HEAD_DIM, 3 * NSA_HEADS,
              RWKV_IN,
              SWA_HEADS * HEAD_DIM, 2 * SWA_KV_HEADS * HEAD_DIM,
              N_BRANCHES * D_MODEL)
_IN_OFF = tuple(int(o) for o in np.cumsum((0,) + _IN_SPLITS))


def _split_in_proj(w):
    o = _IN_OFF
    seg = lambda a, b: w[:, o[a]:o[b]]
    w_ssd = jnp.concatenate([seg(0, 2), _pad_cols(seg(2, 3), LANES)], axis=1)
    w_nsa = jnp.concatenate([seg(3, 5), _pad_cols(seg(5, 6), LANES)], axis=1)
    return tuple(m.astype(BF16) for m in (w_ssd, w_nsa, seg(6, 7), seg(7, 9), seg(9, 10)))


def kernel(x, p, positions, norm_mix, w_in, ssd_conv_w, ssd_conv_b, ssd_dt_bias, ssd_a_log, ssd_d, ssd_norm, nsa_cmp_pe, nsa_cmp_w1, nsa_cmp_w2, rwkv_mu, rwkv_w0, rwkv_w_up, rwkv_a0, rwkv_a_up, rwkv_g_up, rwkv_k_k, rwkv_k_a, rwkv_r_k, rwkv_ln_w, rwkv_ln_b, swa_sinks, w_br_ssd, w_br_nsa, w_br_rwkv, w_br_swa, w_out, norm_ffn, ffn_w_gate, ffn_w_up, ffn_w_down, moe_router, moe_w_gate, moe_w_up, moe_w_down, ple_proj, ple_gate, norm_final):
    b, s, d = x.shape
    t = b * s
    depth = w_in.shape[0]
    xf = x.reshape(t, d)
    rope_cos, rope_sin = rope_tables(positions)
    for i in range(depth):
        w_ssd, w_nsa, w_rwkv, w_swa, w_gates = _split_in_proj(w_in[i])
        g_mix = norm_mix[i]
        u_ssd = norm_matmul(xf, g_mix, w_ssd).reshape(b, s, -1)
        u_nsa = norm_matmul(xf, g_mix, w_nsa).reshape(b, s, -1)
        u_rwkv = norm_matmul(xf, g_mix, w_rwkv).reshape(b, s, -1)
        u_swa = norm_matmul(xf, g_mix, w_swa).reshape(b, s, -1)
        y_ssd = ssd_mixer(u_ssd, ssd_conv_w[i], ssd_conv_b[i], ssd_dt_bias[i], ssd_a_log[i], ssd_d[i], ssd_norm[i])
        y_nsa = nsa_mixer(u_nsa, nsa_cmp_pe[i], nsa_cmp_w1[i], nsa_cmp_w2[i])
        y_rwkv = rwkv7_mixer(u_rwkv, rwkv_mu[i], rwkv_w0[i], rwkv_w_up[i], rwkv_a0[i], rwkv_a_up[i],
                             rwkv_g_up[i], rwkv_k_k[i], rwkv_k_a[i], rwkv_r_k[i], rwkv_ln_w[i], rwkv_ln_b[i])
        y_swa = swa_mixer(u_swa, rope_cos, rope_sin, swa_sinks[i])
        p_stack = jnp.stack([w_br_ssd[i], w_br_nsa[i], w_br_rwkv[i], w_br_swa[i]]).astype(BF16)
        ys = [y.reshape(t, -1) for y in (y_ssd, y_nsa, y_rwkv, y_swa)]
        xf = merge_branches(xf, g_mix, w_gates, ys, p_stack, w_out[i].astype(BF16))
        j = i // 2
        ple_args = (p[i].reshape(t, -1), ple_gate[i].astype(BF16), ple_proj[i].astype(BF16),
                    norm_final if i == depth - 1 else None)
        if i % 2 == 0:
            xf = dense_ffn(xf, norm_ffn[i], ffn_w_gate[j].astype(BF16), ffn_w_up[j].astype(BF16),
                           ffn_w_down[j].astype(BF16), *ple_args)
        else:
            xf = moe_ffn(xf, norm_ffn[i], moe_router[j], moe_w_gate[j].astype(BF16),
                         moe_w_up[j].astype(BF16), moe_w_down[j].astype(BF16), *ple_args)
    return xf.reshape(b, s, d)
```

```python
import functools

import numpy as np
import jax
import jax.numpy as jnp
from jax import lax
from jax.experimental import pallas as pl
from jax.experimental.pallas import tpu as pltpu

F32 = jnp.float32
BF16 = jnp.bfloat16

D_MODEL = 1024
HEAD_DIM = 64
NORM_EPS = 1e-6
NEG = -1e30
BIG = 1e30

SSD_HEADS = 8
SSD_D_INNER = SSD_HEADS * HEAD_DIM
SSD_STATE = 128
SSD_GROUPS = 2
SSD_CONV = 4
SSD_CHUNK = 128
SSD_CONV_DIM = SSD_D_INNER + 2 * SSD_GROUPS * SSD_STATE

NSA_HEADS = 8
NSA_KV_HEADS = 2
NSA_CMP_BLOCK = 32
NSA_CMP_STRIDE = 16
NSA_CMP_HIDDEN = 64
NSA_SEL_BLOCK = 64
NSA_TOPK = 8
NSA_WINDOW = 512

RWKV_HEADS = 8
RWKV_DIM = RWKV_HEADS * HEAD_DIM
RWKV_W_LORA = 64
RWKV_A_LORA = 64
RWKV_G_LORA = 128
RWKV_IN = 3 * RWKV_DIM + RWKV_W_LORA + RWKV_A_LORA + RWKV_G_LORA
RWKV_GN_EPS = 64e-5

SWA_HEADS = 8
SWA_KV_HEADS = 2
SWA_WINDOW = 128
ROPE_THETA = 150000.0

N_BRANCHES = 4
D_FF = 2816
N_EXPERTS = 8
TOP_K = 2
PLE_DIM = 256

LANES = 128
SUBLANES = 8
VMEM_LIMIT = 56 * 1024 * 1024

HIGHEST = lax.Precision.HIGHEST


def _cparams(sem):
    return pltpu.CompilerParams(dimension_semantics=sem, vmem_limit_bytes=VMEM_LIMIT)


def _sigmoid(x):
    return 1.0 / (1.0 + jnp.exp(-x))


def _silu(x):
    return x * _sigmoid(x)


def _softplus(x):
    return jnp.maximum(x, 0.0) + jnp.log1p(jnp.exp(-jnp.abs(x)))


def _dot(a, b):
    return jnp.dot(a, b, preferred_element_type=F32)


def _dot_nt(a, b):
    return lax.dot_general(a, b, (((1,), (1,)), ((), ())), preferred_element_type=F32)


def _pad_cols(w, n):
    return jnp.pad(w, ((0, 0), (0, n - w.shape[1])))


def _row_tile(t, pref):
    while t % pref:
        pref //= 2
    return pref


def _rms_bf16(x, g):
    ms = jnp.mean(x * x, axis=-1, keepdims=True)
    return (x * lax.rsqrt(ms + NORM_EPS) * g).astype(BF16)


def _norm_matmul_kernel(x_ref, g_ref, w_ref, o_ref):
    o_ref[...] = _dot(_rms_bf16(x_ref[...], g_ref[...]), w_ref[...])


def norm_matmul(x, g, w):
    t, d = x.shape
    n = w.shape[1]
    tm = _row_tile(t, 1024)
    return pl.pallas_call(
        _norm_matmul_kernel,
        name="norm_matmul",
        grid=(t // tm,),
        in_specs=[
            pl.BlockSpec((tm, d), lambda i: (i, 0)),
            pl.BlockSpec((1, d), lambda i: (0, 0)),
            pl.BlockSpec((d, n), lambda i: (0, 0)),
        ],
        out_specs=pl.BlockSpec((tm, n), lambda i: (i, 0)),
        out_shape=jax.ShapeDtypeStruct((t, n), F32),
        compiler_params=_cparams(("parallel",)),
    )(x, g.reshape(1, d), w)


def _merge_kernel(x_ref, g_ref, wg_ref, y0_ref, y1_ref, y2_ref, y3_ref, p_ref, wo_ref, o_ref):
    x = x_ref[...]
    h = _rms_bf16(x, g_ref[...])
    acc = None
    for m, y_ref in enumerate((y0_ref, y1_ref, y2_ref, y3_ref)):
        pm = _dot(y_ref[...].astype(BF16), p_ref[m])
        gm = _sigmoid(_dot(h, wg_ref[:, m * D_MODEL:(m + 1) * D_MODEL]))
        acc = gm * pm if acc is None else acc + gm * pm
    o_ref[...] = x + _dot(acc.astype(BF16), wo_ref[...])


def merge_branches(x, g, w_gates, ys, p_stack, w_out):
    t, d = x.shape
    tm = _row_tile(t, 512)
    dm = ys[0].shape[1]
    row = lambda w: pl.BlockSpec((tm, w), lambda i: (i, 0))
    const = lambda shape: pl.BlockSpec(shape, lambda i: (0,) * len(shape))
    return pl.pallas_call(
        _merge_kernel,
        name="merge",
        grid=(t // tm,),
        in_specs=[row(d), const((1, d)), const((d, N_BRANCHES * d)), row(dm), row(dm), row(dm), row(dm),
                  const((N_BRANCHES, dm, d)), const((d, d))],
        out_specs=row(d),
        out_shape=jax.ShapeDtypeStruct((t, d), F32),
        compiler_params=_cparams(("parallel",)),
    )(x, g.reshape(1, d), w_gates, *ys, p_stack, w_out)


def _ple_epilogue(y, p_ref, pg_ref, pp_ref, nf_ref):
    gate = _sigmoid(_dot(y.astype(BF16), pg_ref[...]))
    y = y + gate * _dot(p_ref[...].astype(BF16), pp_ref[...])
    if nf_ref is not None:
        ms = jnp.mean(y * y, axis=-1, keepdims=True)
        y = y * lax.rsqrt(ms + NORM_EPS) * nf_ref[...]
    return y


def _ple_operands(p, ple_gate, ple_proj, norm_final, tm, index_map):
    d = ple_gate.shape[0]
    const = lambda shape: pl.BlockSpec(shape, lambda *_: (0,) * len(shape))
    specs = [pl.BlockSpec((tm, p.shape[1]), index_map), const((d, d)), const((p.shape[1], d))]
    args = [p, ple_gate, ple_proj]
    if norm_final is not None:
        specs.append(const((1, d)))
        args.append(norm_final.reshape(1, d))
    return specs, args


def _ffn_kernel(x_ref, g_ref, wg_ref, wu_ref, wd_ref, p_ref, pg_ref, pp_ref, *rest):
    nf_ref = rest[0] if len(rest) == 4 else None
    o_ref, h_ref, acc_ref = rest[-3:]
    j = pl.program_id(1)

    @pl.when(j == 0)
    def _():
        h_ref[...] = _rms_bf16(x_ref[...], g_ref[...])
        acc_ref[...] = jnp.zeros_like(acc_ref)

    h = h_ref[...]
    a = _silu(_dot(h, wg_ref[...])) * _dot(h, wu_ref[...])
    acc_ref[...] += _dot(a.astype(BF16), wd_ref[...])

    @pl.when(j == pl.num_programs(1) - 1)
    def _():
        o_ref[...] = _ple_epilogue(x_ref[...] + acc_ref[...], p_ref, pg_ref, pp_ref, nf_ref)


def dense_ffn(x, g, w_gate, w_up, w_down, p, ple_gate, ple_proj, norm_final=None):
    t, d = x.shape
    f = w_gate.shape[1]
    tm = _row_tile(t, 512)
    tf = f // 2 if (f // 2) % LANES == 0 else f
    ple_specs, ple_args = _ple_operands(p, ple_gate, ple_proj, norm_final, tm, lambda i, j: (i, 0))
    return pl.pallas_call(
        _ffn_kernel,
        name="dense_ffn",
        grid=(t // tm, f // tf),
        in_specs=[
            pl.BlockSpec((tm, d), lambda i, j: (i, 0)),
            pl.BlockSpec((1, d), lambda i, j: (0, 0)),
            pl.BlockSpec((d, tf), lambda i, j: (0, j)),
            pl.BlockSpec((d, tf), lambda i, j: (0, j)),
            pl.BlockSpec((tf, d), lambda i, j: (j, 0)),
        ] + ple_specs,
        out_specs=pl.BlockSpec((tm, d), lambda i, j: (i, 0)),
        out_shape=jax.ShapeDtypeStruct((t, d), F32),
        scratch_shapes=[pltpu.VMEM((tm, d), BF16), pltpu.VMEM((tm, d), F32)],
        compiler_params=_cparams(("parallel", "arbitrary")),
    )(x, g.reshape(1, d), w_gate, w_up, w_down, *ple_args)


SSD_U_COLS = SSD_D_INNER + SSD_CONV_DIM + LANES
_SSD_GN = SSD_GROUPS * SSD_STATE


def _ssd_kernel(u_ref, cw_ref, cb_ref, dtb_ref, alog_ref, dsk_ref, nw_ref, tril_ref,
                o_ref, xpad_ref, state_ref, y_ref):
    L = SSD_CHUNK
    P = HEAD_DIM
    R = SSD_HEADS // SSD_GROUPS

    @pl.when(pl.program_id(1) == 0)
    def _():
        xpad_ref[0:SUBLANES, :] = jnp.zeros((SUBLANES, SSD_CONV_DIM), F32)
        state_ref[...] = jnp.zeros_like(state_ref)

    z = u_ref[0, :, 0:SSD_D_INNER]
    xbc = u_ref[0, :, SSD_D_INNER:SSD_D_INNER + SSD_CONV_DIM]
    dt_raw = u_ref[0, :, SSD_D_INNER + SSD_CONV_DIM:SSD_U_COLS]

    xpad_ref[SUBLANES:SUBLANES + L, :] = xbc
    conv = cb_ref[...]
    for j in range(SSD_CONV):
        conv = conv + cw_ref[j:j + 1, :] * xpad_ref[pl.ds(SUBLANES - (SSD_CONV - 1) + j, L), :]
    xpad_ref[0:SUBLANES, :] = xbc[L - SUBLANES:L, :]
    act = _silu(conv)
    xs = act[:, 0:SSD_D_INNER]
    bm = act[:, SSD_D_INNER:SSD_D_INNER + _SSD_GN]
    cm = act[:, SSD_D_INNER + _SSD_GN:SSD_CONV_DIM]

    dt = _softplus(dt_raw + dtb_ref[...])
    a_neg = -jnp.exp(alog_ref[...])
    tril = tril_ref[...]
    a_cum = jnp.dot(tril, dt * a_neg, precision=HIGHEST, preferred_element_type=F32)
    a_cum_t = a_cum.T
    dt_t = dt.T
    lower = tril > 0.5

    for g in range(SSD_GROUPS):
        bg = bm[:, g * SSD_STATE:(g + 1) * SSD_STATE]
        cg = cm[:, g * SSD_STATE:(g + 1) * SSD_STATE]
        bg16 = bg.astype(BF16)
        cg16 = cg.astype(BF16)
        cb = _dot_nt(cg16, bg16)
        bgt16 = bg.T.astype(BF16)
        for r in range(R):
            h = g * R + r
            a_col = a_cum[:, h:h + 1]
            a_row = a_cum_t[h:h + 1, :]
            a_last = a_cum[L - 1:L, h:h + 1]
            xh = xs[:, h * P:(h + 1) * P]
            decay = jnp.where(lower, jnp.exp(jnp.where(lower, a_col - a_row, 0.0)), 0.0)
            w_ls = cb * decay * dt_t[h:h + 1, :]
            y_diag = _dot(w_ls.astype(BF16), xh.astype(BF16))
            xw = xh * (jnp.exp(a_last - a_col) * dt[:, h:h + 1])
            st = _dot(bgt16, xw.astype(BF16))
            prev = state_ref[h]
            y_off = _dot(cg16, prev.astype(BF16)) * jnp.exp(a_col)
            state_ref[h] = prev * jnp.exp(a_last) + st
            y_ref[:, h * P:(h + 1) * P] = y_diag + y_off + dsk_ref[:, h:h + 1] * xh

    yg = y_ref[...] * _silu(z)
    gw = SSD_D_INNER // SSD_GROUPS
    for g in range(SSD_GROUPS):
        part = yg[:, g * gw:(g + 1) * gw]
        ms = jnp.mean(part * part, axis=-1, keepdims=True)
        o_ref[0, :, g * gw:(g + 1) * gw] = part * lax.rsqrt(ms + NORM_EPS) * nw_ref[:, g * gw:(g + 1) * gw]


def _lane_row(v):
    return jnp.pad(v.astype(F32), (0, LANES - v.shape[0])).reshape(1, LANES)


def ssd_mixer(u, conv_w, conv_b, dt_bias, a_log, d_skip, norm_w):
    b, s, _ = u.shape
    L = SSD_CHUNK
    tril = jnp.asarray(np.tril(np.ones((L, L), np.float32)))
    full = lambda shape: pl.BlockSpec(shape, lambda i, c: (0,) * len(shape))
    return pl.pallas_call(
        _ssd_kernel,
        name="ssd",
        grid=(b, s // L),
        in_specs=[
            pl.BlockSpec((1, L, SSD_U_COLS), lambda i, c: (i, c, 0)),
            full((SSD_CONV, SSD_CONV_DIM)), full((1, SSD_CONV_DIM)),
            full((1, LANES)), full((1, LANES)), full((1, LANES)),
            full((1, SSD_D_INNER)), full((L, L)),
        ],
        out_specs=pl.BlockSpec((1, L, SSD_D_INNER), lambda i, c: (i, c, 0)),
        out_shape=jax.ShapeDtypeStruct((b, s, SSD_D_INNER), F32),
        scratch_shapes=[pltpu.VMEM((SUBLANES + L, SSD_CONV_DIM), F32),
                        pltpu.VMEM((SSD_HEADS, SSD_STATE, HEAD_DIM), F32),
                        pltpu.VMEM((L, SSD_D_INNER), F32)],
        compiler_params=_cparams(("parallel", "arbitrary")),
    )(u, conv_w, conv_b.reshape(1, -1), _lane_row(dt_bias), _lane_row(a_log), _lane_row(d_skip),
      norm_w.reshape(1, -1), tril)


ATTN_TQ = 128
SWA_TQ = 256
_SCALE = HEAD_DIM ** -0.5


def _stack_heads(q, g, heads_per_group):
    parts = [q[:, (g * heads_per_group + r) * HEAD_DIM:(g * heads_per_group + r + 1) * HEAD_DIM]
             for r in range(heads_per_group)]
    return jnp.concatenate(parts, axis=0)


def _row_pos(t0, tq, reps):
    row = lax.broadcasted_iota(jnp.int32, (reps * tq, 1), 0)
    return t0 + (row & (tq - 1))


def _rope(x, cosf, sinf):
    n = x.shape[1]
    half = HEAD_DIM // 2
    lane = lax.broadcasted_iota(jnp.int32, x.shape, 1)
    first = (lane & (HEAD_DIM - 1)) < half
    rot = jnp.where(first, pltpu.roll(x, n - half, 1), pltpu.roll(x, half, 1))
    return x * cosf + rot * sinf


def _rope_table_kernel(pos_ref, cos_ref, sin_ref):
    half = HEAD_DIM // 2
    lane = lax.broadcasted_iota(jnp.int32, (1, LANES), 1)
    expo = -(lane & (half - 1)).astype(F32) / half
    inv_freq = jnp.power(jnp.full((1, LANES), ROPE_THETA, F32), expo)
    sign = jnp.where((lane & (HEAD_DIM - 1)) < half, -1.0, 1.0)
    ang = pos_ref[0] * inv_freq
    cos_ref[0] = jnp.cos(ang)
    sin_ref[0] = jnp.sin(ang) * sign


def rope_tables(positions):
    b, s = positions.shape
    tm = _row_tile(s, 512)
    out = pl.BlockSpec((1, tm, LANES), lambda i, j: (i, j, 0))
    return pl.pallas_call(
        _rope_table_kernel,
        name="rope_tables",
        grid=(b, s // tm),
        in_specs=[pl.BlockSpec((1, tm, 1), lambda i, j: (i, j, 0))],
        out_specs=[out, out],
        out_shape=[jax.ShapeDtypeStruct((b, s, LANES), F32)] * 2,
        compiler_params=_cparams(("parallel", "parallel")),
    )(positions.astype(F32).reshape(b, s, 1))


def _swa_kernel(q_ref, kv_ref, cosq_ref, sinq_ref, cosk_ref, sink_ref, sinks_ref, o_ref):
    tq = q_ref.shape[1]
    W = SWA_WINDOW
    span = W + tq
    R = SWA_HEADS // SWA_KV_HEADS
    kw = SWA_KV_HEADS * HEAD_DIM
    t0 = pl.program_id(1) * tq
    start = pl.multiple_of(jnp.maximum(t0 - W, 0), SUBLANES)

    reps = q_ref.shape[2] // LANES
    qr = _rope(q_ref[0], jnp.concatenate([cosq_ref[0]] * reps, 1), jnp.concatenate([sinq_ref[0]] * reps, 1))
    qr = qr * _SCALE
    kvs = kv_ref[0, pl.ds(start, span), :]
    kr = _rope(kvs[:, 0:kw], cosk_ref[0, pl.ds(start, span), :], sink_ref[0, pl.ds(start, span), :])
    v = kvs[:, kw:2 * kw]

    rel = _row_pos(t0, tq, 1) - (start + lax.broadcasted_iota(jnp.int32, (1, span), 1))
    bias = jnp.where((rel >= 0) & (rel < W), 0.0, NEG)
    for g in range(SWA_KV_HEADS):
        qg = _stack_heads(qr, g, R).astype(BF16)
        s = _dot_nt(qg, kr[:, g * HEAD_DIM:(g + 1) * HEAD_DIM].astype(BF16)).reshape(R, tq, span) + bias
        snk = jnp.concatenate([jnp.full((1, tq, 1), sinks_ref[g * R + r], F32) for r in range(R)], 0)
        m = jnp.maximum(jnp.max(s, axis=-1, keepdims=True), snk)
        e = jnp.exp(s - m)
        den = jnp.sum(e, axis=-1, keepdims=True) + jnp.exp(snk - m)
        o = _dot(e.reshape(R * tq, span).astype(BF16), v[:, g * HEAD_DIM:(g + 1) * HEAD_DIM].astype(BF16))
        o = o.reshape(R, tq, HEAD_DIM) / den
        for r in range(R):
            h = g * R + r
            o_ref[0, :, h * HEAD_DIM:(h + 1) * HEAD_DIM] = o[r]


def swa_mixer(u, rope_cos, rope_sin, sinks):
    b, s, _ = u.shape
    tq = SWA_TQ
    qw = SWA_HEADS * HEAD_DIM
    kvw = 2 * SWA_KV_HEADS * HEAD_DIM
    tile = pl.BlockSpec((1, tq, LANES), lambda i, j: (i, j, 0))
    full = pl.BlockSpec((1, s, LANES), lambda i, j: (i, 0, 0))
    return pl.pallas_call(
        _swa_kernel,
        name="swa",
        grid=(b, s // tq),
        in_specs=[
            pl.BlockSpec((1, tq, qw), lambda i, j: (i, j, 0)),
            pl.BlockSpec((1, s, kvw), lambda i, j: (i, 0, qw // kvw)),
            tile, tile, full, full,
            pl.BlockSpec(memory_space=pltpu.SMEM),
        ],
        out_specs=pl.BlockSpec((1, tq, qw), lambda i, j: (i, j, 0)),
        out_shape=jax.ShapeDtypeStruct((b, s, qw), F32),
        compiler_params=_cparams(("parallel", "arbitrary")),
    )(u, u, rope_cos, rope_sin, rope_cos, rope_sin, sinks.astype(F32))


NSA_U_COLS = NSA_HEADS * HEAD_DIM + 6 * NSA_KV_HEADS * HEAD_DIM + LANES
_NSA_KVW = NSA_KV_HEADS * HEAD_DIM
_NSA_QBLK = NSA_HEADS * HEAD_DIM // _NSA_KVW
_NSA_R = NSA_HEADS // NSA_KV_HEADS
_CMP_HALF = NSA_CMP_BLOCK // 2
NSA_KEY_CHUNK = 256
NSA_SELECT_TQ = 256


def _nsa_compress_kernel(k_ref, v_ref, pe_ref, w1_ref, w2_ref, kc_ref, vc_ref):
    nc = kc_ref.shape[1]
    for idx, (x_ref, o_ref) in enumerate(((k_ref, kc_ref), (v_ref, vc_ref))):
        ha = jnp.zeros((nc, _NSA_KVW), F32)
        hb = jnp.zeros((nc, _NSA_KVW), F32)
        for l in range(_CMP_HALF):
            y = x_ref[0, pl.ds(l, nc, stride=NSA_CMP_STRIDE), :]
            ha = ha + _dot((y + pe_ref[idx, l:l + 1, :]).astype(BF16), w1_ref[idx, l])
            hb = hb + _dot((y + pe_ref[idx, _CMP_HALF + l:_CMP_HALF + l + 1, :]).astype(BF16),
                           w1_ref[idx, _CMP_HALF + l])
        hid = _silu(ha + pltpu.roll(hb, nc - 1, 0))
        o_ref[0] = _dot(hid.astype(BF16), w2_ref[idx])


def _nsa_select_kernel(q_ref, kc_ref, vc_ref, ov_ref, ocmp_ref, sel_ref):
    tq = q_ref.shape[1]
    nc = kc_ref.shape[1]
    nsel = ov_ref.shape[0]
    k_eff = min(NSA_TOPK, nsel)
    t0 = pl.program_id(1) * tq
    t_row = _row_pos(t0, tq, _NSA_R)
    cmp_end = lax.broadcasted_iota(jnp.int32, (1, nc), 1) * NSA_CMP_STRIDE + (NSA_CMP_BLOCK - 1)
    cvalid = cmp_end <= t_row
    jidx = lax.broadcasted_iota(jnp.int32, (nsel, 1), 0)
    blk_t = (t0 + lax.broadcasted_iota(jnp.int32, (1, tq), 1)) // NSA_SEL_BLOCK
    jvalid = jidx <= blk_t
    forced = (jidx == 0) | (jidx == blk_t)
    q = q_ref[0]
    sel_rows = []
    for g in range(NSA_KV_HEADS):
        qg = _stack_heads(q, g, _NSA_R).astype(BF16)
        kc = kc_ref[0, :, g * HEAD_DIM:(g + 1) * HEAD_DIM].astype(BF16)
        vc = vc_ref[0, :, g * HEAD_DIM:(g + 1) * HEAD_DIM].astype(BF16)
        s = _dot_nt(qg, kc) * _SCALE
        m = jnp.max(jnp.where(cvalid, s, NEG), axis=-1, keepdims=True)
        e = jnp.exp(jnp.where(cvalid, s - m, NEG))
        den = jnp.sum(e, axis=-1, keepdims=True)
        p = e / jnp.where(den > 0.0, den, 1.0)
        o = _dot(p.astype(BF16), vc)
        psum = p[0:tq]
        for r in range(_NSA_R):
            h = g * _NSA_R + r
            ocmp_ref[0, :, h * HEAD_DIM:(h + 1) * HEAD_DIM] = o[r * tq:(r + 1) * tq]
            if r:
                psum = psum + p[r * tq:(r + 1) * tq]
        imp_t = lax.dot_general(ov_ref[...], psum, (((1,), (1,)), ((), ())),
                                precision=HIGHEST, preferred_element_type=F32)
        score = jnp.where(forced, BIG, jnp.where(jvalid, imp_t, NEG))
        cnt = jnp.zeros((nsel, tq), F32)
        for i in range(nsel):
            si = score[i:i + 1, :]
            beats = (si > score) | ((si == score) & (jidx > i))
            cnt = cnt + jnp.where(beats, 1.0, 0.0)
        sel_rows.append(jnp.where(cnt < k_eff, 1.0, 0.0))
    pad = LANES - NSA_KV_HEADS * nsel
    sel_t = jnp.concatenate(sel_rows + [jnp.zeros((pad, tq), F32)], axis=0)
    sel_ref[0] = sel_t.T


def _nsa_attend_kernel(q_ref, sel_ref, gate_ref, ocmp_ref, ks_ref, vs_ref, kw_ref, vw_ref, ext_ref, o_ref):
    tq = q_ref.shape[1]
    s_len = ks_ref.shape[1]
    W = NSA_WINDOW
    span = min(W + tq, s_len)
    t0 = pl.program_id(1) * tq
    start = pl.multiple_of(jnp.maximum(t0 - W, 0), SUBLANES)
    t_row = t0 + lax.broadcasted_iota(jnp.int32, (1, tq), 1)
    rel = t_row - (start + lax.broadcasted_iota(jnp.int32, (span, 1), 0))
    win_bias = jnp.where((rel >= 0) & (rel < W), 0.0, NEG)
    q = q_ref[0] * _SCALE
    sig = _sigmoid(gate_ref[0])
    sel16 = sel_ref[0].astype(BF16)

    def attend_t(qg, k, v, bias_t):
        s = _dot_nt(k.astype(BF16), qg) + jnp.concatenate([bias_t] * _NSA_R, axis=1)
        e = jnp.exp((s - jnp.max(s, axis=0, keepdims=True)).astype(BF16))
        v1 = jnp.concatenate([v.astype(BF16), jnp.ones(v.shape, BF16)], axis=1)
        o_t = lax.dot_general(v1, e, (((0,), (0,)), ((), ())), preferred_element_type=F32)
        return o_t[0:HEAD_DIM] / o_t[HEAD_DIM:HEAD_DIM + 1]

    chunk = min(NSA_KEY_CHUNK, s_len)
    for c in range(s_len // chunk):
        klen = (c + 1) * chunk

        @pl.when(t0 // chunk == c)
        def _():
            kidx = lax.broadcasted_iota(jnp.int32, (klen, 1), 0)
            sel_t, win_t = [], []
            for g in range(NSA_KV_HEADS):
                cols = slice(g * HEAD_DIM, (g + 1) * HEAD_DIM)
                qg = _stack_heads(q, g, _NSA_R).astype(BF16)
                chosen = _dot_nt(ext_ref[g, 0:klen, :], sel16) > 0.5
                sel_bias = jnp.where(chosen & (kidx <= t_row), 0.0, NEG)
                sel_t.append(attend_t(qg, ks_ref[0, 0:klen, cols], vs_ref[0, 0:klen, cols], sel_bias))
                win_t.append(attend_t(qg, kw_ref[0, pl.ds(start, span), cols],
                                      vw_ref[0, pl.ds(start, span), cols], win_bias))
            o_sel = jnp.concatenate(sel_t, axis=0).T
            o_win = jnp.concatenate(win_t, axis=0).T
            for g in range(NSA_KV_HEADS):
                for r in range(_NSA_R):
                    h = g * _NSA_R + r
                    hc = slice(h * HEAD_DIM, (h + 1) * HEAD_DIM)
                    rows = slice(r * tq, (r + 1) * tq)
                    gc = slice(g * HEAD_DIM, (g + 1) * HEAD_DIM)
                    o_ref[0, :, hc] = (sig[:, h:h + 1] * ocmp_ref[0, :, hc]
                                       + sig[:, NSA_HEADS + h:NSA_HEADS + h + 1] * o_sel[rows, gc]
                                       + sig[:, 2 * NSA_HEADS + h:2 * NSA_HEADS + h + 1] * o_win[rows, gc])


def _block_diag2(w):
    z = jnp.zeros_like(w)
    return jnp.concatenate([jnp.concatenate([w, z], -1), jnp.concatenate([z, w], -1)], -2)


def nsa_mixer(u, cmp_pe, cmp_w1, cmp_w2):
    b, s, _ = u.shape
    tq = ATTN_TQ
    nc = s // NSA_CMP_STRIDE
    nsel = s // NSA_SEL_BLOCK
    qw = NSA_HEADS * HEAD_DIM
    pe2 = jnp.concatenate([cmp_pe, cmp_pe], -1)
    w1bd = _block_diag2(cmp_w1).astype(BF16)
    w2bd = _block_diag2(cmp_w2).astype(BF16)
    const = lambda shape: pl.BlockSpec(shape, lambda *_: (0,) * len(shape))
    col = lambda c: pl.BlockSpec((1, s, _NSA_KVW), lambda i, *_: (i, 0, _NSA_QBLK + c))

    kc, vc = pl.pallas_call(
        _nsa_compress_kernel,
        name="nsa_compress",
        grid=(b,),
        in_specs=[col(0), col(1), const(pe2.shape), const(w1bd.shape), const(w2bd.shape)],
        out_specs=[pl.BlockSpec((1, nc, _NSA_KVW), lambda i: (i, 0, 0))] * 2,
        out_shape=[jax.ShapeDtypeStruct((b, nc, _NSA_KVW), F32)] * 2,
        compiler_params=_cparams(("parallel",)),
    )(u, u, pe2, w1bd, w2bd)

    c_start = np.arange(nc) * NSA_CMP_STRIDE
    s_start = np.arange(nsel) * NSA_SEL_BLOCK
    ov_t = ((c_start[None, :] < s_start[:, None] + NSA_SEL_BLOCK)
            & (c_start[None, :] + NSA_CMP_BLOCK > s_start[:, None])
            & (np.arange(nc)[None, :] < nc - 1)).astype(np.float32)
    qspec = pl.BlockSpec((1, tq, qw), lambda i, j: (i, j, 0))
    ts = _row_tile(s, NSA_SELECT_TQ)
    sel_q = pl.BlockSpec((1, ts, qw), lambda i, j: (i, j, 0))
    o_cmp, sel = pl.pallas_call(
        _nsa_select_kernel,
        name="nsa_select",
        grid=(b, s // ts),
        in_specs=[sel_q,
                  pl.BlockSpec((1, nc, _NSA_KVW), lambda i, j: (i, 0, 0)),
                  pl.BlockSpec((1, nc, _NSA_KVW), lambda i, j: (i, 0, 0)),
                  const(ov_t.shape)],
        out_specs=[sel_q, pl.BlockSpec((1, ts, LANES), lambda i, j: (i, j, 0))],
        out_shape=[jax.ShapeDtypeStruct((b, s, qw), F32), jax.ShapeDtypeStruct((b, s, LANES), F32)],
        compiler_params=_cparams(("parallel", "arbitrary")),
    )(u, kc, vc, jnp.asarray(ov_t))

    expand = np.zeros((NSA_KV_HEADS, s, LANES), np.float32)
    for g in range(NSA_KV_HEADS):
        expand[g, np.arange(s), g * nsel + np.arange(s) // NSA_SEL_BLOCK] = 1.0
    gate_blk = NSA_U_COLS // LANES - 1
    return pl.pallas_call(
        _nsa_attend_kernel,
        name="nsa_attend",
        grid=(b, s // tq),
        in_specs=[qspec,
                  pl.BlockSpec((1, tq, LANES), lambda i, j: (i, j, 0)),
                  pl.BlockSpec((1, tq, LANES), lambda i, j: (i, j, gate_blk)),
                  qspec,
                  col(2), col(3), col(4), col(5),
                  const(expand.shape)],
        out_specs=qspec,
        out_shape=jax.ShapeDtypeStruct((b, s, qw), F32),
        compiler_params=_cparams(("parallel", "arbitrary")),
    )(u, sel, u, o_cmp, u, u, u, u, jnp.asarray(expand, BF16))


def _seg_sum(x, ones16):
    hi = x.astype(BF16)
    lo = (x - hi.astype(F32)).astype(BF16)
    return _dot(hi, ones16) + _dot(lo, ones16)


def _head_ones(width):
    idx = np.arange(width) // HEAD_DIM
    return jnp.asarray((idx[:, None] == idx[None, :]).astype(np.float32), BF16)


_RW_R, _RW_K, _RW_V = 0, RWKV_DIM, 2 * RWKV_DIM
_RW_WD = 3 * RWKV_DIM
_RW_AD = _RW_WD + RWKV_W_LORA
_RW_GD = _RW_AD + RWKV_A_LORA


def _rwkv_prep_kernel(u_ref, mu_ref, w0_ref, wup_ref, a0_ref, aup_ref, gup_ref, kk_ref, ka_ref, rk_ref,
                      ones_ref, r_o, w_o, k_o, v_o, kk_o, q_o, g_o, bonus_o, up_ref):
    tm = u_ref.shape[1]

    @pl.when(pl.program_id(1) == 0)
    def _():
        up_ref[0:SUBLANES, :] = jnp.zeros((SUBLANES, RWKV_IN), F32)

    u = u_ref[0]
    up_ref[SUBLANES:SUBLANES + tm, :] = u
    prev = up_ref[pl.ds(SUBLANES - 1, tm), :]
    up_ref[0:SUBLANES, :] = u[tm - SUBLANES:tm, :]
    x = u + (prev - u) * mu_ref[...]
    r = x[:, _RW_R:_RW_R + RWKV_DIM]
    k = x[:, _RW_K:_RW_K + RWKV_DIM]
    v = x[:, _RW_V:_RW_V + RWKV_DIM]
    wd = x[:, _RW_WD:_RW_AD]
    ad = x[:, _RW_AD:_RW_GD]
    gd = x[:, _RW_GD:RWKV_IN]
    w = -_softplus(-(w0_ref[...] + _dot(jnp.tanh(wd).astype(BF16), wup_ref[...]))) - 0.5
    a = _sigmoid(a0_ref[...] + _dot(ad.astype(BF16), aup_ref[...]))
    ones16 = ones_ref[...]
    kk = k * kk_ref[...]
    kk = kk / jnp.maximum(jnp.sqrt(_seg_sum(kk * kk, ones16)), 1e-12)
    k2 = k * (1.0 + (a - 1.0) * ka_ref[...])
    r_o[0] = r
    w_o[0] = jnp.exp(-jnp.exp(w))
    k_o[0] = k2
    v_o[0] = v
    kk_o[0] = kk
    q_o[0] = kk * a
    g_o[0] = _dot(_sigmoid(gd).astype(BF16), gup_ref[...])
    bonus_o[0] = _seg_sum(r * k2 * rk_ref[...], ones16) * v


_RWS_TB = 64
_RWS_VB = HEAD_DIM // 2 // SUBLANES
_RWS_NACC = 2
_RWS_VECS = 5


def _rwkv_scan_vhalf_kernel(kk_ref, w_ref, q_ref, k_ref, r_ref, v_ref, o_ref, s_ref, rows_ref):
    @pl.when(pl.program_id(0) == 0)
    def _():
        s_ref[...] = jnp.zeros_like(s_ref)

    low = lax.broadcasted_iota(jnp.int32, (SUBLANES, LANES), 1) < LANES // 2

    def reduce_acc(acc, vb):
        tot = acc[(vb, 0)]
        for i in range(1, _RWS_NACC):
            tot = tot + acc[(vb, i)]
        return tot

    def group(tg, carry):
        t8 = pl.multiple_of(tg * SUBLANES, SUBLANES)
        for a, ref in enumerate((kk_ref, w_ref, q_ref, k_ref, r_ref)):
            for i in range(HEAD_DIM // 2):
                x = ref[i, pl.ds(t8, SUBLANES), :]
                xr = pltpu.roll(x, LANES // 2, 1)
                rows_ref[a, 0, i] = jnp.where(low, x, xr)
                rows_ref[a, 1, i] = jnp.where(low, xr, x)
        for tl in range(SUBLANES):
            t = t8 + tl
            key_row = lambda a, k: rows_ref[a, k % 2, k // 2, tl:tl + 1, :]
            acc = {}
            for k in range(HEAD_DIM):
                kk_row = key_row(0, k)
                for vb in range(_RWS_VB):
                    term = s_ref[vb, k] * kk_row
                    key = (vb, k % _RWS_NACC)
                    acc[key] = acc[key] + term if key in acc else term
            sa = [reduce_acc(acc, vb) for vb in range(_RWS_VB)]
            vt = [v_ref[t, vb * SUBLANES:(vb + 1) * SUBLANES, :] for vb in range(_RWS_VB)]
            acc = {}
            for k in range(HEAD_DIM):
                w_row, q_row, k_row, r_row = (key_row(a, k) for a in range(1, _RWS_VECS))
                for vb in range(_RWS_VB):
                    st = s_ref[vb, k] * w_row - sa[vb] * q_row + vt[vb] * k_row
                    s_ref[vb, k] = st
                    term = st * r_row
                    key = (vb, k % _RWS_NACC)
                    acc[key] = acc[key] + term if key in acc else term
            for vb in range(_RWS_VB):
                o_ref[t, vb * SUBLANES:(vb + 1) * SUBLANES, :] = reduce_acc(acc, vb)
        return carry

    lax.fori_loop(0, o_ref.shape[0] // SUBLANES, group, 0)


_RL_TB = 128
_RL_ROWS = HEAD_DIM // 2


def _scan_row_base(i, half, interleaved):
    return 2 * i + half if interleaved else half * _RL_ROWS + i


def _to_scan_kernel(*refs, interleaved):
    n = len(interleaved)
    ins, outs, y_ref = refs[:n], refs[n:2 * n], refs[2 * n]
    nb, tb = ins[0].shape[0], ins[0].shape[1]
    for a in range(n):
        for b in range(nb):
            y_ref[b] = ins[a][b].T
        for i in range(_RL_ROWS):
            parts = [y_ref[b, pl.ds(_scan_row_base(i, half, interleaved[a]), RWKV_HEADS, stride=HEAD_DIM), :]
                     for half in range(2) for b in range(nb)]
            rows = jnp.concatenate(parts, axis=0).T
            if interleaved[a]:
                outs[a][i] = rows
            else:
                outs[a][pl.ds(i, tb, stride=_RL_ROWS), :] = rows


def _from_scan_kernel(o_ref, bonus_ref, g_ref, lnw_ref, lnb_ref, out_ref, y_ref):
    nb, tb = out_ref.shape[0], out_ref.shape[1]
    for i in range(_RL_ROWS):
        zt = o_ref[pl.ds(i, tb, stride=_RL_ROWS), :].T
        for half in range(2):
            for b in range(nb):
                r0 = (half * nb + b) * RWKV_HEADS
                y_ref[b, pl.ds(_scan_row_base(i, half, False), RWKV_HEADS, stride=HEAD_DIM), :] = zt[r0:r0 + RWKV_HEADS]
    for b in range(nb):
        for h in range(RWKV_HEADS):
            rows = slice(h * HEAD_DIM, (h + 1) * HEAD_DIM)
            blk = y_ref[b, rows, :]
            cen = blk - jnp.mean(blk, axis=0, keepdims=True)
            var = jnp.mean(cen * cen, axis=0, keepdims=True)
            y_ref[b, rows, :] = cen * lax.rsqrt(var + RWKV_GN_EPS)
        o = y_ref[b].T * lnw_ref[...] + lnb_ref[...]
        out_ref[b] = (o + bonus_ref[b]) * g_ref[b]


def to_scan_layout(arrays, interleaved):
    b, s, dm = arrays[0].shape
    tb = _row_tile(s, _RL_TB)
    n = len(arrays)
    key_spec = pl.BlockSpec((_RL_ROWS, tb, LANES), lambda i: (0, i, 0))
    val_spec = pl.BlockSpec((tb * _RL_ROWS, LANES), lambda i: (i, 0))
    key_shape = jax.ShapeDtypeStruct((_RL_ROWS, s, LANES), F32)
    val_shape = jax.ShapeDtypeStruct((s * _RL_ROWS, LANES), F32)
    outs = pl.pallas_call(
        functools.partial(_to_scan_kernel, interleaved=tuple(interleaved)),
        name="rwkv_to_scan",
        grid=(s // tb,),
        in_specs=[pl.BlockSpec((b, tb, dm), lambda i: (0, i, 0))] * n,
        out_specs=[key_spec if f else val_spec for f in interleaved],
        out_shape=[key_shape if f else val_shape for f in interleaved],
        scratch_shapes=[pltpu.VMEM((b, dm, tb), F32)],
        compiler_params=_cparams(("parallel",)),
    )(*arrays)
    return [o if f else o.reshape(s, _RL_ROWS, LANES) for o, f in zip(outs, interleaved)]


def from_scan_layout(o, bonus, g, ln_w, ln_b):
    s = o.shape[0]
    b, _, dm = bonus.shape
    tb = _row_tile(s, _RL_TB)
    tile = pl.BlockSpec((b, tb, dm), lambda i: (0, i, 0))
    vec = pl.BlockSpec((1, dm), lambda i: (0, 0))
    return pl.pallas_call(
        _from_scan_kernel,
        name="rwkv_from_scan",
        grid=(s // tb,),
        in_specs=[pl.BlockSpec((tb * _RL_ROWS, LANES), lambda i: (i, 0)), tile, tile, vec, vec],
        out_specs=tile,
        out_shape=jax.ShapeDtypeStruct((b, s, dm), F32),
        scratch_shapes=[pltpu.VMEM((b, dm, tb), F32)],
        compiler_params=_cparams(("parallel",)),
    )(o.reshape(s * _RL_ROWS, LANES), bonus, g, ln_w.astype(F32).reshape(1, dm), ln_b.astype(F32).reshape(1, dm))


def rwkv7_mixer(u, mu, w0, w_up, a0, a_up, g_up, k_k, k_a, r_k, ln_w, ln_b):
    b, s, _ = u.shape
    dm = RWKV_DIM
    tm = _row_tile(s, 256)
    row = lambda v: v.astype(F32).reshape(1, -1)
    const = lambda shape: pl.BlockSpec(shape, lambda *_: (0,) * len(shape))
    ones_d = _head_ones(dm)
    tile = pl.BlockSpec((1, tm, dm), lambda i, j: (i, j, 0))
    outs = pl.pallas_call(
        _rwkv_prep_kernel,
        name="rwkv_prep",
        grid=(b, s // tm),
        in_specs=[pl.BlockSpec((1, tm, RWKV_IN), lambda i, j: (i, j, 0)),
                  const((1, RWKV_IN)), const((1, dm)), const((RWKV_W_LORA, dm)), const((1, dm)),
                  const((RWKV_A_LORA, dm)), const((RWKV_G_LORA, dm)), const((1, dm)), const((1, dm)),
                  const((1, dm)), const((dm, dm))],
        out_specs=[tile] * 8,
        out_shape=[jax.ShapeDtypeStruct((b, s, dm), F32)] * 8,
        scratch_shapes=[pltpu.VMEM((SUBLANES + tm, RWKV_IN), F32)],
        compiler_params=_cparams(("parallel", "arbitrary")),
    )(u, row(mu), row(w0), w_up.astype(BF16), row(a0), a_up.astype(BF16), g_up.astype(BF16),
      row(k_k), row(k_a), row(r_k), ones_d)
    r, wdec, k2, v, kk, q, g, bonus = outs

    nchain = b * RWKV_HEADS
    assert 2 * nchain == LANES and s % 2 == 0, "the scan kernel maps (half, batch, head) onto the 128 lanes"
    scan_in = to_scan_layout([kk, wdec, q, k2, r, v], interleaved=[True] * _RWS_VECS + [False])
    tb = _row_tile(s, _RWS_TB)
    kblk = pl.BlockSpec((_RL_ROWS, tb, LANES), lambda i: (0, i, 0))
    vblk = pl.BlockSpec((tb, _RL_ROWS, LANES), lambda i: (i, 0, 0))
    o = pl.pallas_call(
        _rwkv_scan_vhalf_kernel,
        name="rwkv_scan",
        grid=(s // tb,),
        in_specs=[kblk] * _RWS_VECS + [vblk],
        out_specs=vblk,
        out_shape=jax.ShapeDtypeStruct((s, _RL_ROWS, LANES), F32),
        scratch_shapes=[pltpu.VMEM((_RWS_VB, HEAD_DIM, SUBLANES, LANES), F32),
                        pltpu.VMEM((_RWS_VECS, 2, _RL_ROWS, SUBLANES, LANES), F32)],
        compiler_params=_cparams(("arbitrary",)),
    )(*scan_in)
    return from_scan_layout(o, bonus, g, ln_w, ln_b)


MOE_SLOT_BLOCK = 512
MOE_FF_CHUNK = 256
MOE_ROW_TILE = 256
MOE_DMA_UNROLL = 8
_META_E0, _META_E1, _META_R0, _META_R1, _META_W0, _META_W1 = range(6)


def _moe_route_kernel(x_ref, g_ref, wr_ref, tril_ref, h_o, meta_o, cnt_o, carry_ref):
    @pl.when(pl.program_id(0) == 0)
    def _():
        carry_ref[...] = jnp.zeros_like(carry_ref)

    x = x_ref[...]
    ms = jnp.mean(x * x, axis=-1, keepdims=True)
    h = x * lax.rsqrt(ms + NORM_EPS) * g_ref[...]
    h_o[...] = h
    logits = jnp.dot(h, wr_ref[...], precision=HIGHEST, preferred_element_type=F32)
    lane = lax.broadcasted_iota(jnp.int32, logits.shape, 1)
    l1 = jnp.where(lane < N_EXPERTS, logits, NEG)
    m1 = jnp.max(l1, axis=-1, keepdims=True)
    i1 = jnp.min(jnp.where(l1 == m1, lane, LANES), axis=-1, keepdims=True)
    l2 = jnp.where(lane == i1, NEG, l1)
    m2 = jnp.max(l2, axis=-1, keepdims=True)
    i2 = jnp.min(jnp.where(l2 == m2, lane, LANES), axis=-1, keepdims=True)
    e21 = jnp.exp(m2 - m1)
    w1 = 1.0 / (1.0 + e21)
    w2 = e21 * w1
    cnt = jnp.where((lane == i1) | (lane == i2), 1.0, 0.0)
    before = _dot(tril_ref[...], cnt.astype(BF16)) + carry_ref[...]
    r1 = jnp.sum(jnp.where(lane == i1, before, 0.0), axis=-1, keepdims=True)
    r2 = jnp.sum(jnp.where(lane == i2, before, 0.0), axis=-1, keepdims=True)
    carry_ref[...] += jnp.sum(cnt, axis=0, keepdims=True)
    cnt_o[...] = carry_ref[...]
    meta = jnp.zeros(logits.shape, F32)
    for idx, val in ((_META_E0, i1.astype(F32)), (_META_E1, i2.astype(F32)), (_META_R0, r1),
                     (_META_R1, r2), (_META_W0, w1), (_META_W1, w2)):
        meta = jnp.where(lane == idx, val, meta)
    meta_o[...] = meta


def _row_copy(src_ref, src_row, dst_ref, dst_row, sem):
    return pltpu.make_async_copy(src_ref.at[pl.ds(src_row, 1), :], dst_ref.at[pl.ds(dst_row, 1), :], sem)


def _moe_scatter_kernel(d0_ref, d1_ref, h_ref, xs_in_ref, xs_ref, sem):
    del xs_in_ref
    tm = h_ref.shape[0]

    def issue(r, c):
        _row_copy(h_ref, r, xs_ref, d0_ref[r], sem).start(priority=0)
        _row_copy(h_ref, r, xs_ref, d1_ref[r], sem).start(priority=1)
        return c

    def drain(r, c):
        _row_copy(h_ref, r, xs_ref, d0_ref[r], sem).wait()
        _row_copy(h_ref, r, xs_ref, d1_ref[r], sem).wait()
        return c

    lax.fori_loop(0, tm, issue, 0, unroll=MOE_DMA_UNROLL)
    lax.fori_loop(0, tm, drain, 0, unroll=MOE_DMA_UNROLL)


def _moe_expert_kernel(be_ref, nused_ref, xs_ref, wg_ref, wu_ref, wd_ref, ys_ref):
    del be_ref
    i = pl.program_id(0)

    @pl.when(i < nused_ref[0])
    def _():
        x16 = xs_ref[...].astype(BF16)
        acc = jnp.zeros(ys_ref.shape, F32)
        for c in range(D_FF // MOE_FF_CHUNK):
            cols = slice(c * MOE_FF_CHUNK, (c + 1) * MOE_FF_CHUNK)
            a = _silu(_dot(x16, wg_ref[0, :, cols])) * _dot(x16, wu_ref[0, :, cols])
            acc = acc + _dot(a.astype(BF16), wd_ref[0, cols, :])
        ys_ref[...] = acc

    @pl.when(i >= nused_ref[0])
    def _():
        ys_ref[...] = jnp.zeros_like(ys_ref)


def _moe_combine_kernel(d0_ref, d1_ref, x_ref, meta_ref, ys_ref, p_ref, pg_ref, pp_ref, *rest):
    nf_ref = rest[0] if len(rest) == 5 else None
    o_ref, b0_ref, b1_ref, sem = rest[-4:]
    tm = x_ref.shape[0]

    def issue(r, c):
        _row_copy(ys_ref, d0_ref[r], b0_ref, r, sem).start(priority=0)
        _row_copy(ys_ref, d1_ref[r], b1_ref, r, sem).start(priority=1)
        return c

    def drain(r, c):
        _row_copy(ys_ref, d0_ref[r], b0_ref, r, sem).wait()
        _row_copy(ys_ref, d1_ref[r], b1_ref, r, sem).wait()
        return c

    lax.fori_loop(0, tm, issue, 0, unroll=MOE_DMA_UNROLL)
    lax.fori_loop(0, tm, drain, 0, unroll=MOE_DMA_UNROLL)
    w0 = meta_ref[:, _META_W0:_META_W0 + 1]
    w1 = meta_ref[:, _META_W1:_META_W1 + 1]
    y = x_ref[...] + (w0 * b0_ref[...] + w1 * b1_ref[...])
    o_ref[...] = _ple_epilogue(y, p_ref, pg_ref, pp_ref, nf_ref)


def moe_ffn(x, g, router, w_gate, w_up, w_down, p, ple_gate, ple_proj, norm_final=None):
    t, d = x.shape
    tm = _row_tile(t, MOE_ROW_TILE)
    blk = MOE_SLOT_BLOCK
    nblk = -(-(t * TOP_K + N_EXPERTS * (blk - 1)) // blk)
    slots = nblk * blk
    tril = jnp.asarray(np.tril(np.ones((tm, tm), np.float32), -1), BF16)
    const = lambda shape: pl.BlockSpec(shape, lambda *_: (0,) * len(shape))
    row = lambda w: pl.BlockSpec((tm, w), lambda i: (i, 0))
    h, meta, counts = pl.pallas_call(
        _moe_route_kernel,
        name="moe_route",
        grid=(t // tm,),
        in_specs=[row(d), const((1, d)), const((d, LANES)), const((tm, tm))],
        out_specs=[row(d), row(LANES), const((1, LANES))],
        out_shape=[jax.ShapeDtypeStruct((t, d), F32), jax.ShapeDtypeStruct((t, LANES), F32),
                   jax.ShapeDtypeStruct((1, LANES), F32)],
        scratch_shapes=[pltpu.VMEM((1, LANES), F32)],
        compiler_params=_cparams(("arbitrary",)),
    )(x, g.reshape(1, d), _pad_cols(router.astype(F32), LANES), tril)

    cnt = counts[0, :N_EXPERTS].astype(jnp.int32)
    pcnt = (cnt + blk - 1) // blk * blk
    pend = jnp.cumsum(pcnt)
    pstart = pend - pcnt
    e0 = meta[:, _META_E0].astype(jnp.int32)
    e1 = meta[:, _META_E1].astype(jnp.int32)
    d0 = pstart[e0] + meta[:, _META_R0].astype(jnp.int32)
    d1 = pstart[e1] + meta[:, _META_R1].astype(jnp.int32)
    blk_start = jnp.arange(nblk, dtype=jnp.int32) * blk
    blk_e = jnp.minimum(jnp.sum((pend[None, :] <= blk_start[:, None]).astype(jnp.int32), axis=1),
                        N_EXPERTS - 1).astype(jnp.int32)
    nused = (pend[-1:] // blk).astype(jnp.int32)

    smem_rows = pl.BlockSpec((tm,), lambda i: (i,), memory_space=pltpu.SMEM)
    xs = pl.pallas_call(
        _moe_scatter_kernel,
        name="moe_scatter",
        grid=(t // tm,),
        in_specs=[smem_rows, smem_rows, row(d), pl.BlockSpec(memory_space=pl.ANY)],
        out_specs=pl.BlockSpec(memory_space=pl.ANY),
        out_shape=jax.ShapeDtypeStruct((slots, d), F32),
        scratch_shapes=[pltpu.SemaphoreType.DMA(())],
        input_output_aliases={3: 0},
        compiler_params=_cparams(("arbitrary",)),
    )(d0, d1, h, jnp.zeros((slots, d), F32))

    f = w_gate.shape[2]
    ys = pl.pallas_call(
        _moe_expert_kernel,
        name="moe_expert",
        grid_spec=pltpu.PrefetchScalarGridSpec(
            num_scalar_prefetch=2,
            grid=(nblk,),
            in_specs=[pl.BlockSpec((blk, d), lambda i, be, nu: (i, 0)),
                      pl.BlockSpec((1, d, f), lambda i, be, nu: (be[i], 0, 0)),
                      pl.BlockSpec((1, d, f), lambda i, be, nu: (be[i], 0, 0)),
                      pl.BlockSpec((1, f, d), lambda i, be, nu: (be[i], 0, 0))],
            out_specs=pl.BlockSpec((blk, d), lambda i, be, nu: (i, 0)),
        ),
        out_shape=jax.ShapeDtypeStruct((slots, d), F32),
        compiler_params=_cparams(("arbitrary",)),
    )(blk_e, nused, xs, w_gate, w_up, w_down)

    ple_specs, ple_args = _ple_operands(p, ple_gate, ple_proj, norm_final, tm, lambda i: (i, 0))
    return pl.pallas_call(
        _moe_combine_kernel,
        name="moe_combine",
        grid=(t // tm,),
        in_specs=[smem_rows, smem_rows, row(d), row(LANES), pl.BlockSpec(memory_space=pl.ANY)] + ple_specs,
        out_specs=row(d),
        out_shape=jax.ShapeDtypeStruct((t, d), F32),
        scratch_shapes=[pltpu.VMEM((tm, d), F32), pltpu.VMEM((tm, d), F32), pltpu.SemaphoreType.DMA(())],
        compiler_params=_cparams(("arbitrary",)),
    )(d0, d1, x, meta, ys, *ple_args)


_IN_SPLITS = (SSD_D_INNER, SSD_CONV_DIM, SSD_HEADS,
              NSA_HEADS * HEAD_DIM, 6 * NSA_KV_HEADS * HEAD_DIM, 3 * NSA_HEADS,
              RWKV_IN,
              SWA_HEADS * HEAD_DIM, 2 * SWA_KV_HEADS * HEAD_DIM,
              N_BRANCHES * D_MODEL)
_IN_OFF = tuple(int(o) for o in np.cumsum((0,) + _IN_SPLITS))


def _split_in_proj(w):
    o = _IN_OFF
    seg = lambda a, b: w[:, o[a]:o[b]]
    w_ssd = jnp.concatenate([seg(0, 2), _pad_cols(seg(2, 3), LANES)], axis=1)
    w_nsa = jnp.concatenate([seg(3, 5), _pad_cols(seg(5, 6), LANES)], axis=1)
    return tuple(m.astype(BF16) for m in (w_ssd, w_nsa, seg(6, 7), seg(7, 9), seg(9, 10)))


def kernel(x, p, positions, norm_mix, w_in, ssd_conv_w, ssd_conv_b, ssd_dt_bias, ssd_a_log, ssd_d, ssd_norm, nsa_cmp_pe, nsa_cmp_w1, nsa_cmp_w2, rwkv_mu, rwkv_w0, rwkv_w_up, rwkv_a0, rwkv_a_up, rwkv_g_up, rwkv_k_k, rwkv_k_a, rwkv_r_k, rwkv_ln_w, rwkv_ln_b, swa_sinks, w_br_ssd, w_br_nsa, w_br_rwkv, w_br_swa, w_out, norm_ffn, ffn_w_gate, ffn_w_up, ffn_w_down, moe_router, moe_w_gate, moe_w_up, moe_w_down, ple_proj, ple_gate, norm_final):
    b, s, d = x.shape
    t = b * s
    depth = w_in.shape[0]
    xf = x.reshape(t, d)
    rope_cos, rope_sin = rope_tables(positions)
    for i in range(depth):
        w_ssd, w_nsa, w_rwkv, w_swa, w_gates = _split_in_proj(w_in[i])
        g_mix = norm_mix[i]
        u_ssd = norm_matmul(xf, g_mix, w_ssd).reshape(b, s, -1)
        u_nsa = norm_matmul(xf, g_mix, w_nsa).reshape(b, s, -1)
        u_rwkv = norm_matmul(xf, g_mix, w_rwkv).reshape(b, s, -1)
        u_swa = norm_matmul(xf, g_mix, w_swa).reshape(b, s, -1)
        y_ssd = ssd_mixer(u_ssd, ssd_conv_w[i], ssd_conv_b[i], ssd_dt_bias[i], ssd_a_log[i], ssd_d[i], ssd_norm[i])
        y_nsa = nsa_mixer(u_nsa, nsa_cmp_pe[i], nsa_cmp_w1[i], nsa_cmp_w2[i])
        y_rwkv = rwkv7_mixer(u_rwkv, rwkv_mu[i], rwkv_w0[i], rwkv_w_up[i], rwkv_a0[i], rwkv_a_up[i],
                             rwkv_g_up[i], rwkv_k_k[i], rwkv_k_a[i], rwkv_r_k[i], rwkv_ln_w[i], rwkv_ln_b[i])
        y_swa = swa_mixer(u_swa, rope_cos, rope_sin, swa_sinks[i])
        p_stack = jnp.stack([w_br_ssd[i], w_br_nsa[i], w_br_rwkv[i], w_br_swa[i]]).astype(BF16)
        ys = [y.reshape(t, -1) for y in (y_ssd, y_nsa, y_rwkv, y_swa)]
        xf = merge_branches(xf, g_mix, w_gates, ys, p_stack, w_out[i].astype(BF16))
        j = i // 2
        ple_args = (p[i].reshape(t, -1), ple_gate[i].astype(BF16), ple_proj[i].astype(BF16),
                    norm_final if i == depth - 1 else None)
        if i % 2 == 0:
            xf = dense_ffn(xf, norm_ffn[i], ffn_w_gate[j].astype(BF16), ffn_w_up[j].astype(BF16),
                           ffn_w_down[j].astype(BF16), *ple_args)
        else:
            xf = moe_ffn(xf, norm_ffn[i], moe_router[j], moe_w_gate[j].astype(BF16),
                         moe_w_up[j].astype(BF16), moe_w_down[j].astype(BF16), *ple_args)
    return xf.reshape(b, s, d)
```

```python
import functools

import numpy as np
import jax
import jax.numpy as jnp
from jax import lax
from jax.experimental import pallas as pl
from jax.experimental.pallas import tpu as pltpu

F32 = jnp.float32
BF16 = jnp.bfloat16

D_MODEL = 1024
HEAD_DIM = 64
NORM_EPS = 1e-6
NEG = -1e30
BIG = 1e30

SSD_HEADS = 8
SSD_D_INNER = SSD_HEADS * HEAD_DIM
SSD_STATE = 128
SSD_GROUPS = 2
SSD_CONV = 4
SSD_CHUNK = 128
SSD_CONV_DIM = SSD_D_INNER + 2 * SSD_GROUPS * SSD_STATE

NSA_HEADS = 8
NSA_KV_HEADS = 2
NSA_CMP_BLOCK = 32
NSA_CMP_STRIDE = 16
NSA_CMP_HIDDEN = 64
NSA_SEL_BLOCK = 64
NSA_TOPK = 8
NSA_WINDOW = 512

RWKV_HEADS = 8
RWKV_DIM = RWKV_HEADS * HEAD_DIM
RWKV_W_LORA = 64
RWKV_A_LORA = 64
RWKV_G_LORA = 128
RWKV_IN = 3 * RWKV_DIM + RWKV_W_LORA + RWKV_A_LORA + RWKV_G_LORA
RWKV_GN_EPS = 64e-5

SWA_HEADS = 8
SWA_KV_HEADS = 2
SWA_WINDOW = 128
ROPE_THETA = 150000.0

N_BRANCHES = 4
D_FF = 2816
N_EXPERTS = 8
TOP_K = 2
PLE_DIM = 256

LANES = 128
SUBLANES = 8
VMEM_LIMIT = 56 * 1024 * 1024

HIGHEST = lax.Precision.HIGHEST


def _cparams(sem):
    return pltpu.CompilerParams(dimension_semantics=sem, vmem_limit_bytes=VMEM_LIMIT)


def _sigmoid(x):
    return 1.0 / (1.0 + jnp.exp(-x))


def _silu(x):
    return x * _sigmoid(x)


def _softplus(x):
    return jnp.maximum(x, 0.0) + jnp.log1p(jnp.exp(-jnp.abs(x)))


def _dot(a, b):
    return jnp.dot(a, b, preferred_element_type=F32)


def _dot_nt(a, b):
    return lax.dot_general(a, b, (((1,), (1,)), ((), ())), preferred_element_type=F32)


def _pad_cols(w, n):
    return jnp.pad(w, ((0, 0), (0, n - w.shape[1])))


def _row_tile(t, pref):
    while t % pref:
        pref //= 2
    return pref


def _rms_bf16(x, g):
    ms = jnp.mean(x * x, axis=-1, keepdims=True)
    return (x * lax.rsqrt(ms + NORM_EPS) * g).astype(BF16)


def _norm_matmul_kernel(x_ref, g_ref, w_ref, o_ref):
    o_ref[...] = _dot(_rms_bf16(x_ref[...], g_ref[...]), w_ref[...])


def norm_matmul(x, g, w):
    t, d = x.shape
    n = w.shape[1]
    tm = _row_tile(t, 1024)
    return pl.pallas_call(
        _norm_matmul_kernel,
        name="norm_matmul",
        grid=(t // tm,),
        in_specs=[
            pl.BlockSpec((tm, d), lambda i: (i, 0)),
            pl.BlockSpec((1, d), lambda i: (0, 0)),
            pl.BlockSpec((d, n), lambda i: (0, 0)),
        ],
        out_specs=pl.BlockSpec((tm, n), lambda i: (i, 0)),
        out_shape=jax.ShapeDtypeStruct((t, n), F32),
        compiler_params=_cparams(("parallel",)),
    )(x, g.reshape(1, d), w)


def _merge_kernel(x_ref, g_ref, wg_ref, y0_ref, y1_ref, y2_ref, y3_ref, p_ref, wo_ref, o_ref):
    x = x_ref[...]
    h = _rms_bf16(x, g_ref[...])
    acc = None
    for m, y_ref in enumerate((y0_ref, y1_ref, y2_ref, y3_ref)):
        pm = _dot(y_ref[...].astype(BF16), p_ref[m])
        gm = _sigmoid(_dot(h, wg_ref[:, m * D_MODEL:(m + 1) * D_MODEL]))
        acc = gm * pm if acc is None else acc + gm * pm
    o_ref[...] = x + _dot(acc.astype(BF16), wo_ref[...])


def merge_branches(x, g, w_gates, ys, p_stack, w_out):
    t, d = x.shape
    tm = _row_tile(t, 512)
    dm = ys[0].shape[1]
    row = lambda w: pl.BlockSpec((tm, w), lambda i: (i, 0))
    const = lambda shape: pl.BlockSpec(shape, lambda i: (0,) * len(shape))
    return pl.pallas_call(
        _merge_kernel,
        name="merge",
        grid=(t // tm,),
        in_specs=[row(d), const((1, d)), const((d, N_BRANCHES * d)), row(dm), row(dm), row(dm), row(dm),
                  const((N_BRANCHES, dm, d)), const((d, d))],
        out_specs=row(d),
        out_shape=jax.ShapeDtypeStruct((t, d), F32),
        compiler_params=_cparams(("parallel",)),
    )(x, g.reshape(1, d), w_gates, *ys, p_stack, w_out)


def _ple_epilogue(y, p_ref, pg_ref, pp_ref, nf_ref):
    gate = _sigmoid(_dot(y.astype(BF16), pg_ref[...]))
    y = y + gate * _dot(p_ref[...].astype(BF16), pp_ref[...])
    if nf_ref is not None:
        ms = jnp.mean(y * y, axis=-1, keepdims=True)
        y = y * lax.rsqrt(ms + NORM_EPS) * nf_ref[...]
    return y


def _ple_operands(p, ple_gate, ple_proj, norm_final, tm, index_map):
    d = ple_gate.shape[0]
    const = lambda shape: pl.BlockSpec(shape, lambda *_: (0,) * len(shape))
    specs = [pl.BlockSpec((tm, p.shape[1]), index_map), const((d, d)), const((p.shape[1], d))]
    args = [p, ple_gate, ple_proj]
    if norm_final is not None:
        specs.append(const((1, d)))
        args.append(norm_final.reshape(1, d))
    return specs, args


def _ffn_kernel(x_ref, g_ref, wg_ref, wu_ref, wd_ref, p_ref, pg_ref, pp_ref, *rest):
    nf_ref = rest[0] if len(rest) == 4 else None
    o_ref, h_ref, acc_ref = rest[-3:]
    j = pl.program_id(1)

    @pl.when(j == 0)
    def _():
        h_ref[...] = _rms_bf16(x_ref[...], g_ref[...])
        acc_ref[...] = jnp.zeros_like(acc_ref)

    h = h_ref[...]
    a = _silu(_dot(h, wg_ref[...])) * _dot(h, wu_ref[...])
    acc_ref[...] += _dot(a.astype(BF16), wd_ref[...])

    @pl.when(j == pl.num_programs(1) - 1)
    def _():
        o_ref[...] = _ple_epilogue(x_ref[...] + acc_ref[...], p_ref, pg_ref, pp_ref, nf_ref)


def dense_ffn(x, g, w_gate, w_up, w_down, p, ple_gate, ple_proj, norm_final=None):
    t, d = x.shape
    f = w_gate.shape[1]
    tm = _row_tile(t, 512)
    tf = f // 2 if (f // 2) % LANES == 0 else f
    ple_specs, ple_args = _ple_operands(p, ple_gate, ple_proj, norm_final, tm, lambda i, j: (i, 0))
    return pl.pallas_call(
        _ffn_kernel,
        name="dense_ffn",
        grid=(t // tm, f // tf),
        in_specs=[
            pl.BlockSpec((tm, d), lambda i, j: (i, 0)),
            pl.BlockSpec((1, d), lambda i, j: (0, 0)),
            pl.BlockSpec((d, tf), lambda i, j: (0, j)),
            pl.BlockSpec((d, tf), lambda i, j: (0, j)),
            pl.BlockSpec((tf, d), lambda i, j: (j, 0)),
        ] + ple_specs,
        out_specs=pl.BlockSpec((tm, d), lambda i, j: (i, 0)),
        out_shape=jax.ShapeDtypeStruct((t, d), F32),
        scratch_shapes=[pltpu.VMEM((tm, d), BF16), pltpu.VMEM((tm, d), F32)],
        compiler_params=_cparams(("parallel", "arbitrary")),
    )(x, g.reshape(1, d), w_gate, w_up, w_down, *ple_args)


SSD_U_COLS = SSD_D_INNER + SSD_CONV_DIM + LANES
_SSD_GN = SSD_GROUPS * SSD_STATE


def _ssd_kernel(u_ref, cw_ref, cb_ref, dtb_ref, alog_ref, dsk_ref, nw_ref, tril_ref,
                o_ref, xpad_ref, state_ref, y_ref):
    L = SSD_CHUNK
    P = HEAD_DIM
    R = SSD_HEADS // SSD_GROUPS

    @pl.when(pl.program_id(1) == 0)
    def _():
        xpad_ref[0:SUBLANES, :] = jnp.zeros((SUBLANES, SSD_CONV_DIM), F32)
        state_ref[...] = jnp.zeros_like(state_ref)

    z = u_ref[0, :, 0:SSD_D_INNER]
    xbc = u_ref[0, :, SSD_D_INNER:SSD_D_INNER + SSD_CONV_DIM]
    dt_raw = u_ref[0, :, SSD_D_INNER + SSD_CONV_DIM:SSD_U_COLS]

    xpad_ref[SUBLANES:SUBLANES + L, :] = xbc
    conv = cb_ref[...]
    for j in range(SSD_CONV):
        conv = conv + cw_ref[j:j + 1, :] * xpad_ref[pl.ds(SUBLANES - (SSD_CONV - 1) + j, L), :]
    xpad_ref[0:SUBLANES, :] = xbc[L - SUBLANES:L, :]
    act = _silu(conv)
    xs = act[:, 0:SSD_D_INNER]
    bm = act[:, SSD_D_INNER:SSD_D_INNER + _SSD_GN]
    cm = act[:, SSD_D_INNER + _SSD_GN:SSD_CONV_DIM]

    dt = _softplus(dt_raw + dtb_ref[...])
    a_neg = -jnp.exp(alog_ref[...])
    tril = tril_ref[...]
    a_cum = jnp.dot(tril, dt * a_neg, precision=HIGHEST, preferred_element_type=F32)
    a_cum_t = a_cum.T
    dt_t = dt.T
    lower = tril > 0.5

    for g in range(SSD_GROUPS):
        bg = bm[:, g * SSD_STATE:(g + 1) * SSD_STATE]
        cg = cm[:, g * SSD_STATE:(g + 1) * SSD_STATE]
        bg16 = bg.astype(BF16)
        cg16 = cg.astype(BF16)
        cb = _dot_nt(cg16, bg16)
        bgt16 = bg.T.astype(BF16)
        for r in range(R):
            h = g * R + r
            a_col = a_cum[:, h:h + 1]
            a_row = a_cum_t[h:h + 1, :]
            a_last = a_cum[L - 1:L, h:h + 1]
            xh = xs[:, h * P:(h + 1) * P]
            decay = jnp.where(lower, jnp.exp(jnp.where(lower, a_col - a_row, 0.0)), 0.0)
            w_ls = cb * decay * dt_t[h:h + 1, :]
            y_diag = _dot(w_ls.astype(BF16), xh.astype(BF16))
            xw = xh * (jnp.exp(a_last - a_col) * dt[:, h:h + 1])
            st = _dot(bgt16, xw.astype(BF16))
            prev = state_ref[h]
            y_off = _dot(cg16, prev.astype(BF16)) * jnp.exp(a_col)
            state_ref[h] = prev * jnp.exp(a_last) + st
            y_ref[:, h * P:(h + 1) * P] = y_diag + y_off + dsk_ref[:, h:h + 1] * xh

    yg = y_ref[...] * _silu(z)
    gw = SSD_D_INNER // SSD_GROUPS
    for g in range(SSD_GROUPS):
        part = yg[:, g * gw:(g + 1) * gw]
        ms = jnp.mean(part * part, axis=-1, keepdims=True)
        o_ref[0, :, g * gw:(g + 1) * gw] = part * lax.rsqrt(ms + NORM_EPS) * nw_ref[:, g * gw:(g + 1) * gw]


def _lane_row(v):
    return jnp.pad(v.astype(F32), (0, LANES - v.shape[0])).reshape(1, LANES)


def ssd_mixer(u, conv_w, conv_b, dt_bias, a_log, d_skip, norm_w):
    b, s, _ = u.shape
    L = SSD_CHUNK
    tril = jnp.asarray(np.tril(np.ones((L, L), np.float32)))
    full = lambda shape: pl.BlockSpec(shape, lambda i, c: (0,) * len(shape))
    return pl.pallas_call(
        _ssd_kernel,
        name="ssd",
        grid=(b, s // L),
        in_specs=[
            pl.BlockSpec((1, L, SSD_U_COLS), lambda i, c: (i, c, 0)),
            full((SSD_CONV, SSD_CONV_DIM)), full((1, SSD_CONV_DIM)),
            full((1, LANES)), full((1, LANES)), full((1, LANES)),
            full((1, SSD_D_INNER)), full((L, L)),
        ],
        out_specs=pl.BlockSpec((1, L, SSD_D_INNER), lambda i, c: (i, c, 0)),
        out_shape=jax.ShapeDtypeStruct((b, s, SSD_D_INNER), F32),
        scratch_shapes=[pltpu.VMEM((SUBLANES + L, SSD_CONV_DIM), F32),
                        pltpu.VMEM((SSD_HEADS, SSD_STATE, HEAD_DIM), F32),
                        pltpu.VMEM((L, SSD_D_INNER), F32)],
        compiler_params=_cparams(("parallel", "arbitrary")),
    )(u, conv_w, conv_b.reshape(1, -1), _lane_row(dt_bias), _lane_row(a_log), _lane_row(d_skip),
      norm_w.reshape(1, -1), tril)


ATTN_TQ = 128
SWA_TQ = 256
_SCALE = HEAD_DIM ** -0.5


def _stack_heads(q, g, heads_per_group):
    parts = [q[:, (g * heads_per_group + r) * HEAD_DIM:(g * heads_per_group + r + 1) * HEAD_DIM]
             for r in range(heads_per_group)]
    return jnp.concatenate(parts, axis=0)


def _row_pos(t0, tq, reps):
    row = lax.broadcasted_iota(jnp.int32, (reps * tq, 1), 0)
    return t0 + (row & (tq - 1))


def _rope(x, cosf, sinf):
    n = x.shape[1]
    half = HEAD_DIM // 2
    lane = lax.broadcasted_iota(jnp.int32, x.shape, 1)
    first = (lane & (HEAD_DIM - 1)) < half
    rot = jnp.where(first, pltpu.roll(x, n - half, 1), pltpu.roll(x, half, 1))
    return x * cosf + rot * sinf


def _rope_table_kernel(pos_ref, cos_ref, sin_ref):
    half = HEAD_DIM // 2
    lane = lax.broadcasted_iota(jnp.int32, (1, LANES), 1)
    expo = -(lane & (half - 1)).astype(F32) / half
    inv_freq = jnp.power(jnp.full((1, LANES), ROPE_THETA, F32), expo)
    sign = jnp.where((lane & (HEAD_DIM - 1)) < half, -1.0, 1.0)
    ang = pos_ref[0] * inv_freq
    cos_ref[0] = jnp.cos(ang)
    sin_ref[0] = jnp.sin(ang) * sign


def rope_tables(positions):
    b, s = positions.shape
    tm = _row_tile(s, 512)
    out = pl.BlockSpec((1, tm, LANES), lambda i, j: (i, j, 0))
    return pl.pallas_call(
        _rope_table_kernel,
        name="rope_tables",
        grid=(b, s // tm),
        in_specs=[pl.BlockSpec((1, tm, 1), lambda i, j: (i, j, 0))],
        out_specs=[out, out],
        out_shape=[jax.ShapeDtypeStruct((b, s, LANES), F32)] * 2,
        compiler_params=_cparams(("parallel", "parallel")),
    )(positions.astype(F32).reshape(b, s, 1))


def _swa_kernel(q_ref, kv_ref, cosq_ref, sinq_ref, cosk_ref, sink_ref, sinks_ref, o_ref):
    tq = q_ref.shape[1]
    W = SWA_WINDOW
    span = W + tq
    R = SWA_HEADS // SWA_KV_HEADS
    kw = SWA_KV_HEADS * HEAD_DIM
    t0 = pl.program_id(1) * tq
    start = pl.multiple_of(jnp.maximum(t0 - W, 0), SUBLANES)

    reps = q_ref.shape[2] // LANES
    qr = _rope(q_ref[0], jnp.concatenate([cosq_ref[0]] * reps, 1), jnp.concatenate([sinq_ref[0]] * reps, 1))
    qr = qr * _SCALE
    kvs = kv_ref[0, pl.ds(start, span), :]
    kr = _rope(kvs[:, 0:kw], cosk_ref[0, pl.ds(start, span), :], sink_ref[0, pl.ds(start, span), :])
    v = kvs[:, kw:2 * kw]

    rel = (t0 + lax.broadcasted_iota(jnp.int32, (1, tq), 1)) - (start + lax.broadcasted_iota(jnp.int32, (span, 1), 0))
    bias = jnp.where((rel >= 0) & (rel < W), 0.0, NEG)
    parts = []
    for g in range(SWA_KV_HEADS):
        cols = slice(g * HEAD_DIM, (g + 1) * HEAD_DIM)
        qg = _stack_heads(qr, g, R).astype(BF16)
        s = _dot_nt(kr[:, cols].astype(BF16), qg) + jnp.concatenate([bias] * R, axis=1)
        snk = jnp.concatenate([jnp.full((1, tq), sinks_ref[g * R + r], F32) for r in range(R)], axis=1)
        m = jnp.maximum(jnp.max(s, axis=0, keepdims=True), snk)
        e = jnp.exp((s - m).astype(BF16))
        v1 = jnp.concatenate([v[:, cols].astype(BF16), jnp.ones((span, HEAD_DIM), BF16)], axis=1)
        o_t = lax.dot_general(v1, e, (((0,), (0,)), ((), ())), preferred_element_type=F32)
        den = o_t[HEAD_DIM:HEAD_DIM + 1] + jnp.exp(snk - m)
        parts.append(o_t[0:HEAD_DIM] / den)
    o = jnp.concatenate(parts, axis=0).T
    for g in range(SWA_KV_HEADS):
        for r in range(R):
            h = g * R + r
            o_ref[0, :, h * HEAD_DIM:(h + 1) * HEAD_DIM] = o[r * tq:(r + 1) * tq, g * HEAD_DIM:(g + 1) * HEAD_DIM]


def swa_mixer(u, rope_cos, rope_sin, sinks):
    b, s, _ = u.shape
    tq = SWA_TQ
    qw = SWA_HEADS * HEAD_DIM
    kvw = 2 * SWA_KV_HEADS * HEAD_DIM
    tile = pl.BlockSpec((1, tq, LANES), lambda i, j: (i, j, 0))
    full = pl.BlockSpec((1, s, LANES), lambda i, j: (i, 0, 0))
    return pl.pallas_call(
        _swa_kernel,
        name="swa",
        grid=(b, s // tq),
        in_specs=[
            pl.BlockSpec((1, tq, qw), lambda i, j: (i, j, 0)),
            pl.BlockSpec((1, s, kvw), lambda i, j: (i, 0, qw // kvw)),
            tile, tile, full, full,
            pl.BlockSpec(memory_space=pltpu.SMEM),
        ],
        out_specs=pl.BlockSpec((1, tq, qw), lambda i, j: (i, j, 0)),
        out_shape=jax.ShapeDtypeStruct((b, s, qw), F32),
        compiler_params=_cparams(("parallel", "arbitrary")),
    )(u, u, rope_cos, rope_sin, rope_cos, rope_sin, sinks.astype(F32))


NSA_U_COLS = NSA_HEADS * HEAD_DIM + 6 * NSA_KV_HEADS * HEAD_DIM + LANES
_NSA_KVW = NSA_KV_HEADS * HEAD_DIM
_NSA_QBLK = NSA_HEADS * HEAD_DIM // _NSA_KVW
_NSA_R = NSA_HEADS // NSA_KV_HEADS
_CMP_HALF = NSA_CMP_BLOCK // 2
NSA_KEY_CHUNK = 256
NSA_SELECT_TQ = 256


def _nsa_compress_kernel(k_ref, v_ref, pe_ref, w1_ref, w2_ref, kc_ref, vc_ref):
    nc = kc_ref.shape[1]
    for idx, (x_ref, o_ref) in enumerate(((k_ref, kc_ref), (v_ref, vc_ref))):
        ha = jnp.zeros((nc, _NSA_KVW), F32)
        hb = jnp.zeros((nc, _NSA_KVW), F32)
        for l in range(_CMP_HALF):
            y = x_ref[0, pl.ds(l, nc, stride=NSA_CMP_STRIDE), :]
            ha = ha + _dot((y + pe_ref[idx, l:l + 1, :]).astype(BF16), w1_ref[idx, l])
            hb = hb + _dot((y + pe_ref[idx, _CMP_HALF + l:_CMP_HALF + l + 1, :]).astype(BF16),
                           w1_ref[idx, _CMP_HALF + l])
        hid = _silu(ha + pltpu.roll(hb, nc - 1, 0))
        o_ref[0] = _dot(hid.astype(BF16), w2_ref[idx])


def _nsa_select_kernel(q_ref, kc_ref, vc_ref, ov_ref, ocmp_ref, sel_ref):
    tq = q_ref.shape[1]
    nc = kc_ref.shape[1]
    nsel = ov_ref.shape[0]
    k_eff = min(NSA_TOPK, nsel)
    t0 = pl.program_id(1) * tq
    t_row = t0 + lax.broadcasted_iota(jnp.int32, (1, tq), 1)
    cmp_end = lax.broadcasted_iota(jnp.int32, (nc, 1), 0) * NSA_CMP_STRIDE + (NSA_CMP_BLOCK - 1)
    cvalid = jnp.concatenate([cmp_end <= t_row] * _NSA_R, axis=1)
    jidx = lax.broadcasted_iota(jnp.int32, (nsel, 1), 0)
    blk_t = t_row // NSA_SEL_BLOCK
    jvalid = jidx <= blk_t
    forced = (jidx == 0) | (jidx == blk_t)
    q = q_ref[0] * _SCALE
    sel_rows = []
    o_parts = []
    for g in range(NSA_KV_HEADS):
        qg = _stack_heads(q, g, _NSA_R).astype(BF16)
        kc = kc_ref[0, :, g * HEAD_DIM:(g + 1) * HEAD_DIM].astype(BF16)
        vc = vc_ref[0, :, g * HEAD_DIM:(g + 1) * HEAD_DIM].astype(BF16)
        s = _dot_nt(kc, qg)
        m = jnp.max(jnp.where(cvalid, s, NEG), axis=0, keepdims=True)
        e = jnp.exp(jnp.where(cvalid, s - m, NEG))
        den = jnp.sum(e, axis=0, keepdims=True)
        p = e / jnp.where(den > 0.0, den, 1.0)
        o_parts.append(lax.dot_general(vc, p.astype(BF16), (((0,), (0,)), ((), ())),
                                       preferred_element_type=F32))
        psum = p[:, 0:tq]
        for r in range(1, _NSA_R):
            psum = psum + p[:, r * tq:(r + 1) * tq]
        imp_t = jnp.dot(ov_ref[...], psum, precision=HIGHEST, preferred_element_type=F32)
        score = jnp.where(forced, BIG, jnp.where(jvalid, imp_t, NEG))
        cnt = jnp.zeros((nsel, tq), F32)
        for i in range(nsel):
            si = score[i:i + 1, :]
            beats = (si > score) | ((si == score) & (jidx > i))
            cnt = cnt + jnp.where(beats, 1.0, 0.0)
        sel_rows.append(jnp.where(cnt < k_eff, 1.0, 0.0))
    pad = LANES - NSA_KV_HEADS * nsel
    sel_t = jnp.concatenate(sel_rows + [jnp.zeros((pad, tq), F32)], axis=0)
    sel_ref[0] = sel_t.T
    o = jnp.concatenate(o_parts, axis=0).T
    for g in range(NSA_KV_HEADS):
        for r in range(_NSA_R):
            h = g * _NSA_R + r
            ocmp_ref[0, :, h * HEAD_DIM:(h + 1) * HEAD_DIM] = o[r * tq:(r + 1) * tq, g * HEAD_DIM:(g + 1) * HEAD_DIM]


def _nsa_attend_kernel(q_ref, sel_ref, gate_ref, ocmp_ref, ks_ref, vs_ref, kw_ref, vw_ref, ext_ref, o_ref):
    tq = q_ref.shape[1]
    s_len = ks_ref.shape[1]
    W = NSA_WINDOW
    span = min(W + tq, s_len)
    t0 = pl.program_id(1) * tq
    start = pl.multiple_of(jnp.maximum(t0 - W, 0), SUBLANES)
    t_row = t0 + lax.broadcasted_iota(jnp.int32, (1, tq), 1)
    rel = t_row - (start + lax.broadcasted_iota(jnp.int32, (span, 1), 0))
    win_bias = jnp.where((rel >= 0) & (rel < W), 0.0, NEG)
    q = q_ref[0] * _SCALE
    sig = _sigmoid(gate_ref[0])
    sel16 = sel_ref[0].astype(BF16)

    def attend_t(qg, k, v, bias_t):
        s = _dot_nt(k.astype(BF16), qg) + jnp.concatenate([bias_t] * _NSA_R, axis=1)
        e = jnp.exp((s - jnp.max(s, axis=0, keepdims=True)).astype(BF16))
        v1 = jnp.concatenate([v.astype(BF16), jnp.ones(v.shape, BF16)], axis=1)
        o_t = lax.dot_general(v1, e, (((0,), (0,)), ((), ())), preferred_element_type=F32)
        return o_t[0:HEAD_DIM] / o_t[HEAD_DIM:HEAD_DIM + 1]

    chunk = min(NSA_KEY_CHUNK, s_len)
    for c in range(s_len // chunk):
        klen = (c + 1) * chunk

        @pl.when(t0 // chunk == c)
        def _():
            kidx = lax.broadcasted_iota(jnp.int32, (klen, 1), 0)
            sel_t, win_t = [], []
            for g in range(NSA_KV_HEADS):
                cols = slice(g * HEAD_DIM, (g + 1) * HEAD_DIM)
                qg = _stack_heads(q, g, _NSA_R).astype(BF16)
                chosen = _dot_nt(ext_ref[g, 0:klen, :], sel16) > 0.5
                sel_bias = jnp.where(chosen & (kidx <= t_row), 0.0, NEG)
                sel_t.append(attend_t(qg, ks_ref[0, 0:klen, cols], vs_ref[0, 0:klen, cols], sel_bias))
                win_t.append(attend_t(qg, kw_ref[0, pl.ds(start, span), cols],
                                      vw_ref[0, pl.ds(start, span), cols], win_bias))
            o_sel = jnp.concatenate(sel_t, axis=0).T
            o_win = jnp.concatenate(win_t, axis=0).T
            for g in range(NSA_KV_HEADS):
                for r in range(_NSA_R):
                    h = g * _NSA_R + r
                    hc = slice(h * HEAD_DIM, (h + 1) * HEAD_DIM)
                    rows = slice(r * tq, (r + 1) * tq)
                    gc = slice(g * HEAD_DIM, (g + 1) * HEAD_DIM)
                    o_ref[0, :, hc] = (sig[:, h:h + 1] * ocmp_ref[0, :, hc]
                                       + sig[:, NSA_HEADS + h:NSA_HEADS + h + 1] * o_sel[rows, gc]
                                       + sig[:, 2 * NSA_HEADS + h:2 * NSA_HEADS + h + 1] * o_win[rows, gc])


def _block_diag2(w):
    z = jnp.zeros_like(w)
    return jnp.concatenate([jnp.concatenate([w, z], -1), jnp.concatenate([z, w], -1)], -2)


def nsa_mixer(u, cmp_pe, cmp_w1, cmp_w2):
    b, s, _ = u.shape
    tq = ATTN_TQ
    nc = s // NSA_CMP_STRIDE
    nsel = s // NSA_SEL_BLOCK
    qw = NSA_HEADS * HEAD_DIM
    pe2 = jnp.concatenate([cmp_pe, cmp_pe], -1)
    w1bd = _block_diag2(cmp_w1).astype(BF16)
    w2bd = _block_diag2(cmp_w2).astype(BF16)
    const = lambda shape: pl.BlockSpec(shape, lambda *_: (0,) * len(shape))
    col = lambda c: pl.BlockSpec((1, s, _NSA_KVW), lambda i, *_: (i, 0, _NSA_QBLK + c))

    kc, vc = pl.pallas_call(
        _nsa_compress_kernel,
        name="nsa_compress",
        grid=(b,),
        in_specs=[col(0), col(1), const(pe2.shape), const(w1bd.shape), const(w2bd.shape)],
        out_specs=[pl.BlockSpec((1, nc, _NSA_KVW), lambda i: (i, 0, 0))] * 2,
        out_shape=[jax.ShapeDtypeStruct((b, nc, _NSA_KVW), F32)] * 2,
        compiler_params=_cparams(("parallel",)),
    )(u, u, pe2, w1bd, w2bd)

    c_start = np.arange(nc) * NSA_CMP_STRIDE
    s_start = np.arange(nsel) * NSA_SEL_BLOCK
    ov_t = ((c_start[None, :] < s_start[:, None] + NSA_SEL_BLOCK)
            & (c_start[None, :] + NSA_CMP_BLOCK > s_start[:, None])
            & (np.arange(nc)[None, :] < nc - 1)).astype(np.float32)
    qspec = pl.BlockSpec((1, tq, qw), lambda i, j: (i, j, 0))
    ts = _row_tile(s, NSA_SELECT_TQ)
    sel_q = pl.BlockSpec((1, ts, qw), lambda i, j: (i, j, 0))
    o_cmp, sel = pl.pallas_call(
        _nsa_select_kernel,
        name="nsa_select",
        grid=(b, s // ts),
        in_specs=[sel_q,
                  pl.BlockSpec((1, nc, _NSA_KVW), lambda i, j: (i, 0, 0)),
                  pl.BlockSpec((1, nc, _NSA_KVW), lambda i, j: (i, 0, 0)),
                  const(ov_t.shape)],
        out_specs=[sel_q, pl.BlockSpec((1, ts, LANES), lambda i, j: (i, j, 0))],
        out_shape=[jax.ShapeDtypeStruct((b, s, qw), F32), jax.ShapeDtypeStruct((b, s, LANES), F32)],
        compiler_params=_cparams(("parallel", "arbitrary")),
    )(u, kc, vc, jnp.asarray(ov_t))

    expand = np.zeros((NSA_KV_HEADS, s, LANES), np.float32)
    for g in range(NSA_KV_HEADS):
        expand[g, np.arange(s), g * nsel + np.arange(s) // NSA_SEL_BLOCK] = 1.0
    gate_blk = NSA_U_COLS // LANES - 1
    return pl.pallas_call(
        _nsa_attend_kernel,
        name="nsa_attend",
        grid=(b, s // tq),
        in_specs=[qspec,
                  pl.BlockSpec((1, tq, LANES), lambda i, j: (i, j, 0)),
                  pl.BlockSpec((1, tq, LANES), lambda i, j: (i, j, gate_blk)),
                  qspec,
                  col(2), col(3), col(4), col(5),
                  const(expand.shape)],
        out_specs=qspec,
        out_shape=jax.ShapeDtypeStruct((b, s, qw), F32),
        compiler_params=_cparams(("parallel", "arbitrary")),
    )(u, sel, u, o_cmp, u, u, u, u, jnp.asarray(expand, BF16))


def _seg_sum(x, ones16):
    hi = x.astype(BF16)
    lo = (x - hi.astype(F32)).astype(BF16)
    return _dot(hi, ones16) + _dot(lo, ones16)


def _head_ones(width):
    idx = np.arange(width) // HEAD_DIM
    return jnp.asarray((idx[:, None] == idx[None, :]).astype(np.float32), BF16)


_RW_R, _RW_K, _RW_V = 0, RWKV_DIM, 2 * RWKV_DIM
_RW_WD = 3 * RWKV_DIM
_RW_AD = _RW_WD + RWKV_W_LORA
_RW_GD = _RW_AD + RWKV_A_LORA


def _rwkv_prep_kernel(u_ref, mu_ref, w0_ref, wup_ref, a0_ref, aup_ref, gup_ref, kk_ref, ka_ref, rk_ref,
                      ones_ref, r_o, w_o, k_o, v_o, kk_o, q_o, g_o, bonus_o, up_ref):
    tm = u_ref.shape[1]

    @pl.when(pl.program_id(1) == 0)
    def _():
        up_ref[0:SUBLANES, :] = jnp.zeros((SUBLANES, RWKV_IN), F32)

    u = u_ref[0]
    up_ref[SUBLANES:SUBLANES + tm, :] = u
    prev = up_ref[pl.ds(SUBLANES - 1, tm), :]
    up_ref[0:SUBLANES, :] = u[tm - SUBLANES:tm, :]
    x = u + (prev - u) * mu_ref[...]
    r = x[:, _RW_R:_RW_R + RWKV_DIM]
    k = x[:, _RW_K:_RW_K + RWKV_DIM]
    v = x[:, _RW_V:_RW_V + RWKV_DIM]
    wd = x[:, _RW_WD:_RW_AD]
    ad = x[:, _RW_AD:_RW_GD]
    gd = x[:, _RW_GD:RWKV_IN]
    w = -_softplus(-(w0_ref[...] + _dot(jnp.tanh(wd).astype(BF16), wup_ref[...]))) - 0.5
    a = _sigmoid(a0_ref[...] + _dot(ad.astype(BF16), aup_ref[...]))
    ones16 = ones_ref[...]
    kk = k * kk_ref[...]
    kk = kk / jnp.maximum(jnp.sqrt(_seg_sum(kk * kk, ones16)), 1e-12)
    k2 = k * (1.0 + (a - 1.0) * ka_ref[...])
    r_o[0] = r
    w_o[0] = jnp.exp(-jnp.exp(w))
    k_o[0] = k2
    v_o[0] = v
    kk_o[0] = kk
    q_o[0] = kk * a
    g_o[0] = _dot(_sigmoid(gd).astype(BF16), gup_ref[...])
    bonus_o[0] = _seg_sum(r * k2 * rk_ref[...], ones16) * v


_RWS_TB = 64
_RWS_VB = HEAD_DIM // 2 // SUBLANES
_RWS_NACC = 2
_RWS_VECS = 5


def _rwkv_scan_vhalf_kernel(kk_ref, w_ref, q_ref, k_ref, r_ref, v_ref, o_ref, s_ref, rows_ref):
    @pl.when(pl.program_id(0) == 0)
    def _():
        s_ref[...] = jnp.zeros_like(s_ref)

    low = lax.broadcasted_iota(jnp.int32, (SUBLANES, LANES), 1) < LANES // 2

    def reduce_acc(acc, vb):
        tot = acc[(vb, 0)]
        for i in range(1, _RWS_NACC):
            tot = tot + acc[(vb, i)]
        return tot

    def group(tg, carry):
        t8 = pl.multiple_of(tg * SUBLANES, SUBLANES)
        for a, ref in enumerate((kk_ref, w_ref, q_ref, k_ref, r_ref)):
            for i in range(HEAD_DIM // 2):
                x = ref[i, pl.ds(t8, SUBLANES), :]
                xr = pltpu.roll(x, LANES // 2, 1)
                rows_ref[a, 0, i] = jnp.where(low, x, xr)
                rows_ref[a, 1, i] = jnp.where(low, xr, x)
        for tl in range(SUBLANES):
            t = t8 + tl
            key_row = lambda a, k: rows_ref[a, k % 2, k // 2, tl:tl + 1, :]
            acc = {}
            for k in range(HEAD_DIM):
                kk_row = key_row(0, k)
                for vb in range(_RWS_VB):
                    term = s_ref[vb, k] * kk_row
                    key = (vb, k % _RWS_NACC)
                    acc[key] = acc[key] + term if key in acc else term
            sa = [reduce_acc(acc, vb) for vb in range(_RWS_VB)]
            vt = [v_ref[t, vb * SUBLANES:(vb + 1) * SUBLANES, :] for vb in range(_RWS_VB)]
            acc = {}
            for k in range(HEAD_DIM):
                w_row, q_row, k_row, r_row = (key_row(a, k) for a in range(1, _RWS_VECS))
                for vb in range(_RWS_VB):
                    st = s_ref[vb, k] * w_row - sa[vb] * q_row + vt[vb] * k_row
                    s_ref[vb, k] = st
                    term = st * r_row
                    key = (vb, k % _RWS_NACC)
                    acc[key] = acc[key] + term if key in acc else term
            for vb in range(_RWS_VB):
                o_ref[t, vb * SUBLANES:(vb + 1) * SUBLANES, :] = reduce_acc(acc, vb)
        return carry

    lax.fori_loop(0, o_ref.shape[0] // SUBLANES, group, 0)


_RL_TB = 128
_RL_ROWS = HEAD_DIM // 2


def _scan_row_base(i, half, interleaved):
    return 2 * i + half if interleaved else half * _RL_ROWS + i


def _to_scan_kernel(*refs, interleaved):
    n = len(interleaved)
    ins, outs, y_ref = refs[:n], refs[n:2 * n], refs[2 * n]
    nb, tb = ins[0].shape[0], ins[0].shape[1]
    for a in range(n):
        for b in range(nb):
            y_ref[b] = ins[a][b].T
        for i in range(_RL_ROWS):
            parts = [y_ref[b, pl.ds(_scan_row_base(i, half, interleaved[a]), RWKV_HEADS, stride=HEAD_DIM), :]
                     for half in range(2) for b in range(nb)]
            rows = jnp.concatenate(parts, axis=0).T
            if interleaved[a]:
                outs[a][i] = rows
            else:
                outs[a][pl.ds(i, tb, stride=_RL_ROWS), :] = rows


def _from_scan_kernel(o_ref, bonus_ref, g_ref, lnw_ref, lnb_ref, out_ref, y_ref):
    nb, tb = out_ref.shape[0], out_ref.shape[1]
    for i in range(_RL_ROWS):
        zt = o_ref[pl.ds(i, tb, stride=_RL_ROWS), :].T
        for half in range(2):
            for b in range(nb):
                r0 = (half * nb + b) * RWKV_HEADS
                y_ref[b, pl.ds(_scan_row_base(i, half, False), RWKV_HEADS, stride=HEAD_DIM), :] = zt[r0:r0 + RWKV_HEADS]
    for b in range(nb):
        for h in range(RWKV_HEADS):
            rows = slice(h * HEAD_DIM, (h + 1) * HEAD_DIM)
            blk = y_ref[b, rows, :]
            cen = blk - jnp.mean(blk, axis=0, keepdims=True)
            var = jnp.mean(cen * cen, axis=0, keepdims=True)
            y_ref[b, rows, :] = cen * lax.rsqrt(var + RWKV_GN_EPS)
        o = y_ref[b].T * lnw_ref[...] + lnb_ref[...]
        out_ref[b] = (o + bonus_ref[b]) * g_ref[b]


def to_scan_layout(arrays, interleaved):
    b, s, dm = arrays[0].shape
    tb = _row_tile(s, _RL_TB)
    n = len(arrays)
    key_spec = pl.BlockSpec((_RL_ROWS, tb, LANES), lambda i: (0, i, 0))
    val_spec = pl.BlockSpec((tb * _RL_ROWS, LANES), lambda i: (i, 0))
    key_shape = jax.ShapeDtypeStruct((_RL_ROWS, s, LANES), F32)
    val_shape = jax.ShapeDtypeStruct((s * _RL_ROWS, LANES), F32)
    outs = pl.pallas_call(
        functools.partial(_to_scan_kernel, interleaved=tuple(interleaved)),
        name="rwkv_to_scan",
        grid=(s // tb,),
        in_specs=[pl.BlockSpec((b, tb, dm), lambda i: (0, i, 0))] * n,
        out_specs=[key_spec if f else val_spec for f in interleaved],
        out_shape=[key_shape if f else val_shape for f in interleaved],
        scratch_shapes=[pltpu.VMEM((b, dm, tb), F32)],
        compiler_params=_cparams(("parallel",)),
    )(*arrays)
    return [o if f else o.reshape(s, _RL_ROWS, LANES) for o, f in zip(outs, interleaved)]


def from_scan_layout(o, bonus, g, ln_w, ln_b):
    s = o.shape[0]
    b, _, dm = bonus.shape
    tb = _row_tile(s, _RL_TB)
    tile = pl.BlockSpec((b, tb, dm), lambda i: (0, i, 0))
    vec = pl.BlockSpec((1, dm), lambda i: (0, 0))
    return pl.pallas_call(
        _from_scan_kernel,
        name="rwkv_from_scan",
        grid=(s // tb,),
        in_specs=[pl.BlockSpec((tb * _RL_ROWS, LANES), lambda i: (i, 0)), tile, tile, vec, vec],
        out_specs=tile,
        out_shape=jax.ShapeDtypeStruct((b, s, dm), F32),
        scratch_shapes=[pltpu.VMEM((b, dm, tb), F32)],
        compiler_params=_cparams(("parallel",)),
    )(o.reshape(s * _RL_ROWS, LANES), bonus, g, ln_w.astype(F32).reshape(1, dm), ln_b.astype(F32).reshape(1, dm))


def rwkv7_mixer(u, mu, w0, w_up, a0, a_up, g_up, k_k, k_a, r_k, ln_w, ln_b):
    b, s, _ = u.shape
    dm = RWKV_DIM
    tm = _row_tile(s, 256)
    row = lambda v: v.astype(F32).reshape(1, -1)
    const = lambda shape: pl.BlockSpec(shape, lambda *_: (0,) * len(shape))
    ones_d = _head_ones(dm)
    tile = pl.BlockSpec((1, tm, dm), lambda i, j: (i, j, 0))
    outs = pl.pallas_call(
        _rwkv_prep_kernel,
        name="rwkv_prep",
        grid=(b, s // tm),
        in_specs=[pl.BlockSpec((1, tm, RWKV_IN), lambda i, j: (i, j, 0)),
                  const((1, RWKV_IN)), const((1, dm)), const((RWKV_W_LORA, dm)), const((1, dm)),
                  const((RWKV_A_LORA, dm)), const((RWKV_G_LORA, dm)), const((1, dm)), const((1, dm)),
                  const((1, dm)), const((dm, dm))],
        out_specs=[tile] * 8,
        out_shape=[jax.ShapeDtypeStruct((b, s, dm), F32)] * 8,
        scratch_shapes=[pltpu.VMEM((SUBLANES + tm, RWKV_IN), F32)],
        compiler_params=_cparams(("parallel", "arbitrary")),
    )(u, row(mu), row(w0), w_up.astype(BF16), row(a0), a_up.astype(BF16), g_up.astype(BF16),
      row(k_k), row(k_a), row(r_k), ones_d)
    r, wdec, k2, v, kk, q, g, bonus = outs

    nchain = b * RWKV_HEADS
    assert 2 * nchain == LANES and s % 2 == 0, "the scan kernel maps (half, batch, head) onto the 128 lanes"
    scan_in = to_scan_layout([kk, wdec, q, k2, r, v], interleaved=[True] * _RWS_VECS + [False])
    tb = _row_tile(s, _RWS_TB)
    kblk = pl.BlockSpec((_RL_ROWS, tb, LANES), lambda i: (0, i, 0))
    vblk = pl.BlockSpec((tb, _RL_ROWS, LANES), lambda i: (i, 0, 0))
    o = pl.pallas_call(
        _rwkv_scan_vhalf_kernel,
        name="rwkv_scan",
        grid=(s // tb,),
        in_specs=[kblk] * _RWS_VECS + [vblk],
        out_specs=vblk,
        out_shape=jax.ShapeDtypeStruct((s, _RL_ROWS, LANES), F32),
        scratch_shapes=[pltpu.VMEM((_RWS_VB, HEAD_DIM, SUBLANES, LANES), F32),
                        pltpu.VMEM((_RWS_VECS, 2, _RL_ROWS, SUBLANES, LANES), F32)],
        compiler_params=_cparams(("arbitrary",)),
    )(*scan_in)
    return from_scan_layout(o, bonus, g, ln_w, ln_b)


MOE_SLOT_BLOCK = 512
MOE_FF_CHUNK = 256
MOE_ROW_TILE = 256
MOE_DMA_UNROLL = 8
_META_E0, _META_E1, _META_R0, _META_R1, _META_W0, _META_W1 = range(6)


def _moe_route_kernel(x_ref, g_ref, wr_ref, tril_ref, h_o, meta_o, cnt_o, carry_ref):
    @pl.when(pl.program_id(0) == 0)
    def _():
        carry_ref[...] = jnp.zeros_like(carry_ref)

    x = x_ref[...]
    ms = jnp.mean(x * x, axis=-1, keepdims=True)
    h = x * lax.rsqrt(ms + NORM_EPS) * g_ref[...]
    h_o[...] = h
    logits = jnp.dot(h, wr_ref[...], precision=HIGHEST, preferred_element_type=F32)
    lane = lax.broadcasted_iota(jnp.int32, logits.shape, 1)
    l1 = jnp.where(lane < N_EXPERTS, logits, NEG)
    m1 = jnp.max(l1, axis=-1, keepdims=True)
    i1 = jnp.min(jnp.where(l1 == m1, lane, LANES), axis=-1, keepdims=True)
    l2 = jnp.where(lane == i1, NEG, l1)
    m2 = jnp.max(l2, axis=-1, keepdims=True)
    i2 = jnp.min(jnp.where(l2 == m2, lane, LANES), axis=-1, keepdims=True)
    e21 = jnp.exp(m2 - m1)
    w1 = 1.0 / (1.0 + e21)
    w2 = e21 * w1
    cnt = jnp.where((lane == i1) | (lane == i2), 1.0, 0.0)
    before = _dot(tril_ref[...], cnt.astype(BF16)) + carry_ref[...]
    r1 = jnp.sum(jnp.where(lane == i1, before, 0.0), axis=-1, keepdims=True)
    r2 = jnp.sum(jnp.where(lane == i2, before, 0.0), axis=-1, keepdims=True)
    carry_ref[...] += jnp.sum(cnt, axis=0, keepdims=True)
    cnt_o[...] = carry_ref[...]
    meta = jnp.zeros(logits.shape, F32)
    for idx, val in ((_META_E0, i1.astype(F32)), (_META_E1, i2.astype(F32)), (_META_R0, r1),
                     (_META_R1, r2), (_META_W0, w1), (_META_W1, w2)):
        meta = jnp.where(lane == idx, val, meta)
    meta_o[...] = meta


def _row_copy(src_ref, src_row, dst_ref, dst_row, sem):
    return pltpu.make_async_copy(src_ref.at[pl.ds(src_row, 1), :], dst_ref.at[pl.ds(dst_row, 1), :], sem)


def _moe_scatter_kernel(d0_ref, d1_ref, h_ref, xs_in_ref, xs_ref, sem):
    del xs_in_ref
    tm = h_ref.shape[0]

    def issue(r, c):
        _row_copy(h_ref, r, xs_ref, d0_ref[r], sem).start(priority=0)
        _row_copy(h_ref, r, xs_ref, d1_ref[r], sem).start(priority=1)
        return c

    def drain(r, c):
        _row_copy(h_ref, r, xs_ref, d0_ref[r], sem).wait()
        _row_copy(h_ref, r, xs_ref, d1_ref[r], sem).wait()
        return c

    lax.fori_loop(0, tm, issue, 0, unroll=MOE_DMA_UNROLL)
    lax.fori_loop(0, tm, drain, 0, unroll=MOE_DMA_UNROLL)


def _moe_expert_kernel(be_ref, nused_ref, xs_ref, wg_ref, wu_ref, wd_ref, ys_ref):
    del be_ref
    i = pl.program_id(0)

    @pl.when(i < nused_ref[0])
    def _():
        x16 = xs_ref[...].astype(BF16)
        acc = jnp.zeros(ys_ref.shape, F32)
        for c in range(D_FF // MOE_FF_CHUNK):
            cols = slice(c * MOE_FF_CHUNK, (c + 1) * MOE_FF_CHUNK)
            a = _silu(_dot(x16, wg_ref[0, :, cols])) * _dot(x16, wu_ref[0, :, cols])
            acc = acc + _dot(a.astype(BF16), wd_ref[0, cols, :])
        ys_ref[...] = acc

    @pl.when(i >= nused_ref[0])
    def _():
        ys_ref[...] = jnp.zeros_like(ys_ref)


def _moe_combine_kernel(d0_ref, d1_ref, x_ref, meta_ref, ys_ref, p_ref, pg_ref, pp_ref, *rest):
    nf_ref = rest[0] if len(rest) == 5 else None
    o_ref, b0_ref, b1_ref, sem = rest[-4:]
    tm = x_ref.shape[0]

    def issue(r, c):
        _row_copy(ys_ref, d0_ref[r], b0_ref, r, sem).start(priority=0)
        _row_copy(ys_ref, d1_ref[r], b1_ref, r, sem).start(priority=1)
        return c

    def drain(r, c):
        _row_copy(ys_ref, d0_ref[r], b0_ref, r, sem).wait()
        _row_copy(ys_ref, d1_ref[r], b1_ref, r, sem).wait()
        return c

    lax.fori_loop(0, tm, issue, 0, unroll=MOE_DMA_UNROLL)
    lax.fori_loop(0, tm, drain, 0, unroll=MOE_DMA_UNROLL)
    w0 = meta_ref[:, _META_W0:_META_W0 + 1]
    w1 = meta_ref[:, _META_W1:_META_W1 + 1]
    y = x_ref[...] + (w0 * b0_ref[...] + w1 * b1_ref[...])
    o_ref[...] = _ple_epilogue(y, p_ref, pg_ref, pp_ref, nf_ref)


def moe_ffn(x, g, router, w_gate, w_up, w_down, p, ple_gate, ple_proj, norm_final=None):
    t, d = x.shape
    tm = _row_tile(t, MOE_ROW_TILE)
    blk = MOE_SLOT_BLOCK
    nblk = -(-(t * TOP_K + N_EXPERTS * (blk - 1)) // blk)
    slots = nblk * blk
    tril = jnp.asarray(np.tril(np.ones((tm, tm), np.float32), -1), BF16)
    const = lambda shape: pl.BlockSpec(shape, lambda *_: (0,) * len(shape))
    row = lambda w: pl.BlockSpec((tm, w), lambda i: (i, 0))
    h, meta, counts = pl.pallas_call(
        _moe_route_kernel,
        name="moe_route",
        grid=(t // tm,),
        in_specs=[row(d), const((1, d)), const((d, LANES)), const((tm, tm))],
        out_specs=[row(d), row(LANES), const((1, LANES))],
        out_shape=[jax.ShapeDtypeStruct((t, d), F32), jax.ShapeDtypeStruct((t, LANES), F32),
                   jax.ShapeDtypeStruct((1, LANES), F32)],
        scratch_shapes=[pltpu.VMEM((1, LANES), F32)],
        compiler_params=_cparams(("arbitrary",)),
    )(x, g.reshape(1, d), _pad_cols(router.astype(F32), LANES), tril)

    cnt = counts[0, :N_EXPERTS].astype(jnp.int32)
    pcnt = (cnt + blk - 1) // blk * blk
    pend = jnp.cumsum(pcnt)
    pstart = pend - pcnt
    e0 = meta[:, _META_E0].astype(jnp.int32)
    e1 = meta[:, _META_E1].astype(jnp.int32)
    d0 = pstart[e0] + meta[:, _META_R0].astype(jnp.int32)
    d1 = pstart[e1] + meta[:, _META_R1].astype(jnp.int32)
    blk_start = jnp.arange(nblk, dtype=jnp.int32) * blk
    blk_e = jnp.minimum(jnp.sum((pend[None, :] <= blk_start[:, None]).astype(jnp.int32), axis=1),
                        N_EXPERTS - 1).astype(jnp.int32)
    nused = (pend[-1:] // blk).astype(jnp.int32)

    smem_rows = pl.BlockSpec((tm,), lambda i: (i,), memory_space=pltpu.SMEM)
    xs = pl.pallas_call(
        _moe_scatter_kernel,
        name="moe_scatter",
        grid=(t // tm,),
        in_specs=[smem_rows, smem_rows, row(d), pl.BlockSpec(memory_space=pl.ANY)],
        out_specs=pl.BlockSpec(memory_space=pl.ANY),
        out_shape=jax.ShapeDtypeStruct((slots, d), F32),
        scratch_shapes=[pltpu.SemaphoreType.DMA(())],
        input_output_aliases={3: 0},
        compiler_params=_cparams(("arbitrary",)),
    )(d0, d1, h, jnp.zeros((slots, d), F32))

    f = w_gate.shape[2]
    ys = pl.pallas_call(
        _moe_expert_kernel,
        name="moe_expert",
        grid_spec=pltpu.PrefetchScalarGridSpec(
            num_scalar_prefetch=2,
            grid=(nblk,),
            in_specs=[pl.BlockSpec((blk, d), lambda i, be, nu: (i, 0)),
                      pl.BlockSpec((1, d, f), lambda i, be, nu: (be[i], 0, 0)),
                      pl.BlockSpec((1, d, f), lambda i, be, nu: (be[i], 0, 0)),
                      pl.BlockSpec((1, f, d), lambda i, be, nu: (be[i], 0, 0))],
            out_specs=pl.BlockSpec((blk, d), lambda i, be, nu: (i, 0)),
        ),
        out_shape=jax.ShapeDtypeStruct((slots, d), F32),
        compiler_params=_cparams(("arbitrary",)),
    )(blk_e, nused, xs, w_gate, w_up, w_down)

    ple_specs, ple_args = _ple_operands(p, ple_gate, ple_proj, norm_final, tm, lambda i: (i, 0))
    return pl.pallas_call(
        _moe_combine_kernel,
        name="moe_combine",
        grid=(t // tm,),
        in_specs=[smem_rows, smem_rows, row(d), row(LANES), pl.BlockSpec(memory_space=pl.ANY)] + ple_specs,
        out_specs=row(d),
        out_shape=jax.ShapeDtypeStruct((t, d), F32),
        scratch_shapes=[pltpu.VMEM((tm, d), F32), pltpu.VMEM((tm, d), F32), pltpu.SemaphoreType.DMA(())],
        compiler_params=_cparams(("arbitrary",)),
    )(d0, d1, x, meta, ys, *ple_args)


_IN_SPLITS = (SSD_D_INNER, SSD_CONV_DIM, SSD_HEADS,
              NSA_HEADS * HEAD_DIM, 6 * NSA_KV_HEADS * HEAD_DIM, 3 * NSA_HEADS,
              RWKV_IN,
              SWA_HEADS * HEAD_DIM, 2 * SWA_KV_HEADS * HEAD_DIM,
              N_BRANCHES * D_MODEL)
_IN_OFF = tuple(int(o) for o in np.cumsum((0,) + _IN_SPLITS))


def _split_in_proj(w):
    o = _IN_OFF
    seg = lambda a, b: w[:, o[a]:o[b]]
    w_ssd = jnp.concatenate([seg(0, 2), _pad_cols(seg(2, 3), LANES)], axis=1)
    w_nsa = jnp.concatenate([seg(3, 5), _pad_cols(seg(5, 6), LANES)], axis=1)
    return tuple(m.astype(BF16) for m in (w_ssd, w_nsa, seg(6, 7), seg(7, 9), seg(9, 10)))


def kernel(x, p, positions, norm_mix, w_in, ssd_conv_w, ssd_conv_b, ssd_dt_bias, ssd_a_log, ssd_d, ssd_norm, nsa_cmp_pe, nsa_cmp_w1, nsa_cmp_w2, rwkv_mu, rwkv_w0, rwkv_w_up, rwkv_a0, rwkv_a_up, rwkv_g_up, rwkv_k_k, rwkv_k_a, rwkv_r_k, rwkv_ln_w, rwkv_ln_b, swa_sinks, w_br_ssd, w_br_nsa, w_br_rwkv, w_br_swa, w_out, norm_ffn, ffn_w_gate, ffn_w_up, ffn_w_down, moe_router, moe_w_gate, moe_w_up, moe_w_down, ple_proj, ple_gate, norm_final):
    b, s, d = x.shape
    t = b * s
    depth = w_in.shape[0]
    xf = x.reshape(t, d)
    rope_cos, rope_sin = rope_tables(positions)
    for i in range(depth):
        w_ssd, w_nsa, w_rwkv, w_swa, w_gates = _split_in_proj(w_in[i])
        g_mix = norm_mix[i]
        u_ssd = norm_matmul(xf, g_mix, w_ssd).reshape(b, s, -1)
        u_nsa = norm_matmul(xf, g_mix, w_nsa).reshape(b, s, -1)
        u_rwkv = norm_matmul(xf, g_mix, w_rwkv).reshape(b, s, -1)
        u_swa = norm_matmul(xf, g_mix, w_swa).reshape(b, s, -1)
        y_ssd = ssd_mixer(u_ssd, ssd_conv_w[i], ssd_conv_b[i], ssd_dt_bias[i], ssd_a_log[i], ssd_d[i], ssd_norm[i])
        y_nsa = nsa_mixer(u_nsa, nsa_cmp_pe[i], nsa_cmp_w1[i], nsa_cmp_w2[i])
        y_rwkv = rwkv7_mixer(u_rwkv, rwkv_mu[i], rwkv_w0[i], rwkv_w_up[i], rwkv_a0[i], rwkv_a_up[i],
                             rwkv_g_up[i], rwkv_k_k[i], rwkv_k_a[i], rwkv_r_k[i], rwkv_ln_w[i], rwkv_ln_b[i])
        y_swa = swa_mixer(u_swa, rope_cos, rope_sin, swa_sinks[i])
        p_stack = jnp.stack([w_br_ssd[i], w_br_nsa[i], w_br_rwkv[i], w_br_swa[i]]).astype(BF16)
        ys = [y.reshape(t, -1) for y in (y_ssd, y_nsa, y_rwkv, y_swa)]
        xf = merge_branches(xf, g_mix, w_gates, ys, p_stack, w_out[i].astype(BF16))
        j = i // 2
        ple_args = (p[i].reshape(t, -1), ple_gate[i].astype(BF16), ple_proj[i].astype(BF16),
                    norm_final if i == depth - 1 else None)
        if i % 2 == 0:
            xf = dense_ffn(xf, norm_ffn[i], ffn_w_gate[j].astype(BF16), ffn_w_up[j].astype(BF16),
                           ffn_w_down[j].astype(BF16), *ple_args)
        else:
            xf = moe_ffn(xf, norm_ffn[i], moe_router[j], moe_w_gate[j].astype(BF16),
                         moe_w_up[j].astype(BF16), moe_w_down[j].astype(BF16), *ple_args)
    return xf.reshape(b, s, d)
```

```python
import functools

import numpy as np
import jax
import jax.numpy as jnp
from jax import lax
from jax.experimental import pallas as pl
from jax.experimental.pallas import tpu as pltpu

F32 = jnp.float32
BF16 = jnp.bfloat16

D_MODEL = 1024
HEAD_DIM = 64
NORM_EPS = 1e-6
NEG = -1e30
BIG = 1e30

SSD_HEADS = 8
SSD_D_INNER = SSD_HEADS * HEAD_DIM
SSD_STATE = 128
SSD_GROUPS = 2
SSD_CONV = 4
SSD_CHUNK = 128
SSD_CONV_DIM = SSD_D_INNER + 2 * SSD_GROUPS * SSD_STATE

NSA_HEADS = 8
NSA_KV_HEADS = 2
NSA_CMP_BLOCK = 32
NSA_CMP_STRIDE = 16
NSA_CMP_HIDDEN = 64
NSA_SEL_BLOCK = 64
NSA_TOPK = 8
NSA_WINDOW = 512

RWKV_HEADS = 8
RWKV_DIM = RWKV_HEADS * HEAD_DIM
RWKV_W_LORA = 64
RWKV_A_LORA = 64
RWKV_G_LORA = 128
RWKV_IN = 3 * RWKV_DIM + RWKV_W_LORA + RWKV_A_LORA + RWKV_G_LORA
RWKV_GN_EPS = 64e-5

SWA_HEADS = 8
SWA_KV_HEADS = 2
SWA_WINDOW = 128
ROPE_THETA = 150000.0

N_BRANCHES = 4
D_FF = 2816
N_EXPERTS = 8
TOP_K = 2
PLE_DIM = 256

LANES = 128
SUBLANES = 8
VMEM_LIMIT = 56 * 1024 * 1024

HIGHEST = lax.Precision.HIGHEST


def _cparams(sem):
    return pltpu.CompilerParams(dimension_semantics=sem, vmem_limit_bytes=VMEM_LIMIT)


def _sigmoid(x):
    return 1.0 / (1.0 + jnp.exp(-x))


def _silu(x):
    return x * _sigmoid(x)


def _softplus(x):
    return jnp.maximum(x, 0.0) + jnp.log1p(jnp.exp(-jnp.abs(x)))


def _dot(a, b):
    return jnp.dot(a, b, preferred_element_type=F32)


def _dot_nt(a, b):
    return lax.dot_general(a, b, (((1,), (1,)), ((), ())), preferred_element_type=F32)


def _pad_cols(w, n):
    return jnp.pad(w, ((0, 0), (0, n - w.shape[1])))


def _row_tile(t, pref):
    while t % pref:
        pref //= 2
    return pref


def _rms_bf16(x, g):
    ms = jnp.mean(x * x, axis=-1, keepdims=True)
    return (x * lax.rsqrt(ms + NORM_EPS) * g).astype(BF16)


def _norm_matmul_kernel(x_ref, g_ref, w_ref, o_ref):
    o_ref[...] = _dot(_rms_bf16(x_ref[...], g_ref[...]), w_ref[...])


def norm_matmul(x, g, w):
    t, d = x.shape
    n = w.shape[1]
    tm = _row_tile(t, 1024)
    return pl.pallas_call(
        _norm_matmul_kernel,
        name="norm_matmul",
        grid=(t // tm,),
        in_specs=[
            pl.BlockSpec((tm, d), lambda i: (i, 0)),
            pl.BlockSpec((1, d), lambda i: (0, 0)),
            pl.BlockSpec((d, n), lambda i: (0, 0)),
        ],
        out_specs=pl.BlockSpec((tm, n), lambda i: (i, 0)),
        out_shape=jax.ShapeDtypeStruct((t, n), F32),
        compiler_params=_cparams(("parallel",)),
    )(x, g.reshape(1, d), w)


def _merge_kernel(x_ref, g_ref, wg_ref, y0_ref, y1_ref, y2_ref, y3_ref, p_ref, wo_ref, o_ref):
    x = x_ref[...]
    h = _rms_bf16(x, g_ref[...])
    acc = None
    for m, y_ref in enumerate((y0_ref, y1_ref, y2_ref, y3_ref)):
        pm = _dot(y_ref[...].astype(BF16), p_ref[m])
        gm = _sigmoid(_dot(h, wg_ref[:, m * D_MODEL:(m + 1) * D_MODEL]))
        acc = gm * pm if acc is None else acc + gm * pm
    o_ref[...] = x + _dot(acc.astype(BF16), wo_ref[...])


def merge_branches(x, g, w_gates, ys, p_stack, w_out):
    t, d = x.shape
    tm = _row_tile(t, 512)
    dm = ys[0].shape[1]
    row = lambda w: pl.BlockSpec((tm, w), lambda i: (i, 0))
    const = lambda shape: pl.BlockSpec(shape, lambda i: (0,) * len(shape))
    return pl.pallas_call(
        _merge_kernel,
        name="merge",
        grid=(t // tm,),
        in_specs=[row(d), const((1, d)), const((d, N_BRANCHES * d)), row(dm), row(dm), row(dm), row(dm),
                  const((N_BRANCHES, dm, d)), const((d, d))],
        out_specs=row(d),
        out_shape=jax.ShapeDtypeStruct((t, d), F32),
        compiler_params=_cparams(("parallel",)),
    )(x, g.reshape(1, d), w_gates, *ys, p_stack, w_out)


def _ple_epilogue(y, p_ref, pg_ref, pp_ref, nf_ref):
    gate = _sigmoid(_dot(y.astype(BF16), pg_ref[...]))
    y = y + gate * _dot(p_ref[...].astype(BF16), pp_ref[...])
    if nf_ref is not None:
        ms = jnp.mean(y * y, axis=-1, keepdims=True)
        y = y * lax.rsqrt(ms + NORM_EPS) * nf_ref[...]
    return y


def _ple_operands(p, ple_gate, ple_proj, norm_final, tm, index_map):
    d = ple_gate.shape[0]
    const = lambda shape: pl.BlockSpec(shape, lambda *_: (0,) * len(shape))
    specs = [pl.BlockSpec((tm, p.shape[1]), index_map), const((d, d)), const((p.shape[1], d))]
    args = [p, ple_gate, ple_proj]
    if norm_final is not None:
        specs.append(const((1, d)))
        args.append(norm_final.reshape(1, d))
    return specs, args


def _ffn_kernel(x_ref, g_ref, wg_ref, wu_ref, wd_ref, p_ref, pg_ref, pp_ref, *rest):
    nf_ref = rest[0] if len(rest) == 4 else None
    o_ref, h_ref, acc_ref = rest[-3:]
    j = pl.program_id(1)

    @pl.when(j == 0)
    def _():
        h_ref[...] = _rms_bf16(x_ref[...], g_ref[...])
        acc_ref[...] = jnp.zeros_like(acc_ref)

    h = h_ref[...]
    a = _silu(_dot(h, wg_ref[...])) * _dot(h, wu_ref[...])
    acc_ref[...] += _dot(a.astype(BF16), wd_ref[...])

    @pl.when(j == pl.num_programs(1) - 1)
    def _():
        o_ref[...] = _ple_epilogue(x_ref[...] + acc_ref[...], p_ref, pg_ref, pp_ref, nf_ref)


def dense_ffn(x, g, w_gate, w_up, w_down, p, ple_gate, ple_proj, norm_final=None):
    t, d = x.shape
    f = w_gate.shape[1]
    tm = _row_tile(t, 512)
    tf = f // 2 if (f // 2) % LANES == 0 else f
    ple_specs, ple_args = _ple_operands(p, ple_gate, ple_proj, norm_final, tm, lambda i, j: (i, 0))
    return pl.pallas_call(
        _ffn_kernel,
        name="dense_ffn",
        grid=(t // tm, f // tf),
        in_specs=[
            pl.BlockSpec((tm, d), lambda i, j: (i, 0)),
            pl.BlockSpec((1, d), lambda i, j: (0, 0)),
            pl.BlockSpec((d, tf), lambda i, j: (0, j)),
            pl.BlockSpec((d, tf), lambda i, j: (0, j)),
            pl.BlockSpec((tf, d), lambda i, j: (j, 0)),
        ] + ple_specs,
        out_specs=pl.BlockSpec((tm, d), lambda i, j: (i, 0)),
        out_shape=jax.ShapeDtypeStruct((t, d), F32),
        scratch_shapes=[pltpu.VMEM((tm, d), BF16), pltpu.VMEM((tm, d), F32)],
        compiler_params=_cparams(("parallel", "arbitrary")),
    )(x, g.reshape(1, d), w_gate, w_up, w_down, *ple_args)


SSD_U_COLS = SSD_D_INNER + SSD_CONV_DIM + LANES
_SSD_GN = SSD_GROUPS * SSD_STATE


def _ssd_kernel(u_ref, cw_ref, cb_ref, dtb_ref, alog_ref, dsk_ref, nw_ref, tril_ref,
                o_ref, xpad_ref, state_ref, y_ref):
    L = SSD_CHUNK
    P = HEAD_DIM
    R = SSD_HEADS // SSD_GROUPS

    @pl.when(pl.program_id(1) == 0)
    def _():
        xpad_ref[0:SUBLANES, :] = jnp.zeros((SUBLANES, SSD_CONV_DIM), F32)
        state_ref[...] = jnp.zeros_like(state_ref)

    z = u_ref[0, :, 0:SSD_D_INNER]
    xbc = u_ref[0, :, SSD_D_INNER:SSD_D_INNER + SSD_CONV_DIM]
    dt_raw = u_ref[0, :, SSD_D_INNER + SSD_CONV_DIM:SSD_U_COLS]

    xpad_ref[SUBLANES:SUBLANES + L, :] = xbc
    conv = cb_ref[...]
    for j in range(SSD_CONV):
        conv = conv + cw_ref[j:j + 1, :] * xpad_ref[pl.ds(SUBLANES - (SSD_CONV - 1) + j, L), :]
    xpad_ref[0:SUBLANES, :] = xbc[L - SUBLANES:L, :]
    act = _silu(conv)
    xs = act[:, 0:SSD_D_INNER]
    bm = act[:, SSD_D_INNER:SSD_D_INNER + _SSD_GN]
    cm = act[:, SSD_D_INNER + _SSD_GN:SSD_CONV_DIM]

    dt = _softplus(dt_raw + dtb_ref[...])
    a_neg = -jnp.exp(alog_ref[...])
    tril = tril_ref[...]
    a_cum = jnp.dot(tril, dt * a_neg, precision=HIGHEST, preferred_element_type=F32)
    a_cum_t = a_cum.T
    dt_t = dt.T
    lower = tril > 0.5

    for g in range(SSD_GROUPS):
        bg = bm[:, g * SSD_STATE:(g + 1) * SSD_STATE]
        cg = cm[:, g * SSD_STATE:(g + 1) * SSD_STATE]
        bg16 = bg.astype(BF16)
        cg16 = cg.astype(BF16)
        cb = _dot_nt(cg16, bg16)
        bgt16 = bg.T.astype(BF16)
        for r in range(R):
            h = g * R + r
            a_col = a_cum[:, h:h + 1]
            a_row = a_cum_t[h:h + 1, :]
            a_last = a_cum[L - 1:L, h:h + 1]
            xh = xs[:, h * P:(h + 1) * P]
            decay = jnp.where(lower, jnp.exp(jnp.where(lower, a_col - a_row, 0.0)), 0.0)
            w_ls = cb * decay * dt_t[h:h + 1, :]
            y_diag = _dot(w_ls.astype(BF16), xh.astype(BF16))
            xw = xh * (jnp.exp(a_last - a_col) * dt[:, h:h + 1])
            st = _dot(bgt16, xw.astype(BF16))
            prev = state_ref[h]
            y_off = _dot(cg16, prev.astype(BF16)) * jnp.exp(a_col)
            state_ref[h] = prev * jnp.exp(a_last) + st
            y_ref[:, h * P:(h + 1) * P] = y_diag + y_off + dsk_ref[:, h:h + 1] * xh

    yg = y_ref[...] * _silu(z)
    gw = SSD_D_INNER // SSD_GROUPS
    for g in range(SSD_GROUPS):
        part = yg[:, g * gw:(g + 1) * gw]
        ms = jnp.mean(part * part, axis=-1, keepdims=True)
        o_ref[0, :, g * gw:(g + 1) * gw] = part * lax.rsqrt(ms + NORM_EPS) * nw_ref[:, g * gw:(g + 1) * gw]


def _lane_row(v):
    return jnp.pad(v.astype(F32), (0, LANES - v.shape[0])).reshape(1, LANES)


def ssd_mixer(u, conv_w, conv_b, dt_bias, a_log, d_skip, norm_w):
    b, s, _ = u.shape
    L = SSD_CHUNK
    tril = jnp.asarray(np.tril(np.ones((L, L), np.float32)))
    full = lambda shape: pl.BlockSpec(shape, lambda i, c: (0,) * len(shape))
    return pl.pallas_call(
        _ssd_kernel,
        name="ssd",
        grid=(b, s // L),
        in_specs=[
            pl.BlockSpec((1, L, SSD_U_COLS), lambda i, c: (i, c, 0)),
            full((SSD_CONV, SSD_CONV_DIM)), full((1, SSD_CONV_DIM)),
            full((1, LANES)), full((1, LANES)), full((1, LANES)),
            full((1, SSD_D_INNER)), full((L, L)),
        ],
        out_specs=pl.BlockSpec((1, L, SSD_D_INNER), lambda i, c: (i, c, 0)),
        out_shape=jax.ShapeDtypeStruct((b, s, SSD_D_INNER), F32),
        scratch_shapes=[pltpu.VMEM((SUBLANES + L, SSD_CONV_DIM), F32),
                        pltpu.VMEM((SSD_HEADS, SSD_STATE, HEAD_DIM), F32),
                        pltpu.VMEM((L, SSD_D_INNER), F32)],
        compiler_params=_cparams(("parallel", "arbitrary")),
    )(u, conv_w, conv_b.reshape(1, -1), _lane_row(dt_bias), _lane_row(a_log), _lane_row(d_skip),
      norm_w.reshape(1, -1), tril)


ATTN_TQ = 128
SWA_TQ = 256
_SCALE = HEAD_DIM ** -0.5


def _stack_heads(q, g, heads_per_group):
    parts = [q[:, (g * heads_per_group + r) * HEAD_DIM:(g * heads_per_group + r + 1) * HEAD_DIM]
             for r in range(heads_per_group)]
    return jnp.concatenate(parts, axis=0)


def _row_pos(t0, tq, reps):
    row = lax.broadcasted_iota(jnp.int32, (reps * tq, 1), 0)
    return t0 + (row & (tq - 1))


def _rope(x, cosf, sinf):
    n = x.shape[1]
    half = HEAD_DIM // 2
    lane = lax.broadcasted_iota(jnp.int32, x.shape, 1)
    first = (lane & (HEAD_DIM - 1)) < half
    rot = jnp.where(first, pltpu.roll(x, n - half, 1), pltpu.roll(x, half, 1))
    return x * cosf + rot * sinf


def _rope_table_kernel(pos_ref, cos_ref, sin_ref):
    half = HEAD_DIM // 2
    lane = lax.broadcasted_iota(jnp.int32, (1, LANES), 1)
    expo = -(lane & (half - 1)).astype(F32) / half
    inv_freq = jnp.power(jnp.full((1, LANES), ROPE_THETA, F32), expo)
    sign = jnp.where((lane & (HEAD_DIM - 1)) < half, -1.0, 1.0)
    ang = pos_ref[0] * inv_freq
    cos_ref[0] = jnp.cos(ang)
    sin_ref[0] = jnp.sin(ang) * sign


def rope_tables(positions):
    b, s = positions.shape
    tm = _row_tile(s, 512)
    out = pl.BlockSpec((1, tm, LANES), lambda i, j: (i, j, 0))
    return pl.pallas_call(
        _rope_table_kernel,
        name="rope_tables",
        grid=(b, s // tm),
        in_specs=[pl.BlockSpec((1, tm, 1), lambda i, j: (i, j, 0))],
        out_specs=[out, out],
        out_shape=[jax.ShapeDtypeStruct((b, s, LANES), F32)] * 2,
        compiler_params=_cparams(("parallel", "parallel")),
    )(positions.astype(F32).reshape(b, s, 1))


def _swa_kernel(q_ref, kv_ref, cosq_ref, sinq_ref, cosk_ref, sink_ref, sinks_ref, o_ref):
    tq = q_ref.shape[1]
    W = SWA_WINDOW
    span = W + tq
    R = SWA_HEADS // SWA_KV_HEADS
    kw = SWA_KV_HEADS * HEAD_DIM
    t0 = pl.program_id(1) * tq
    start = pl.multiple_of(jnp.maximum(t0 - W, 0), SUBLANES)

    reps = q_ref.shape[2] // LANES
    qr = _rope(q_ref[0], jnp.concatenate([cosq_ref[0]] * reps, 1), jnp.concatenate([sinq_ref[0]] * reps, 1))
    qr = qr * _SCALE
    kvs = kv_ref[0, pl.ds(start, span), :]
    kr = _rope(kvs[:, 0:kw], cosk_ref[0, pl.ds(start, span), :], sink_ref[0, pl.ds(start, span), :])
    v = kvs[:, kw:2 * kw]

    rel = (t0 + lax.broadcasted_iota(jnp.int32, (1, tq), 1)) - (start + lax.broadcasted_iota(jnp.int32, (span, 1), 0))
    bias = jnp.where((rel >= 0) & (rel < W), 0.0, NEG)
    parts = []
    for g in range(SWA_KV_HEADS):
        cols = slice(g * HEAD_DIM, (g + 1) * HEAD_DIM)
        qg = _stack_heads(qr, g, R).astype(BF16)
        s = _dot_nt(kr[:, cols].astype(BF16), qg) + jnp.concatenate([bias] * R, axis=1)
        snk = jnp.concatenate([jnp.full((1, tq), sinks_ref[g * R + r], F32) for r in range(R)], axis=1)
        m = jnp.maximum(jnp.max(s, axis=0, keepdims=True), snk)
        e = jnp.exp((s - m).astype(BF16))
        v1 = jnp.concatenate([v[:, cols].astype(BF16), jnp.ones((span, HEAD_DIM), BF16)], axis=1)
        o_t = lax.dot_general(v1, e, (((0,), (0,)), ((), ())), preferred_element_type=F32)
        den = o_t[HEAD_DIM:HEAD_DIM + 1] + jnp.exp(snk - m)
        parts.append(o_t[0:HEAD_DIM] / den)
    o = jnp.concatenate(parts, axis=0).T
    for g in range(SWA_KV_HEADS):
        for r in range(R):
            h = g * R + r
            o_ref[0, :, h * HEAD_DIM:(h + 1) * HEAD_DIM] = o[r * tq:(r + 1) * tq, g * HEAD_DIM:(g + 1) * HEAD_DIM]


def swa_mixer(u, rope_cos, rope_sin, sinks):
    b, s, _ = u.shape
    tq = SWA_TQ
    qw = SWA_HEADS * HEAD_DIM
    kvw = 2 * SWA_KV_HEADS * HEAD_DIM
    tile = pl.BlockSpec((1, tq, LANES), lambda i, j: (i, j, 0))
    full = pl.BlockSpec((1, s, LANES), lambda i, j: (i, 0, 0))
    return pl.pallas_call(
        _swa_kernel,
        name="swa",
        grid=(b, s // tq),
        in_specs=[
            pl.BlockSpec((1, tq, qw), lambda i, j: (i, j, 0)),
            pl.BlockSpec((1, s, kvw), lambda i, j: (i, 0, qw // kvw)),
            tile, tile, full, full,
            pl.BlockSpec(memory_space=pltpu.SMEM),
        ],
        out_specs=pl.BlockSpec((1, tq, qw), lambda i, j: (i, j, 0)),
        out_shape=jax.ShapeDtypeStruct((b, s, qw), F32),
        compiler_params=_cparams(("parallel", "arbitrary")),
    )(u, u, rope_cos, rope_sin, rope_cos, rope_sin, sinks.astype(F32))


NSA_U_COLS = NSA_HEADS * HEAD_DIM + 6 * NSA_KV_HEADS * HEAD_DIM + LANES
_NSA_KVW = NSA_KV_HEADS * HEAD_DIM
_NSA_QBLK = NSA_HEADS * HEAD_DIM // _NSA_KVW
_NSA_R = NSA_HEADS // NSA_KV_HEADS
_CMP_HALF = NSA_CMP_BLOCK // 2
NSA_KEY_CHUNK = 256
NSA_SELECT_TQ = 512


def _nsa_compress_kernel(k_ref, v_ref, pe_ref, w1_ref, w2_ref, kc_ref, vc_ref):
    nc = kc_ref.shape[1]
    for idx, (x_ref, o_ref) in enumerate(((k_ref, kc_ref), (v_ref, vc_ref))):
        ha = jnp.zeros((nc, _NSA_KVW), F32)
        hb = jnp.zeros((nc, _NSA_KVW), F32)
        for l in range(_CMP_HALF):
            y = x_ref[0, pl.ds(l, nc, stride=NSA_CMP_STRIDE), :]
            ha = ha + _dot((y + pe_ref[idx, l:l + 1, :]).astype(BF16), w1_ref[idx, l])
            hb = hb + _dot((y + pe_ref[idx, _CMP_HALF + l:_CMP_HALF + l + 1, :]).astype(BF16),
                           w1_ref[idx, _CMP_HALF + l])
        hid = _silu(ha + pltpu.roll(hb, nc - 1, 0))
        o_ref[0] = _dot(hid.astype(BF16), w2_ref[idx])


def _nsa_select_kernel(q_ref, kc_ref, vc_ref, ov_ref, ocmp_ref, sel_ref):
    tq = q_ref.shape[1]
    nc = kc_ref.shape[1]
    nsel = ov_ref.shape[0]
    k_eff = min(NSA_TOPK, nsel)
    t0 = pl.program_id(1) * tq
    t_row = t0 + lax.broadcasted_iota(jnp.int32, (1, tq), 1)
    cmp_end = lax.broadcasted_iota(jnp.int32, (nc, 1), 0) * NSA_CMP_STRIDE + (NSA_CMP_BLOCK - 1)
    cvalid = jnp.concatenate([cmp_end <= t_row] * _NSA_R, axis=1)
    jidx = lax.broadcasted_iota(jnp.int32, (nsel, 1), 0)
    blk_t = t_row // NSA_SEL_BLOCK
    jvalid = jidx <= blk_t
    forced = (jidx == 0) | (jidx == blk_t)
    q = q_ref[0] * _SCALE
    sel_rows = []
    o_parts = []
    for g in range(NSA_KV_HEADS):
        qg = _stack_heads(q, g, _NSA_R).astype(BF16)
        kc = kc_ref[0, :, g * HEAD_DIM:(g + 1) * HEAD_DIM].astype(BF16)
        vc = vc_ref[0, :, g * HEAD_DIM:(g + 1) * HEAD_DIM].astype(BF16)
        s = _dot_nt(kc, qg)
        m = jnp.max(jnp.where(cvalid, s, NEG), axis=0, keepdims=True)
        e = jnp.exp(jnp.where(cvalid, s - m, NEG))
        den = jnp.sum(e, axis=0, keepdims=True)
        p = e / jnp.where(den > 0.0, den, 1.0)
        o_parts.append(lax.dot_general(vc, p.astype(BF16), (((0,), (0,)), ((), ())),
                                       preferred_element_type=F32))
        psum = p[:, 0:tq]
        for r in range(1, _NSA_R):
            psum = psum + p[:, r * tq:(r + 1) * tq]
        imp_t = jnp.dot(ov_ref[...], psum, precision=HIGHEST, preferred_element_type=F32)
        score = jnp.where(forced, BIG, jnp.where(jvalid, imp_t, NEG))
        cnt = jnp.zeros((nsel, tq), F32)
        for i in range(nsel):
            si = score[i:i + 1, :]
            beats = (si > score) | ((si == score) & (jidx > i))
            cnt = cnt + jnp.where(beats, 1.0, 0.0)
        sel_rows.append(jnp.where(cnt < k_eff, 1.0, 0.0))
    pad = LANES - NSA_KV_HEADS * nsel
    sel_t = jnp.concatenate(sel_rows + [jnp.zeros((pad, tq), F32)], axis=0)
    sel_ref[0] = sel_t.T
    o = jnp.concatenate(o_parts, axis=0).T
    for g in range(NSA_KV_HEADS):
        for r in range(_NSA_R):
            h = g * _NSA_R + r
            ocmp_ref[0, :, h * HEAD_DIM:(h + 1) * HEAD_DIM] = o[r * tq:(r + 1) * tq, g * HEAD_DIM:(g + 1) * HEAD_DIM]


def _nsa_attend_kernel(q_ref, sel_ref, gate_ref, ocmp_ref, ks_ref, vs_ref, kw_ref, vw_ref, ext_ref, o_ref):
    tq = q_ref.shape[1]
    s_len = ks_ref.shape[1]
    W = NSA_WINDOW
    span = min(W + tq, s_len)
    t0 = pl.program_id(1) * tq
    start = pl.multiple_of(jnp.maximum(t0 - W, 0), SUBLANES)
    t_row = t0 + lax.broadcasted_iota(jnp.int32, (1, tq), 1)
    rel = t_row - (start + lax.broadcasted_iota(jnp.int32, (span, 1), 0))
    win_bias = jnp.where((rel >= 0) & (rel < W), 0.0, NEG)
    q = q_ref[0] * _SCALE
    sig = _sigmoid(gate_ref[0])
    sel16 = sel_ref[0].astype(BF16)

    def attend_t(qg, k, v, bias_t):
        s = _dot_nt(k.astype(BF16), qg) + jnp.concatenate([bias_t] * _NSA_R, axis=1)
        e = jnp.exp((s - jnp.max(s, axis=0, keepdims=True)).astype(BF16))
        v1 = jnp.concatenate([v.astype(BF16), jnp.ones(v.shape, BF16)], axis=1)
        o_t = lax.dot_general(v1, e, (((0,), (0,)), ((), ())), preferred_element_type=F32)
        return o_t[0:HEAD_DIM] / o_t[HEAD_DIM:HEAD_DIM + 1]

    chunk = min(NSA_KEY_CHUNK, s_len)
    for c in range(s_len // chunk):
        klen = (c + 1) * chunk

        @pl.when(t0 // chunk == c)
        def _():
            kidx = lax.broadcasted_iota(jnp.int32, (klen, 1), 0)
            sel_t, win_t = [], []
            for g in range(NSA_KV_HEADS):
                cols = slice(g * HEAD_DIM, (g + 1) * HEAD_DIM)
                qg = _stack_heads(q, g, _NSA_R).astype(BF16)
                chosen = _dot_nt(ext_ref[g, 0:klen, :], sel16) > 0.5
                sel_bias = jnp.where(chosen & (kidx <= t_row), 0.0, NEG)
                sel_t.append(attend_t(qg, ks_ref[0, 0:klen, cols], vs_ref[0, 0:klen, cols], sel_bias))
                win_t.append(attend_t(qg, kw_ref[0, pl.ds(start, span), cols],
                                      vw_ref[0, pl.ds(start, span), cols], win_bias))
            o_sel = jnp.concatenate(sel_t, axis=0).T
            o_win = jnp.concatenate(win_t, axis=0).T
            for g in range(NSA_KV_HEADS):
                for r in range(_NSA_R):
                    h = g * _NSA_R + r
                    hc = slice(h * HEAD_DIM, (h + 1) * HEAD_DIM)
                    rows = slice(r * tq, (r + 1) * tq)
                    gc = slice(g * HEAD_DIM, (g + 1) * HEAD_DIM)
                    o_ref[0, :, hc] = (sig[:, h:h + 1] * ocmp_ref[0, :, hc]
                                       + sig[:, NSA_HEADS + h:NSA_HEADS + h + 1] * o_sel[rows, gc]
                                       + sig[:, 2 * NSA_HEADS + h:2 * NSA_HEADS + h + 1] * o_win[rows, gc])


def _block_diag2(w):
    z = jnp.zeros_like(w)
    return jnp.concatenate([jnp.concatenate([w, z], -1), jnp.concatenate([z, w], -1)], -2)


def nsa_mixer(u, cmp_pe, cmp_w1, cmp_w2):
    b, s, _ = u.shape
    tq = ATTN_TQ
    nc = s // NSA_CMP_STRIDE
    nsel = s // NSA_SEL_BLOCK
    qw = NSA_HEADS * HEAD_DIM
    pe2 = jnp.concatenate([cmp_pe, cmp_pe], -1)
    w1bd = _block_diag2(cmp_w1).astype(BF16)
    w2bd = _block_diag2(cmp_w2).astype(BF16)
    const = lambda shape: pl.BlockSpec(shape, lambda *_: (0,) * len(shape))
    col = lambda c: pl.BlockSpec((1, s, _NSA_KVW), lambda i, *_: (i, 0, _NSA_QBLK + c))

    kc, vc = pl.pallas_call(
        _nsa_compress_kernel,
        name="nsa_compress",
        grid=(b,),
        in_specs=[col(0), col(1), const(pe2.shape), const(w1bd.shape), const(w2bd.shape)],
        out_specs=[pl.BlockSpec((1, nc, _NSA_KVW), lambda i: (i, 0, 0))] * 2,
        out_shape=[jax.ShapeDtypeStruct((b, nc, _NSA_KVW), F32)] * 2,
        compiler_params=_cparams(("parallel",)),
    )(u, u, pe2, w1bd, w2bd)

    c_start = np.arange(nc) * NSA_CMP_STRIDE
    s_start = np.arange(nsel) * NSA_SEL_BLOCK
    ov_t = ((c_start[None, :] < s_start[:, None] + NSA_SEL_BLOCK)
            & (c_start[None, :] + NSA_CMP_BLOCK > s_start[:, None])
            & (np.arange(nc)[None, :] < nc - 1)).astype(np.float32)
    qspec = pl.BlockSpec((1, tq, qw), lambda i, j: (i, j, 0))
    ts = _row_tile(s, NSA_SELECT_TQ)
    sel_q = pl.BlockSpec((1, ts, qw), lambda i, j: (i, j, 0))
    o_cmp, sel = pl.pallas_call(
        _nsa_select_kernel,
        name="nsa_select",
        grid=(b, s // ts),
        in_specs=[sel_q,
                  pl.BlockSpec((1, nc, _NSA_KVW), lambda i, j: (i, 0, 0)),
                  pl.BlockSpec((1, nc, _NSA_KVW), lambda i, j: (i, 0, 0)),
                  const(ov_t.shape)],
        out_specs=[sel_q, pl.BlockSpec((1, ts, LANES), lambda i, j: (i, j, 0))],
        out_shape=[jax.ShapeDtypeStruct((b, s, qw), F32), jax.ShapeDtypeStruct((b, s, LANES), F32)],
        compiler_params=_cparams(("parallel", "arbitrary")),
    )(u, kc, vc, jnp.asarray(ov_t))

    expand = np.zeros((NSA_KV_HEADS, s, LANES), np.float32)
    for g in range(NSA_KV_HEADS):
        expand[g, np.arange(s), g * nsel + np.arange(s) // NSA_SEL_BLOCK] = 1.0
    gate_blk = NSA_U_COLS // LANES - 1
    return pl.pallas_call(
        _nsa_attend_kernel,
        name="nsa_attend",
        grid=(b, s // tq),
        in_specs=[qspec,
                  pl.BlockSpec((1, tq, LANES), lambda i, j: (i, j, 0)),
                  pl.BlockSpec((1, tq, LANES), lambda i, j: (i, j, gate_blk)),
                  qspec,
                  col(2), col(3), col(4), col(5),
                  const(expand.shape)],
        out_specs=qspec,
        out_shape=jax.ShapeDtypeStruct((b, s, qw), F32),
        compiler_params=_cparams(("parallel", "arbitrary")),
    )(u, sel, u, o_cmp, u, u, u, u, jnp.asarray(expand, BF16))


def _seg_sum(x, ones16):
    hi = x.astype(BF16)
    lo = (x - hi.astype(F32)).astype(BF16)
    return _dot(hi, ones16) + _dot(lo, ones16)


def _head_ones(width):
    idx = np.arange(width) // HEAD_DIM
    return jnp.asarray((idx[:, None] == idx[None, :]).astype(np.float32), BF16)


_RW_R, _RW_K, _RW_V = 0, RWKV_DIM, 2 * RWKV_DIM
_RW_WD = 3 * RWKV_DIM
_RW_AD = _RW_WD + RWKV_W_LORA
_RW_GD = _RW_AD + RWKV_A_LORA


def _rwkv_prep_kernel(u_ref, mu_ref, w0_ref, wup_ref, a0_ref, aup_ref, gup_ref, kk_ref, ka_ref, rk_ref,
                      ones_ref, r_o, w_o, k_o, v_o, kk_o, q_o, g_o, bonus_o, up_ref):
    tm = u_ref.shape[1]

    @pl.when(pl.program_id(1) == 0)
    def _():
        up_ref[0:SUBLANES, :] = jnp.zeros((SUBLANES, RWKV_IN), F32)

    u = u_ref[0]
    up_ref[SUBLANES:SUBLANES + tm, :] = u
    prev = up_ref[pl.ds(SUBLANES - 1, tm), :]
    up_ref[0:SUBLANES, :] = u[tm - SUBLANES:tm, :]
    x = u + (prev - u) * mu_ref[...]
    r = x[:, _RW_R:_RW_R + RWKV_DIM]
    k = x[:, _RW_K:_RW_K + RWKV_DIM]
    v = x[:, _RW_V:_RW_V + RWKV_DIM]
    wd = x[:, _RW_WD:_RW_AD]
    ad = x[:, _RW_AD:_RW_GD]
    gd = x[:, _RW_GD:RWKV_IN]
    w = -_softplus(-(w0_ref[...] + _dot(jnp.tanh(wd).astype(BF16), wup_ref[...]))) - 0.5
    a = _sigmoid(a0_ref[...] + _dot(ad.astype(BF16), aup_ref[...]))
    ones16 = ones_ref[...]
    kk = k * kk_ref[...]
    kk = kk / jnp.maximum(jnp.sqrt(_seg_sum(kk * kk, ones16)), 1e-12)
    k2 = k * (1.0 + (a - 1.0) * ka_ref[...])
    r_o[0] = r
    w_o[0] = jnp.exp(-jnp.exp(w))
    k_o[0] = k2
    v_o[0] = v
    kk_o[0] = kk
    q_o[0] = kk * a
    g_o[0] = _dot(_sigmoid(gd).astype(BF16), gup_ref[...])
    bonus_o[0] = _seg_sum(r * k2 * rk_ref[...], ones16) * v


_RWS_TB = 64
_RWS_VB = HEAD_DIM // 2 // SUBLANES
_RWS_NACC = 2
_RWS_VECS = 5


def _rwkv_scan_vhalf_kernel(kk_ref, w_ref, q_ref, k_ref, r_ref, v_ref, o_ref, s_ref, rows_ref):
    @pl.when(pl.program_id(0) == 0)
    def _():
        s_ref[...] = jnp.zeros_like(s_ref)

    low = lax.broadcasted_iota(jnp.int32, (SUBLANES, LANES), 1) < LANES // 2

    def reduce_acc(acc, vb):
        tot = acc[(vb, 0)]
        for i in range(1, _RWS_NACC):
            tot = tot + acc[(vb, i)]
        return tot

    def group(tg, carry):
        t8 = pl.multiple_of(tg * SUBLANES, SUBLANES)
        for a, ref in enumerate((kk_ref, w_ref, q_ref, k_ref, r_ref)):
            for i in range(HEAD_DIM // 2):
                x = ref[i, pl.ds(t8, SUBLANES), :]
                xr = pltpu.roll(x, LANES // 2, 1)
                rows_ref[a, 0, i] = jnp.where(low, x, xr)
                rows_ref[a, 1, i] = jnp.where(low, xr, x)
        for tl in range(SUBLANES):
            t = t8 + tl
            key_row = lambda a, k: rows_ref[a, k % 2, k // 2, tl:tl + 1, :]
            acc = {}
            for k in range(HEAD_DIM):
                kk_row = key_row(0, k)
                for vb in range(_RWS_VB):
                    term = s_ref[vb, k] * kk_row
                    key = (vb, k % _RWS_NACC)
                    acc[key] = acc[key] + term if key in acc else term
            sa = [reduce_acc(acc, vb) for vb in range(_RWS_VB)]
            vt = [v_ref[t, vb * SUBLANES:(vb + 1) * SUBLANES, :] for vb in range(_RWS_VB)]
            acc = {}
            for k in range(HEAD_DIM):
                w_row, q_row, k_row, r_row = (key_row(a, k) for a in range(1, _RWS_VECS))
                for vb in range(_RWS_VB):
                    st = s_ref[vb, k] * w_row - sa[vb] * q_row + vt[vb] * k_row
                    s_ref[vb, k] = st
                    term = st * r_row
                    key = (vb, k % _RWS_NACC)
                    acc[key] = acc[key] + term if key in acc else term
            for vb in range(_RWS_VB):
                o_ref[t, vb * SUBLANES:(vb + 1) * SUBLANES, :] = reduce_acc(acc, vb)
        return carry

    lax.fori_loop(0, o_ref.shape[0] // SUBLANES, group, 0)


_RL_TB = 128
_RL_ROWS = HEAD_DIM // 2


def _scan_row_base(i, half, interleaved):
    return 2 * i + half if interleaved else half * _RL_ROWS + i


def _to_scan_kernel(*refs, interleaved):
    n = len(interleaved)
    ins, outs, y_ref = refs[:n], refs[n:2 * n], refs[2 * n]
    nb, tb = ins[0].shape[0], ins[0].shape[1]
    for a in range(n):
        for b in range(nb):
            y_ref[b] = ins[a][b].T
        for i in range(_RL_ROWS):
            parts = [y_ref[b, pl.ds(_scan_row_base(i, half, interleaved[a]), RWKV_HEADS, stride=HEAD_DIM), :]
                     for half in range(2) for b in range(nb)]
            rows = jnp.concatenate(parts, axis=0).T
            if interleaved[a]:
                outs[a][i] = rows
            else:
                outs[a][pl.ds(i, tb, stride=_RL_ROWS), :] = rows


def _from_scan_kernel(o_ref, bonus_ref, g_ref, lnw_ref, lnb_ref, out_ref, y_ref):
    nb, tb = out_ref.shape[0], out_ref.shape[1]
    for i in range(_RL_ROWS):
        zt = o_ref[pl.ds(i, tb, stride=_RL_ROWS), :].T
        for half in range(2):
            for b in range(nb):
                r0 = (half * nb + b) * RWKV_HEADS
                y_ref[b, pl.ds(_scan_row_base(i, half, False), RWKV_HEADS, stride=HEAD_DIM), :] = zt[r0:r0 + RWKV_HEADS]
    for b in range(nb):
        for h in range(RWKV_HEADS):
            rows = slice(h * HEAD_DIM, (h + 1) * HEAD_DIM)
            blk = y_ref[b, rows, :]
            cen = blk - jnp.mean(blk, axis=0, keepdims=True)
            var = jnp.mean(cen * cen, axis=0, keepdims=True)
            y_ref[b, rows, :] = cen * lax.rsqrt(var + RWKV_GN_EPS)
        o = y_ref[b].T * lnw_ref[...] + lnb_ref[...]
        out_ref[b] = (o + bonus_ref[b]) * g_ref[b]


def to_scan_layout(arrays, interleaved):
    b, s, dm = arrays[0].shape
    tb = _row_tile(s, _RL_TB)
    n = len(arrays)
    key_spec = pl.BlockSpec((_RL_ROWS, tb, LANES), lambda i: (0, i, 0))
    val_spec = pl.BlockSpec((tb * _RL_ROWS, LANES), lambda i: (i, 0))
    key_shape = jax.ShapeDtypeStruct((_RL_ROWS, s, LANES), F32)
    val_shape = jax.ShapeDtypeStruct((s * _RL_ROWS, LANES), F32)
    outs = pl.pallas_call(
        functools.partial(_to_scan_kernel, interleaved=tuple(interleaved)),
        name="rwkv_to_scan",
        grid=(s // tb,),
        in_specs=[pl.BlockSpec((b, tb, dm), lambda i: (0, i, 0))] * n,
        out_specs=[key_spec if f else val_spec for f in interleaved],
        out_shape=[key_shape if f else val_shape for f in interleaved],
        scratch_shapes=[pltpu.VMEM((b, dm, tb), F32)],
        compiler_params=_cparams(("parallel",)),
    )(*arrays)
    return [o if f else o.reshape(s, _RL_ROWS, LANES) for o, f in zip(outs, interleaved)]


def from_scan_layout(o, bonus, g, ln_w, ln_b):
    s = o.shape[0]
    b, _, dm = bonus.shape
    tb = _row_tile(s, _RL_TB)
    tile = pl.BlockSpec((b, tb, dm), lambda i: (0, i, 0))
    vec = pl.BlockSpec((1, dm), lambda i: (0, 0))
    return pl.pallas_call(
        _from_scan_kernel,
        name="rwkv_from_scan",
        grid=(s // tb,),
        in_specs=[pl.BlockSpec((tb * _RL_ROWS, LANES), lambda i: (i, 0)), tile, tile, vec, vec],
        out_specs=tile,
        out_shape=jax.ShapeDtypeStruct((b, s, dm), F32),
        scratch_shapes=[pltpu.VMEM((b, dm, tb), F32)],
        compiler_params=_cparams(("parallel",)),
    )(o.reshape(s * _RL_ROWS, LANES), bonus, g, ln_w.astype(F32).reshape(1, dm), ln_b.astype(F32).reshape(1, dm))


def rwkv7_mixer(u, mu, w0, w_up, a0, a_up, g_up, k_k, k_a, r_k, ln_w, ln_b):
    b, s, _ = u.shape
    dm = RWKV_DIM
    tm = _row_tile(s, 256)
    row = lambda v: v.astype(F32).reshape(1, -1)
    const = lambda shape: pl.BlockSpec(shape, lambda *_: (0,) * len(shape))
    ones_d = _head_ones(dm)
    tile = pl.BlockSpec((1, tm, dm), lambda i, j: (i, j, 0))
    outs = pl.pallas_call(
        _rwkv_prep_kernel,
        name="rwkv_prep",
        grid=(b, s // tm),
        in_specs=[pl.BlockSpec((1, tm, RWKV_IN), lambda i, j: (i, j, 0)),
                  const((1, RWKV_IN)), const((1, dm)), const((RWKV_W_LORA, dm)), const((1, dm)),
                  const((RWKV_A_LORA, dm)), const((RWKV_G_LORA, dm)), const((1, dm)), const((1, dm)),
                  const((1, dm)), const((dm, dm))],
        out_specs=[tile] * 8,
        out_shape=[jax.ShapeDtypeStruct((b, s, dm), F32)] * 8,
        scratch_shapes=[pltpu.VMEM((SUBLANES + tm, RWKV_IN), F32)],
        compiler_params=_cparams(("parallel", "arbitrary")),
    )(u, row(mu), row(w0), w_up.astype(BF16), row(a0), a_up.astype(BF16), g_up.astype(BF16),
      row(k_k), row(k_a), row(r_k), ones_d)
    r, wdec, k2, v, kk, q, g, bonus = outs

    nchain = b * RWKV_HEADS
    assert 2 * nchain == LANES and s % 2 == 0, "the scan kernel maps (half, batch, head) onto the 128 lanes"
    scan_in = to_scan_layout([kk, wdec, q, k2, r, v], interleaved=[True] * _RWS_VECS + [False])
    tb = _row_tile(s, _RWS_TB)
    kblk = pl.BlockSpec((_RL_ROWS, tb, LANES), lambda i: (0, i, 0))
    vblk = pl.BlockSpec((tb, _RL_ROWS, LANES), lambda i: (i, 0, 0))
    o = pl.pallas_call(
        _rwkv_scan_vhalf_kernel,
        name="rwkv_scan",
        grid=(s // tb,),
        in_specs=[kblk] * _RWS_VECS + [vblk],
        out_specs=vblk,
        out_shape=jax.ShapeDtypeStruct((s, _RL_ROWS, LANES), F32),
        scratch_shapes=[pltpu.VMEM((_RWS_VB, HEAD_DIM, SUBLANES, LANES), F32),
                        pltpu.VMEM((_RWS_VECS, 2, _RL_ROWS, SUBLANES, LANES), F32)],
        compiler_params=_cparams(("arbitrary",)),
    )(*scan_in)
    return from_scan_layout(o, bonus, g, ln_w, ln_b)


MOE_SLOT_BLOCK = 512
MOE_FF_CHUNK = 256
MOE_ROW_TILE = 512
MOE_DMA_UNROLL = 8
_META_E0, _META_E1, _META_R0, _META_R1, _META_W0, _META_W1 = range(6)


def _moe_route_kernel(x_ref, g_ref, wr_ref, tril_ref, h_o, meta_o, cnt_o, carry_ref):
    @pl.when(pl.program_id(0) == 0)
    def _():
        carry_ref[...] = jnp.zeros_like(carry_ref)

    x = x_ref[...]
    ms = jnp.mean(x * x, axis=-1, keepdims=True)
    h = x * lax.rsqrt(ms + NORM_EPS) * g_ref[...]
    h_o[...] = h
    logits = jnp.dot(h, wr_ref[...], precision=HIGHEST, preferred_element_type=F32)
    lane = lax.broadcasted_iota(jnp.int32, logits.shape, 1)
    l1 = jnp.where(lane < N_EXPERTS, logits, NEG)
    m1 = jnp.max(l1, axis=-1, keepdims=True)
    i1 = jnp.min(jnp.where(l1 == m1, lane, LANES), axis=-1, keepdims=True)
    l2 = jnp.where(lane == i1, NEG, l1)
    m2 = jnp.max(l2, axis=-1, keepdims=True)
    i2 = jnp.min(jnp.where(l2 == m2, lane, LANES), axis=-1, keepdims=True)
    e21 = jnp.exp(m2 - m1)
    w1 = 1.0 / (1.0 + e21)
    w2 = e21 * w1
    cnt = jnp.where((lane == i1) | (lane == i2), 1.0, 0.0)
    before = _dot(tril_ref[...], cnt.astype(BF16)) + carry_ref[...]
    r1 = jnp.sum(jnp.where(lane == i1, before, 0.0), axis=-1, keepdims=True)
    r2 = jnp.sum(jnp.where(lane == i2, before, 0.0), axis=-1, keepdims=True)
    carry_ref[...] += jnp.sum(cnt, axis=0, keepdims=True)
    cnt_o[...] = carry_ref[...]
    meta = jnp.zeros(logits.shape, F32)
    for idx, val in ((_META_E0, i1.astype(F32)), (_META_E1, i2.astype(F32)), (_META_R0, r1),
                     (_META_R1, r2), (_META_W0, w1), (_META_W1, w2)):
        meta = jnp.where(lane == idx, val, meta)
    meta_o[...] = meta


def _row_copy(src_ref, src_row, dst_ref, dst_row, sem):
    return pltpu.make_async_copy(src_ref.at[pl.ds(src_row, 1), :], dst_ref.at[pl.ds(dst_row, 1), :], sem)


def _moe_scatter_kernel(d0_ref, d1_ref, h_ref, xs_in_ref, xs_ref, sem):
    del xs_in_ref
    tm = h_ref.shape[0]

    def issue(r, c):
        _row_copy(h_ref, r, xs_ref, d0_ref[r], sem).start(priority=0)
        _row_copy(h_ref, r, xs_ref, d1_ref[r], sem).start(priority=1)
        return c

    def drain(r, c):
        _row_copy(h_ref, r, xs_ref, d0_ref[r], sem).wait()
        _row_copy(h_ref, r, xs_ref, d1_ref[r], sem).wait()
        return c

    lax.fori_loop(0, tm, issue, 0, unroll=MOE_DMA_UNROLL)
    lax.fori_loop(0, tm, drain, 0, unroll=MOE_DMA_UNROLL)


def _moe_expert_kernel(be_ref, nused_ref, xs_ref, wg_ref, wu_ref, wd_ref, ys_ref):
    del be_ref
    i = pl.program_id(0)

    @pl.when(i < nused_ref[0])
    def _():
        x16 = xs_ref[...].astype(BF16)
        acc = jnp.zeros(ys_ref.shape, F32)
        for c in range(D_FF // MOE_FF_CHUNK):
            cols = slice(c * MOE_FF_CHUNK, (c + 1) * MOE_FF_CHUNK)
            a = _silu(_dot(x16, wg_ref[0, :, cols])) * _dot(x16, wu_ref[0, :, cols])
            acc = acc + _dot(a.astype(BF16), wd_ref[0, cols, :])
        ys_ref[...] = acc

    @pl.when(i >= nused_ref[0])
    def _():
        ys_ref[...] = jnp.zeros_like(ys_ref)


def _moe_combine_kernel(d0_ref, d1_ref, x_ref, meta_ref, ys_ref, p_ref, pg_ref, pp_ref, *rest):
    nf_ref = rest[0] if len(rest) == 5 else None
    o_ref, b0_ref, b1_ref, sem = rest[-4:]
    tm = x_ref.shape[0]

    def issue(r, c):
        _row_copy(ys_ref, d0_ref[r], b0_ref, r, sem).start(priority=0)
        _row_copy(ys_ref, d1_ref[r], b1_ref, r, sem).start(priority=1)
        return c

    def drain(r, c):
        _row_copy(ys_ref, d0_ref[r], b0_ref, r, sem).wait()
        _row_copy(ys_ref, d1_ref[r], b1_ref, r, sem).wait()
        return c

    lax.fori_loop(0, tm, issue, 0, unroll=MOE_DMA_UNROLL)
    lax.fori_loop(0, tm, drain, 0, unroll=MOE_DMA_UNROLL)
    w0 = meta_ref[:, _META_W0:_META_W0 + 1]
    w1 = meta_ref[:, _META_W1:_META_W1 + 1]
    y = x_ref[...] + (w0 * b0_ref[...] + w1 * b1_ref[...])
    o_ref[...] = _ple_epilogue(y, p_ref, pg_ref, pp_ref, nf_ref)


def moe_ffn(x, g, router, w_gate, w_up, w_down, p, ple_gate, ple_proj, norm_final=None):
    t, d = x.shape
    tm = _row_tile(t, MOE_ROW_TILE)
    blk = MOE_SLOT_BLOCK
    nblk = -(-(t * TOP_K + N_EXPERTS * (blk - 1)) // blk)
    slots = nblk * blk
    tril = jnp.asarray(np.tril(np.ones((tm, tm), np.float32), -1), BF16)
    const = lambda shape: pl.BlockSpec(shape, lambda *_: (0,) * len(shape))
    row = lambda w: pl.BlockSpec((tm, w), lambda i: (i, 0))
    h, meta, counts = pl.pallas_call(
        _moe_route_kernel,
        name="moe_route",
        grid=(t // tm,),
        in_specs=[row(d), const((1, d)), const((d, LANES)), const((tm, tm))],
        out_specs=[row(d), row(LANES), const((1, LANES))],
        out_shape=[jax.ShapeDtypeStruct((t, d), F32), jax.ShapeDtypeStruct((t, LANES), F32),
                   jax.ShapeDtypeStruct((1, LANES), F32)],
        scratch_shapes=[pltpu.VMEM((1, LANES), F32)],
        compiler_params=_cparams(("arbitrary",)),
    )(x, g.reshape(1, d), _pad_cols(router.astype(F32), LANES), tril)

    cnt = counts[0, :N_EXPERTS].astype(jnp.int32)
    pcnt = (cnt + blk - 1) // blk * blk
    pend = jnp.cumsum(pcnt)
    pstart = pend - pcnt
    e0 = meta[:, _META_E0].astype(jnp.int32)
    e1 = meta[:, _META_E1].astype(jnp.int32)
    d0 = pstart[e0] + meta[:, _META_R0].astype(jnp.int32)
    d1 = pstart[e1] + meta[:, _META_R1].astype(jnp.int32)
    blk_start = jnp.arange(nblk, dtype=jnp.int32) * blk
    blk_e = jnp.minimum(jnp.sum((pend[None, :] <= blk_start[:, None]).astype(jnp.int32), axis=1),
                        N_EXPERTS - 1).astype(jnp.int32)
    nused = (pend[-1:] // blk).astype(jnp.int32)

    smem_rows = pl.BlockSpec((tm,), lambda i: (i,), memory_space=pltpu.SMEM)
    xs = pl.pallas_call(
        _moe_scatter_kernel,
        name="moe_scatter",
        grid=(t // tm,),
        in_specs=[smem_rows, smem_rows, row(d), pl.BlockSpec(memory_space=pl.ANY)],
        out_specs=pl.BlockSpec(memory_space=pl.ANY),
        out_shape=jax.ShapeDtypeStruct((slots, d), F32),
        scratch_shapes=[pltpu.SemaphoreType.DMA(())],
        input_output_aliases={3: 0},
        compiler_params=_cparams(("arbitrary",)),
    )(d0, d1, h, jnp.zeros((slots, d), F32))

    f = w_gate.shape[2]
    ys = pl.pallas_call(
        _moe_expert_kernel,
        name="moe_expert",
        grid_spec=pltpu.PrefetchScalarGridSpec(
            num_scalar_prefetch=2,
            grid=(nblk,),
            in_specs=[pl.BlockSpec((blk, d), lambda i, be, nu: (i, 0)),
                      pl.BlockSpec((1, d, f), lambda i, be, nu: (be[i], 0, 0)),
                      pl.BlockSpec((1, d, f), lambda i, be, nu: (be[i], 0, 0)),
                      pl.BlockSpec((1, f, d), lambda i, be, nu: (be[i], 0, 0))],
            out_specs=pl.BlockSpec((blk, d), lambda i, be, nu: (i, 0)),
        ),
        out_shape=jax.ShapeDtypeStruct((slots, d), F32),
        compiler_params=_cparams(("arbitrary",)),
    )(blk_e, nused, xs, w_gate, w_up, w_down)

    ple_specs, ple_args = _ple_operands(p, ple_gate, ple_proj, norm_final, tm, lambda i: (i, 0))
    return pl.pallas_call(
        _moe_combine_kernel,
        name="moe_combine",
        grid=(t // tm,),
        in_specs=[smem_rows, smem_rows, row(d), row(LANES), pl.BlockSpec(memory_space=pl.ANY)] + ple_specs,
        out_specs=row(d),
        out_shape=jax.ShapeDtypeStruct((t, d), F32),
        scratch_shapes=[pltpu.VMEM((tm, d), F32), pltpu.VMEM((tm, d), F32), pltpu.SemaphoreType.DMA(())],
        compiler_params=_cparams(("arbitrary",)),
    )(d0, d1, x, meta, ys, *ple_args)


_IN_SPLITS = (SSD_D_INNER, SSD_CONV_DIM, SSD_HEADS,
              NSA_HEADS * HEAD_DIM, 6 * NSA_KV_HEADS * HEAD_DIM, 3 * NSA_HEADS,
              RWKV_IN,
              SWA_HEADS * HEAD_DIM, 2 * SWA_KV_HEADS * HEAD_DIM,
              N_BRANCHES * D_MODEL)
_IN_OFF = tuple(int(o) for o in np.cumsum((0,) + _IN_SPLITS))


def _split_in_proj(w):
    o = _IN_OFF
    seg = lambda a, b: w[:, o[a]:o[b]]
    w_ssd = jnp.concatenate([seg(0, 2), _pad_cols(seg(2, 3), LANES)], axis=1)
    w_nsa = jnp.concatenate([seg(3, 5), _pad_cols(seg(5, 6), LANES)], axis=1)
    return tuple(m.astype(BF16) for m in (w_ssd, w_nsa, seg(6, 7), seg(7, 9), seg(9, 10)))


def kernel(x, p, positions, norm_mix, w_in, ssd_conv_w, ssd_conv_b, ssd_dt_bias, ssd_a_log, ssd_d, ssd_norm, nsa_cmp_pe, nsa_cmp_w1, nsa_cmp_w2, rwkv_mu, rwkv_w0, rwkv_w_up, rwkv_a0, rwkv_a_up, rwkv_g_up, rwkv_k_k, rwkv_k_a, rwkv_r_k, rwkv_ln_w, rwkv_ln_b, swa_sinks, w_br_ssd, w_br_nsa, w_br_rwkv, w_br_swa, w_out, norm_ffn, ffn_w_gate, ffn_w_up, ffn_w_down, moe_router, moe_w_gate, moe_w_up, moe_w_down, ple_proj, ple_gate, norm_final):
    b, s, d = x.shape
    t = b * s
    depth = w_in.shape[0]
    xf = x.reshape(t, d)
    rope_cos, rope_sin = rope_tables(positions)
    for i in range(depth):
        w_ssd, w_nsa, w_rwkv, w_swa, w_gates = _split_in_proj(w_in[i])
        g_mix = norm_mix[i]
        u_ssd = norm_matmul(xf, g_mix, w_ssd).reshape(b, s, -1)
        u_nsa = norm_matmul(xf, g_mix, w_nsa).reshape(b, s, -1)
        u_rwkv = norm_matmul(xf, g_mix, w_rwkv).reshape(b, s, -1)
        u_swa = norm_matmul(xf, g_mix, w_swa).reshape(b, s, -1)
        y_ssd = ssd_mixer(u_ssd, ssd_conv_w[i], ssd_conv_b[i], ssd_dt_bias[i], ssd_a_log[i], ssd_d[i], ssd_norm[i])
        y_nsa = nsa_mixer(u_nsa, nsa_cmp_pe[i], nsa_cmp_w1[i], nsa_cmp_w2[i])
        y_rwkv = rwkv7_mixer(u_rwkv, rwkv_mu[i], rwkv_w0[i], rwkv_w_up[i], rwkv_a0[i], rwkv_a_up[i],
                             rwkv_g_up[i], rwkv_k_k[i], rwkv_k_a[i], rwkv_r_k[i], rwkv_ln_w[i], rwkv_ln_b[i])
        y_swa = swa_mixer(u_swa, rope_cos, rope_sin, swa_sinks[i])
        p_stack = jnp.stack([w_br_ssd[i], w_br_nsa[i], w_br_rwkv[i], w_br_swa[i]]).astype(BF16)
        ys = [y.reshape(t, -1) for y in (y_ssd, y_nsa, y_rwkv, y_swa)]
        xf = merge_branches(xf, g_mix, w_gates, ys, p_stack, w_out[i].astype(BF16))
        j = i // 2
        ple_args = (p[i].reshape(t, -1), ple_gate[i].astype(BF16), ple_proj[i].astype(BF16),
                    norm_final if i == depth - 1 else None)
        if i % 2 == 0:
            xf = dense_ffn(xf, norm_ffn[i], ffn_w_gate[j].astype(BF16), ffn_w_up[j].astype(BF16),
                           ffn_w_down[j].astype(BF16), *ple_args)
        else:
            xf = moe_ffn(xf, norm_ffn[i], moe_router[j], moe_w_gate[j].astype(BF16),
                         moe_w_up[j].astype(BF16), moe_w_down[j].astype(BF16), *ple_args)
    return xf.reshape(b, s, d)
```
